```python
import math
import jax
import jax.numpy as jnp
from jax import lax

D_MODEL = 1024
BATCH = 16
SEQ = 2048
DEPTH = 2

NORM_EPS = 1e-6
GLA_HEADS = 4
GLA_DK = 64
GLA_DV = 128
GLA_QK = GLA_HEADS * GLA_DK
GLA_V = GLA_HEADS * GLA_DV
GLA_LOWRANK = 16
GLA_GATE_NORMALIZER = 16.0
GLA_CHUNK = 64
SGU_GROUPS = 4
SGU_GROUP_DIM = 128
SGU_DIM = SGU_GROUPS * SGU_GROUP_DIM
SGU_CHUNK = 128
AB_SPLIT = (GLA_QK, GLA_QK, GLA_V, GLA_V, GLA_LOWRANK, GLA_LOWRANK, SGU_DIM, SGU_DIM)
AB_IN_DIM = sum(AB_SPLIT)
AB_MIX_DIM = GLA_V + SGU_DIM
GDN_HEADS = 8
GDN_DK = 128
GDN_DV = 128
GDN_QK = GDN_HEADS * GDN_DK
GDN_V = GDN_HEADS * GDN_DV
GDN_CONV = 3
GDN_CHUNK = 64
GDN_CONV_DIM = 2 * GDN_QK + GDN_V
GDN_SPLIT = (GDN_CONV_DIM, GDN_V, GDN_HEADS, GDN_HEADS, GDN_HEADS, GDN_HEADS)
GDN_IN_DIM = sum(GDN_SPLIT)
FFN_DIM = 2816
FFN_CONV = 3
N_EVEN = (DEPTH + 1) // 2
N_ODD = DEPTH // 2

kernel_name = "hybrid_gla_sgu_gdn_encoder"


def rms_norm(x, gain):
    xf = x.astype(jnp.float32)
    y = xf * lax.rsqrt(jnp.mean(xf * xf, axis=-1, keepdims=True) + NORM_EPS)
    return (y * gain.astype(jnp.float32)).astype(x.dtype)


def layer_norm(x, gain, bias):
    xf = x.astype(jnp.float32)
    mu = jnp.mean(xf, axis=-1, keepdims=True)
    var = jnp.mean(jnp.square(xf - mu), axis=-1, keepdims=True)
    y = (xf - mu) * lax.rsqrt(var + NORM_EPS)
    return y * gain.astype(jnp.float32) + bias.astype(jnp.float32)


def l2_normalize(x):
    return x * lax.rsqrt(jnp.sum(x * x, axis=-1, keepdims=True) + NORM_EPS)


def split_cols(t, sizes):
    out, start = [], 0
    for s in sizes:
        out.append(t[..., start:start + s])
        start += s
    return out


def depthwise_conv(x, w):
    c = x.shape[-1]
    return lax.conv_general_dilated(
        x, w[:, None, :].astype(x.dtype), window_strides=(1,), padding='SAME',
        dimension_numbers=('NWC', 'WIO', 'NWC'), feature_group_count=c)


def gla_scan(q, k, v, log_a):
    b_, s_, h_, dk = q.shape
    dv = v.shape[-1]
    c = GLA_CHUNK
    n = s_ // c
    q, k, v, log_a = (t.reshape(b_, n, c, h_, t.shape[-1]) for t in (q, k, v, log_a))
    cum = jnp.cumsum(log_a, axis=2)
    last = cum[:, :, -1:]
    q_dec = q * jnp.exp(cum)
    k_inv = k * jnp.exp(-cum)
    k_end = k * jnp.exp(last - cum)
    incl_lower = jnp.tril(jnp.ones((c, c), dtype=bool))
    scores = jnp.einsum('bnihd,bnjhd->bnhij', q_dec, k_inv)
    scores = jnp.where(incl_lower, scores, 0.0)
    o_intra = jnp.einsum('bnhij,bnjhv->bnihv', scores, v)
    chunk_kv = jnp.einsum('bnjhd,bnjhv->nbhdv', k_end, v)
    chunk_decay = jnp.exp(last[:, :, 0]).transpose(1, 0, 2, 3)

    def step(state, inp):
        decay, kv = inp
        return state * decay[..., None] + kv, state

    _, states = lax.scan(step, jnp.zeros((b_, h_, dk, dv), q.dtype), (chunk_decay, chunk_kv))
    o_inter = jnp.einsum('bnihd,nbhdv->bnihv', q_dec, states)
    return (o_intra + o_inter).reshape(b_, s_, h_, dv)


def gated_delta_scan(q, k, v, beta, g):
    b_, s_, h_, dk = q.shape
    dv = v.shape[-1]
    c = GDN_CHUNK
    n = s_ // c
    q, k, v = (t.reshape(b_, n, c, h_, t.shape[-1]) for t in (q, k, v))
    beta = beta.reshape(b_, n, c, h_)
    cum = jnp.cumsum(g.reshape(b_, n, c, h_), axis=2)
    cum_h = cum.transpose(0, 1, 3, 2)
    incl_lower = jnp.tril(jnp.ones((c, c), dtype=bool))
    strict_lower = jnp.tril(jnp.ones((c, c), dtype=bool), -1)
    diff = cum_h[..., :, None] - cum_h[..., None, :]
    decay = jnp.exp(jnp.where(incl_lower, diff, -jnp.inf))
    kb = k * beta[..., None]
    a_kk = jnp.einsum('bnihd,bnjhd->bnhij', kb, k) * decay
    tmat = jnp.where(strict_lower, a_kk, 0.0) + jnp.eye(c, dtype=q.dtype)
    rhs = jnp.concatenate([v * beta[..., None], kb * jnp.exp(cum)[..., None]], axis=-1)
    rhs = rhs.transpose(0, 1, 3, 2, 4)
    sol = lax.linalg.triangular_solve(tmat, rhs, left_side=True, lower=True, unit_diagonal=True)
    u = sol[..., :dv]
    w = sol[..., dv:]
    a_qk = jnp.einsum('bnihd,bnjhd->bnhij', q, k) * decay
    q_dec = (q * jnp.exp(cum)[..., None]).transpose(0, 1, 3, 2, 4)
    k_end = (k * jnp.exp(cum[:, :, -1:] - cum)[..., None]).transpose(0, 1, 3, 2, 4)
    chunk_decay = jnp.exp(cum[:, :, -1])
    xs = tuple(jnp.moveaxis(t, 1, 0) for t in (u, w, q_dec, a_qk, k_end, chunk_decay))

    def step(state, inp):
        u_c, w_c, qd_c, aqk_c, ke_c, dec_c = inp
        v_new = u_c - jnp.einsum('bhid,bhdv->bhiv', w_c, state)
        o = jnp.einsum('bhid,bhdv->bhiv', qd_c, state) + jnp.einsum('bhij,bhjv->bhiv', aqk_c, v_new)
        state = state * dec_c[..., None, None] + jnp.einsum('bhjd,bhjv->bhdv', ke_c, v_new)
        return state, o

    _, o = lax.scan(step, jnp.zeros((b_, h_, dk, dv), q.dtype), xs)
    return o.transpose(1, 0, 3, 2, 4).reshape(b_, s_, h_, dv)


def gla_sgu_mixer(h, w_in, w_gate_fwd, b_gate_fwd, w_gate_bwd, b_gate_bwd, gla_norm,
                  sgu_ln_g, sgu_ln_b, sgu_w_s, sgu_b_s, w_out):
    f32 = jnp.float32
    b_, s_, _ = h.shape
    q, k, v, gate, lr_f, lr_b, su, sv = split_cols(h @ w_in, AB_SPLIT)
    qh = (q.astype(f32) * GLA_DK ** -0.5).reshape(b_, s_, GLA_HEADS, GLA_DK)
    kh = k.astype(f32).reshape(b_, s_, GLA_HEADS, GLA_DK)
    vh = v.astype(f32).reshape(b_, s_, GLA_HEADS, GLA_DV)

    def log_decay(lr, w, b):
        z = lr.astype(f32) @ w.astype(f32) + b.astype(f32)
        return (jax.nn.log_sigmoid(z) / GLA_GATE_NORMALIZER).reshape(b_, s_, GLA_HEADS, GLA_DK)

    o_f = gla_scan(qh, kh, vh, log_decay(lr_f, w_gate_fwd, b_gate_fwd))
    o_b = jnp.flip(gla_scan(jnp.flip(qh, 1), jnp.flip(kh, 1), jnp.flip(vh, 1),
                            jnp.flip(log_decay(lr_b, w_gate_bwd, b_gate_bwd), 1)), 1)
    o_a = rms_norm(o_f + o_b, gla_norm).reshape(b_, s_, GLA_V) * jax.nn.silu(gate.astype(f32))
    u = jax.nn.gelu(su.astype(f32))
    vv = layer_norm(jax.nn.gelu(sv.astype(f32)), sgu_ln_g, sgu_ln_b)
    vv = vv.reshape(b_, s_ // SGU_CHUNK, SGU_CHUNK, SGU_GROUPS, SGU_GROUP_DIM)
    mixed = jnp.einsum('gij,bnjgc->bnigc', sgu_w_s.astype(f32), vv) \
        + sgu_b_s.astype(f32).T[None, None, :, :, None]
    o_b_mix = u * mixed.reshape(b_, s_, SGU_DIM)
    return jnp.concatenate([o_a, o_b_mix], axis=-1).astype(h.dtype) @ w_out


def gdn_mixer(h, w_in, conv_w, a_log_fwd, dt_bias_fwd, a_log_bwd, dt_bias_bwd, norm_g, w_out):
    f32 = jnp.float32
    b_, s_, _ = h.shape
    qkv, z, beta_f, beta_b, a_f, a_b = split_cols(h @ w_in, GDN_SPLIT)
    qkv = jax.nn.silu(depthwise_conv(qkv, conv_w)).astype(f32)
    q, k, v = split_cols(qkv, (GDN_QK, GDN_QK, GDN_V))
    q = l2_normalize(q.reshape(b_, s_, GDN_HEADS, GDN_DK)) * GDN_DK ** -0.5
    k = l2_normalize(k.reshape(b_, s_, GDN_HEADS, GDN_DK))
    v = v.reshape(b_, s_, GDN_HEADS, GDN_DV)

    def log_decay(a, a_log, dt_bias):
        return -jnp.exp(a_log.astype(f32)) * jax.nn.softplus(a.astype(f32) + dt_bias.astype(f32))

    o_f = gated_delta_scan(q, k, v, jax.nn.sigmoid(beta_f.astype(f32)),
                           log_decay(a_f, a_log_fwd, dt_bias_fwd))
    o_b = jnp.flip(gated_delta_scan(jnp.flip(q, 1), jnp.flip(k, 1), jnp.flip(v, 1),
                                    jnp.flip(jax.nn.sigmoid(beta_b.astype(f32)), 1),
                                    jnp.flip(log_decay(a_b, a_log_bwd, dt_bias_bwd), 1)), 1)
    zg = jax.nn.silu(z.astype(f32).reshape(b_, s_, GDN_HEADS, GDN_DV))
    o = rms_norm(o_f + o_b, norm_g) * zg
    return o.reshape(b_, s_, GDN_V).astype(h.dtype) @ w_out


def conv_glu_ffn(h, w_up, conv_w, conv_b, w_down):
    gate, up = split_cols(h @ w_up, (FFN_DIM, FFN_DIM))
    gate = depthwise_conv(gate, conv_w) + conv_b.astype(gate.dtype)
    return (jax.nn.silu(gate) * up) @ w_down


def setup_inputs(seed: int = 0) -> dict:
    key = jax.random.key(seed)
    keys = list(jax.random.split(key, 32))

    def normal(shape, scale):
        return jax.random.normal(keys.pop(), shape, jnp.float32) * scale

    def gain(shape):
        return 1.0 + normal(shape, 0.02)

    def a_log():
        return jnp.log(jax.random.uniform(keys.pop(), (N_ODD, GDN_HEADS), jnp.float32, 1.0, 16.0))

    def dt_bias():
        dt = jnp.exp(jax.random.uniform(keys.pop(), (N_ODD, GDN_HEADS), jnp.float32,
                                        math.log(1e-3), math.log(1e-1)))
        return jnp.log(jnp.expm1(dt))

    return {
        'x': normal((BATCH, SEQ, D_MODEL), 1.0),
        'norm_mix': gain((DEPTH, D_MODEL)),
        'norm_ffn': gain((DEPTH, D_MODEL)),
        'norm_final': gain((D_MODEL,)),
        'ab_w_in': normal((N_EVEN, D_MODEL, AB_IN_DIM), D_MODEL ** -0.5),
        'gla_w_gate_fwd': normal((N_EVEN, GLA_LOWRANK, GLA_QK), GLA_LOWRANK ** -0.5),
        'gla_b_gate_fwd': normal((N_EVEN, GLA_QK), 0.1),
        'gla_w_gate_bwd': normal((N_EVEN, GLA_LOWRANK, GLA_QK), GLA_LOWRANK ** -0.5),
        'gla_b_gate_bwd': normal((N_EVEN, GLA_QK), 0.1),
        'gla_norm': gain((N_EVEN, GLA_DV)),
        'sgu_ln_g': gain((N_EVEN, SGU_DIM)),
        'sgu_ln_b': normal((N_EVEN, SGU_DIM), 0.02),
        'sgu_w_s': normal((N_EVEN, SGU_GROUPS, SGU_CHUNK, SGU_CHUNK), SGU_CHUNK ** -0.5),
        'sgu_b_s': gain((N_EVEN, SGU_GROUPS, SGU_CHUNK)),
        'ab_w_out': normal((N_EVEN, AB_MIX_DIM, D_MODEL), AB_MIX_DIM ** -0.5),
        'gdn_w_in': normal((N_ODD, D_MODEL, GDN_IN_DIM), D_MODEL ** -0.5),
        'gdn_conv_w': normal((N_ODD, GDN_CONV, GDN_CONV_DIM), GDN_CONV ** -0.5),
        'gdn_a_log_fwd': a_log(),
        'gdn_dt_bias_fwd': dt_bias(),
        'gdn_a_log_bwd': a_log(),
        'gdn_dt_bias_bwd': dt_bias(),
        'gdn_norm': gain((N_ODD, GDN_DV)),
        'gdn_w_out': normal((N_ODD, GDN_V, D_MODEL), GDN_V ** -0.5),
        'ffn_w_up': normal((DEPTH, D_MODEL, 2 * FFN_DIM), D_MODEL ** -0.5),
        'ffn_conv_w': normal((DEPTH, FFN_CONV, FFN_DIM), FFN_CONV ** -0.5),
        'ffn_conv_b': normal((DEPTH, FFN_DIM), 0.02),
        'ffn_w_down': normal((DEPTH, FFN_DIM, D_MODEL), FFN_DIM ** -0.5),
    }


def reference(x, norm_mix, norm_ffn, norm_final, ab_w_in, gla_w_gate_fwd, gla_b_gate_fwd,
              gla_w_gate_bwd, gla_b_gate_bwd, gla_norm, sgu_ln_g, sgu_ln_b, sgu_w_s, sgu_b_s,
              ab_w_out, gdn_w_in, gdn_conv_w, gdn_a_log_fwd, gdn_dt_bias_fwd, gdn_a_log_bwd,
              gdn_dt_bias_bwd, gdn_norm, gdn_w_out, ffn_w_up, ffn_conv_w, ffn_conv_b, ffn_w_down):
    h = x
    for layer in range(DEPTH):
        i = layer // 2
        hn = rms_norm(h, norm_mix[layer])
        if layer % 2 == 0:
            mix = gla_sgu_mixer(hn, ab_w_in[i], gla_w_gate_fwd[i], gla_b_gate_fwd[i],
                                gla_w_gate_bwd[i], gla_b_gate_bwd[i], gla_norm[i],
                                sgu_ln_g[i], sgu_ln_b[i], sgu_w_s[i], sgu_b_s[i], ab_w_out[i])
        else:
            mix = gdn_mixer(hn, gdn_w_in[i], gdn_conv_w[i], gdn_a_log_fwd[i], gdn_dt_bias_fwd[i],
                            gdn_a_log_bwd[i], gdn_dt_bias_bwd[i], gdn_norm[i], gdn_w_out[i])
        h = h + mix
        hn = rms_norm(h, norm_ffn[layer])
        h = h + conv_glu_ffn(hn, ffn_w_up[layer], ffn_conv_w[layer], ffn_conv_b[layer], ffn_w_down[layer])
    return rms_norm(h, norm_final)
```

```python
import functools

import jax
import jax.numpy as jnp
from jax import lax
from jax.experimental import pallas as pl
from jax.experimental.pallas import tpu as pltpu

F32 = jnp.float32
BF16 = jnp.bfloat16

NORM_EPS = 1e-6
GLA_HEADS = 4
GLA_DK = 64
GLA_DV = 128
GLA_QK = GLA_HEADS * GLA_DK
GLA_V = GLA_HEADS * GLA_DV
GLA_LOWRANK = 16
GLA_GATE_NORMALIZER = 16.0
SGU_GROUPS = 4
SGU_GROUP_DIM = 128
SGU_DIM = SGU_GROUPS * SGU_GROUP_DIM
SGU_CHUNK = 128
GDN_HEADS = 8
GDN_DK = 128
GDN_DV = 128
GDN_QK = GDN_HEADS * GDN_DK
GDN_V = GDN_HEADS * GDN_DV
GDN_CONV_DIM = 2 * GDN_QK + GDN_V
FFN_DIM = 2816

LANES = 128
SUBLANES_F32 = 8
CHUNK = 64
SMALL_W = LANES

ROW_TILE = 512
COL_TILE = 256
HALO = SUBLANES_F32
VMEM_LIMIT = 56 * 1024 * 1024


def _dot(a, b):
    return jnp.dot(a, b, preferred_element_type=F32)


def _dot_nt(a, b):
    return lax.dot_general(a, b, (((1,), (1,)), ((), ())), preferred_element_type=F32)


def _dot_tn(a, b):
    return lax.dot_general(a, b, (((0,), (0,)), ((), ())), preferred_element_type=F32)


def _split(a):
    hi = a.astype(BF16)
    lo = (a - hi.astype(F32)).astype(BF16)
    return hi, lo


def _dot_exact_lhs(l_bf16, a):
    hi, lo = _split(a)
    return _dot(l_bf16, hi) + _dot(l_bf16, lo)


def _dot_exact_rhs(a, r_bf16):
    hi, lo = _split(a)
    return _dot(hi, r_bf16) + _dot(lo, r_bf16)


def _dot3(a, b):
    ah, al = _split(a)
    bh, bl = _split(b)
    return _dot(ah, bh) + _dot(ah, bl) + _dot(al, bh)


def _rms(x, gain):
    ms = jnp.mean(x * x, axis=-1, keepdims=True)
    return x * lax.rsqrt(ms + NORM_EPS) * gain


def _sigmoid(x):
    return 1.0 / (1.0 + jnp.exp(-x))


def _silu(x):
    return x * _sigmoid(x)


def _softplus(x):
    return jnp.maximum(x, 0.0) + jnp.log(1.0 + jnp.exp(-jnp.abs(x)))


def _gelu_tanh(x):
    c = 0.7978845608028654
    return 0.5 * x * (1.0 + jnp.tanh(c * (x + 0.044715 * (x * x * x))))


def _iota(shape, dim):
    return lax.broadcasted_iota(jnp.int32, shape, dim)


def _shift_rows(g, first_row, last_row):
    n = g.shape[0]
    row = _iota(g.shape, 0)
    g_prev = jnp.where(row == 0, first_row, pltpu.roll(g, 1, axis=0))
    g_next = jnp.where(row == n - 1, last_row, pltpu.roll(g, n - 1, axis=0))
    return g_prev, g_next


def _halo_rows(gh, tiles_per_seq):
    i = pl.program_id(0)
    pos = i % tiles_per_seq
    keep_prev = (pos != 0).astype(F32)
    keep_next = (pos != tiles_per_seq - 1).astype(F32)
    prev_row = gh[HALO - 1:HALO, :] * keep_prev
    next_row = gh[HALO:HALO + 1, :] * keep_next
    return prev_row, next_row


def _normed_with_halo(x_ref, xp_ref, xn_ref, g_ref):
    g = g_ref[...]
    hn = _rms(x_ref[...], g).astype(BF16)
    halo = jnp.concatenate([xp_ref[...], xn_ref[...]], axis=0)
    hh = _rms(halo, g).astype(BF16)
    return hn, jnp.concatenate([hn, hh], axis=0)


def _norm_proj_kernel(*refs, n_main, conv_cols, tiles_per_seq):
    if conv_cols:
        x_ref, xp_ref, xn_ref, g_ref, w_ref, cw_ref, o_ref, s_ref = refs
        hn, hx = _normed_with_halo(x_ref, xp_ref, xn_ref, g_ref)
    else:
        x_ref, g_ref, w_ref, o_ref, s_ref = refs
        hn = _rms(x_ref[...], g_ref[...]).astype(BF16)
    tm = hn.shape[0]
    for c in range(0, n_main, COL_TILE):
        cs = slice(c, c + COL_TILE)
        if c < conv_cols:
            acc = _dot(hx, w_ref[:, cs])
            gp = acc[:tm]
            prev_row, next_row = _halo_rows(acc[tm:], tiles_per_seq)
            g_prev, g_next = _shift_rows(gp, prev_row, next_row)
            cw = cw_ref[:, cs]
            y = cw[0:1] * g_prev + cw[1:2] * gp + cw[2:3] * g_next
            o_ref[:, cs] = _silu(y).astype(o_ref.dtype)
        else:
            o_ref[:, cs] = _dot(hn, w_ref[:, cs]).astype(o_ref.dtype)
    s_ref[...] = _dot(hn, w_ref[:, n_main:])


def _halo_specs(d, n_rows):
    blocks_per_tile = ROW_TILE // HALO
    last = n_rows // HALO - 1
    prev = pl.BlockSpec((HALO, d), lambda i: (jnp.maximum(i * blocks_per_tile - 1, 0), 0))
    nxt = pl.BlockSpec((HALO, d), lambda i: (jnp.minimum((i + 1) * blocks_per_tile, last), 0))
    return prev, nxt


def _resident(shape):
    return pl.BlockSpec(shape, lambda i: (0,) * len(shape), pipeline_mode=pl.Buffered(1))


def _norm_proj(x2d, gain, w, conv_w, seq_len, n_main, conv_cols, name):
    t, d = x2d.shape
    n_all = w.shape[1]
    row = pl.BlockSpec((ROW_TILE, d), lambda i: (i, 0))
    in_specs = [row]
    args = [x2d]
    if conv_cols:
        prev, nxt = _halo_specs(d, t)
        in_specs += [prev, nxt]
        args += [x2d, x2d]
    in_specs += [_resident((1, d)), _resident((d, n_all))]
    args += [gain, w]
    if conv_cols:
        in_specs.append(_resident(conv_w.shape))
        args.append(conv_w)
    kern = functools.partial(_norm_proj_kernel, n_main=n_main, conv_cols=conv_cols,
                             tiles_per_seq=seq_len // ROW_TILE)
    return pl.pallas_call(
        kern,
        grid=(t // ROW_TILE,),
        in_specs=in_specs,
        out_specs=[pl.BlockSpec((ROW_TILE, n_main), lambda i: (i, 0)),
                   pl.BlockSpec((ROW_TILE, SMALL_W), lambda i: (i, 0))],
        out_shape=[jax.ShapeDtypeStruct((t, n_main), BF16),
                   jax.ShapeDtypeStruct((t, SMALL_W), F32)],
        compiler_params=pltpu.CompilerParams(dimension_semantics=("arbitrary",),
                                             vmem_limit_bytes=VMEM_LIMIT),
        name=name,
    )(*args)


def _ffn_kernel(*refs, tiles_per_seq, final_norm):
    if final_norm:
        x_ref, xp_ref, xn_ref, g_ref, wup_ref, cw_ref, cb_ref, wdn_ref, gf_ref, o_ref, act_ref = refs
    else:
        x_ref, xp_ref, xn_ref, g_ref, wup_ref, cw_ref, cb_ref, wdn_ref, o_ref, act_ref = refs
    hn, hx = _normed_with_halo(x_ref, xp_ref, xn_ref, g_ref)
    tm = hn.shape[0]
    for c in range(0, FFN_DIM, COL_TILE):
        cs = slice(c, c + COL_TILE)
        acc = _dot(hx, wup_ref[:, cs])
        up = _dot(hn, wup_ref[:, FFN_DIM + c:FFN_DIM + c + COL_TILE])
        gp = acc[:tm]
        prev_row, next_row = _halo_rows(acc[tm:], tiles_per_seq)
        g_prev, g_next = _shift_rows(gp, prev_row, next_row)
        cw = cw_ref[:, cs]
        gate = cw[0:1] * g_prev + cw[1:2] * gp + cw[2:3] * g_next + cb_ref[:, cs]
        act_ref[:, cs] = (_silu(gate) * up).astype(BF16)
    out = x_ref[...] + _dot(act_ref[...], wdn_ref[...])
    if final_norm:
        out = _rms(out, gf_ref[...])
    o_ref[...] = out


def _ffn(h2d, gain, w_up, conv_w, conv_b, w_down, seq_len, final_gain, name):
    t, d = h2d.shape
    row = pl.BlockSpec((ROW_TILE, d), lambda i: (i, 0))
    prev, nxt = _halo_specs(d, t)
    in_specs = [row, prev, nxt, _resident((1, d)), _resident(w_up.shape), _resident(conv_w.shape),
                _resident(conv_b.shape), _resident(w_down.shape)]
    args = [h2d, h2d, h2d, gain, w_up, conv_w, conv_b, w_down]
    if final_gain is not None:
        in_specs.append(_resident((1, d)))
        args.append(final_gain)
    kern = functools.partial(_ffn_kernel, tiles_per_seq=seq_len // ROW_TILE,
                             final_norm=final_gain is not None)
    return pl.pallas_call(
        kern,
        grid=(t // ROW_TILE,),
        in_specs=in_specs,
        out_specs=row,
        out_shape=jax.ShapeDtypeStruct((t, d), F32),
        scratch_shapes=[pltpu.VMEM((ROW_TILE, FFN_DIM), BF16)],
        compiler_params=pltpu.CompilerParams(dimension_semantics=("arbitrary",),
                                             vmem_limit_bytes=VMEM_LIMIT),
        name=name,
    )(*args)


def _pair_rows(x):
    lo = _iota(x.shape, 1) < CHUNK
    return jnp.concatenate([jnp.where(lo, x, 0.0), jnp.where(lo, 0.0, x)], axis=0)


def _pair_blockdiag(x):
    lo = _iota(x.shape, 1) < LANES
    return jnp.concatenate([jnp.where(lo, x, 0.0), jnp.where(lo, 0.0, x)], axis=0)


def _tri_ones(lower):
    r = _iota((CHUNK, CHUNK), 0)
    c = _iota((CHUNK, CHUNK), 1)
    keep = (c <= r) if lower else (c >= r)
    return jnp.where(keep, 1.0, 0.0).astype(BF16)


def _packed_tri_mask(lower, strict=False):
    r = _iota((CHUNK, LANES), 0)
    c = _iota((CHUNK, LANES), 1) & (CHUNK - 1)
    if lower:
        return (c < r) if strict else (c <= r)
    return (c > r) if strict else (c >= r)


def _gla_kernel(q_ref, k_ref, v_ref, gate_ref, lr_ref, wg_ref, bg_ref, gn_ref, o_ref,
                la_ref, vt_ref, of_ref, ob_ref, st_ref, *, seq_len):
    n_pairs = GLA_HEADS // 2
    n_dbl = seq_len // (2 * CHUNK)
    blk = 2 * CHUNK

    def prep(rb, carry):
        r0 = pl.multiple_of(rb * blk, blk)
        rows = pl.ds(r0, blk)
        z = _dot3(lr_ref[rows, :], wg_ref[...]) + bg_ref[...]
        log_sig = jnp.minimum(z, 0.0) - jnp.log(1.0 + jnp.exp(-jnp.abs(z)))
        la_ref[rows, :] = log_sig * (1.0 / GLA_GATE_NORMALIZER)
        for cb in range(GLA_V // LANES):
            vblk = v_ref[rows, cb * LANES:(cb + 1) * LANES].astype(F32)
            vt_ref[rb, cb * LANES:(cb + 1) * LANES, :] = vblk.T.astype(BF16)
        return carry

    lax.fori_loop(0, n_dbl, prep, 0)
    st_ref[...] = jnp.zeros(st_ref.shape, F32)

    tri = (_tri_ones(True), _tri_ones(False))
    masks = (_packed_tri_mask(True), _packed_tri_mask(False))
    st_mask = (_iota((2 * GLA_DV, LANES), 0) // GLA_DV) == (_iota((2 * GLA_DV, LANES), 1) // GLA_DK)

    def chunk_step(db, parity, dirn, o_dst):
        r0 = pl.multiple_of(db * blk + parity * CHUNK, CHUNK)
        rows = pl.ds(r0, CHUNK)
        la = la_ref[rows, dirn * GLA_QK:(dirn + 1) * GLA_QK]
        cum = _dot_exact_lhs(tri[dirn], la)
        tot = cum[CHUNK - 1:CHUNK, :] if dirn == 0 else cum[0:1, :]
        q = q_ref[rows, :].astype(F32) * (GLA_DK ** -0.5)
        k = k_ref[rows, :].astype(F32)
        q_dec = q * jnp.exp(cum)
        k_inv = k * jnp.exp(-cum)
        k_end = k * jnp.exp(tot - cum)
        dec = jnp.exp(tot)
        for p in range(n_pairs):
            ls = slice(p * LANES, (p + 1) * LANES)
            vs = slice(p * 2 * GLA_DV, (p + 1) * 2 * GLA_DV)
            qd = q_dec[:, ls].astype(BF16)
            kbd = _pair_rows(k_inv[:, ls]).astype(BF16)
            sc = _dot_nt(qd, kbd)
            sc = jnp.where(masks[dirn], sc, 0.0).astype(BF16)
            vbd = _pair_blockdiag(v_ref[rows, vs].astype(F32)).astype(BF16)
            state = st_ref[dirn, p]
            o_dst[rows, vs] = _dot(sc, vbd) + _dot_nt(qd, state.astype(BF16))
            ke = k_end[:, ls].astype(BF16)
            zero = jnp.zeros_like(ke)
            rhs = jnp.concatenate([ke, zero] if parity == 0 else [zero, ke], axis=0)
            upd = _dot(vt_ref[db, vs, :], rhs)
            st_ref[dirn, p] = state * dec[:, ls] + jnp.where(st_mask, upd, 0.0)

    def body(it, carry):
        db_f = it
        db_b = n_dbl - 1 - it
        chunk_step(db_f, 0, 0, of_ref)
        chunk_step(db_b, 1, 1, ob_ref)
        chunk_step(db_f, 1, 0, of_ref)
        chunk_step(db_b, 0, 1, ob_ref)
        return carry

    lax.fori_loop(0, n_dbl, body, 0)

    def finish(rb, carry):
        r0 = pl.multiple_of(rb * blk, blk)
        rows = pl.ds(r0, blk)
        for h in range(GLA_HEADS):
            hs = slice(h * GLA_DV, (h + 1) * GLA_DV)
            o = of_ref[rows, hs] + ob_ref[rows, hs]
            y = _rms(o, gn_ref[...]) * _silu(gate_ref[rows, hs].astype(F32))
            o_ref[rows, hs] = y.astype(o_ref.dtype)
        return carry

    lax.fori_loop(0, n_dbl, finish, 0)


def _gla(p_main, p_small, wg, bg, gnorm, batch, seq_len):
    t = batch * seq_len
    kern = functools.partial(_gla_kernel, seq_len=seq_len)
    qk_blk = GLA_QK
    return pl.pallas_call(
        kern,
        grid=(batch,),
        in_specs=[
            pl.BlockSpec((seq_len, GLA_QK), lambda b: (b, 0)),
            pl.BlockSpec((seq_len, GLA_QK), lambda b: (b, 1)),
            pl.BlockSpec((seq_len, GLA_V), lambda b: (b, (2 * qk_blk) // GLA_V)),
            pl.BlockSpec((seq_len, GLA_V), lambda b: (b, (2 * qk_blk) // GLA_V + 1)),
            pl.BlockSpec((seq_len, SMALL_W), lambda b: (b, 0)),
            _resident(wg.shape), _resident(bg.shape), _resident(gnorm.shape),
        ],
        out_specs=pl.BlockSpec((seq_len, GLA_V), lambda b: (b, 0)),
        out_shape=jax.ShapeDtypeStruct((t, GLA_V), BF16),
        scratch_shapes=[
            pltpu.VMEM((seq_len, 2 * GLA_QK), F32),
            pltpu.VMEM((seq_len // (2 * CHUNK), GLA_V, 2 * CHUNK), BF16),
            pltpu.VMEM((seq_len, GLA_V), F32),
            pltpu.VMEM((seq_len, GLA_V), F32),
            pltpu.VMEM((2, GLA_HEADS // 2, 2 * GLA_DV, LANES), F32),
        ],
        compiler_params=pltpu.CompilerParams(dimension_semantics=("arbitrary",),
                                             vmem_limit_bytes=VMEM_LIMIT),
        name="gla_mixer",
    )(p_main, p_main, p_main, p_main, p_small, wg, bg, gnorm)


def _sgu_out_kernel(h_ref, oa_ref, su_ref, sv_ref, lng_ref, lnb_ref, ws_ref, bs_ref, wo_ref, o_ref):
    tm = h_ref.shape[0]
    u = _gelu_tanh(su_ref[...].astype(F32))
    g = _gelu_tanh(sv_ref[...].astype(F32))
    mu = jnp.mean(g, axis=-1, keepdims=True)
    gc = g - mu
    var = jnp.mean(gc * gc, axis=-1, keepdims=True)
    vv = (gc * lax.rsqrt(var + NORM_EPS) * lng_ref[...] + lnb_ref[...]).astype(BF16)
    rows = []
    for c in range(tm // SGU_CHUNK):
        rs = slice(c * SGU_CHUNK, (c + 1) * SGU_CHUNK)
        cols = []
        for gi in range(SGU_GROUPS):
            gs = slice(gi * SGU_GROUP_DIM, (gi + 1) * SGU_GROUP_DIM)
            cols.append(_dot(ws_ref[gi], vv[rs, gs]))
        rows.append(jnp.concatenate(cols, axis=1) + bs_ref[...])
    mixed = jnp.concatenate(rows, axis=0)
    ob = (u * mixed).astype(BF16)
    acc = _dot(oa_ref[...], wo_ref[:GLA_V, :]) + _dot(ob, wo_ref[GLA_V:, :])
    o_ref[...] = h_ref[...] + acc


def _sgu_out(h2d, o_a, p_main, ln_g, ln_b, w_s, b_full, w_out):
    t, d = h2d.shape
    su_blk = (2 * GLA_QK + 2 * GLA_V) // SGU_DIM
    return pl.pallas_call(
        _sgu_out_kernel,
        grid=(t // ROW_TILE,),
        in_specs=[
            pl.BlockSpec((ROW_TILE, d), lambda i: (i, 0)),
            pl.BlockSpec((ROW_TILE, GLA_V), lambda i: (i, 0)),
            pl.BlockSpec((ROW_TILE, SGU_DIM), lambda i: (i, su_blk)),
            pl.BlockSpec((ROW_TILE, SGU_DIM), lambda i: (i, su_blk + 1)),
            _resident(ln_g.shape), _resident(ln_b.shape), _resident(w_s.shape),
            _resident(b_full.shape), _resident(w_out.shape),
        ],
        out_specs=pl.BlockSpec((ROW_TILE, d), lambda i: (i, 0)),
        out_shape=jax.ShapeDtypeStruct((t, d), F32),
        compiler_params=pltpu.CompilerParams(dimension_semantics=("arbitrary",),
                                             vmem_limit_bytes=VMEM_LIMIT),
        name="sgu_out_proj",
    )(h2d, o_a, p_main, p_main, ln_g, ln_b, w_s, b_full, w_out)


GDN_SCALARS = 6


def _packed_product(x, y):
    return _dot3(x, _pair_rows(y))


def _gdn_kernel(q_ref, k_ref, v_ref, sm_ref, tab_ref, esum_ref, eexp_ref, o_ref,
                ex_ref, of_ref, ob_ref, st_ref, *, seq_len):
    n_chunks = seq_len // CHUNK
    blk = 2 * CHUNK
    hp = pl.program_id(1)

    def prep(rb, carry):
        r0 = pl.multiple_of(rb * blk, blk)
        rows = pl.ds(r0, blk)
        sm = sm_ref[rows, :]
        a_exp = tab_ref[0:1, :]
        dt_b = tab_ref[1:2, :]
        lane = _iota(sm.shape, 1)
        gates = jnp.where(lane < 2 * GDN_HEADS, _sigmoid(sm), -a_exp * _softplus(sm + dt_b))
        q = q_ref[rows, :].astype(F32)
        k = k_ref[rows, :].astype(F32)
        sq = _dot((q * q).astype(BF16), esum_ref[0])
        sk = _dot((k * k).astype(BF16), esum_ref[1])
        inv = lax.rsqrt(sq + sk + NORM_EPS)
        inv = jnp.where((lane >= 4 * GDN_HEADS) & (lane < 4 * GDN_HEADS + 2),
                        inv * (GDN_DK ** -0.5), inv)
        table = jnp.where(lane < 4 * GDN_HEADS, gates, inv)
        hi, lo = _split(table)
        ex_ref[rows, :] = _dot(hi, eexp_ref[0]) + _dot(lo, eexp_ref[0])
        return carry

    lax.fori_loop(0, seq_len // blk, prep, 0)
    st_ref[...] = jnp.zeros(st_ref.shape, F32)

    tri = (_tri_ones(True), _tri_ones(False))
    incl = (_packed_tri_mask(True), _packed_tri_mask(False))
    strict = (_packed_tri_mask(True, strict=True), _packed_tri_mask(False, strict=True))
    diag = _iota((CHUNK, LANES), 0) == (_iota((CHUNK, LANES), 1) & (CHUNK - 1))
    lo_half = _iota((1, LANES), 1) < CHUNK
    st_mask = (_iota((2 * GDN_DK, 2 * GDN_DV), 0) // GDN_DK) == (_iota((2 * GDN_DK, 2 * GDN_DV), 1) // GDN_DV)
    eye = jnp.where(diag, 1.0, 0.0)

    def row_form(col_form):
        return jnp.sum(jnp.where(diag, col_form, 0.0), axis=0, keepdims=True)

    def chunk_step(ci, dirn, o_dst):
        r0 = pl.multiple_of(ci * CHUNK, CHUNK)
        rows = pl.ds(r0, CHUNK)
        k_raw = k_ref[rows, :]
        q_raw = q_ref[rows, :]
        v_raw = v_ref[rows, :]
        g_c = ex_ref[rows, (0 + dirn) * LANES:(1 + dirn) * LANES]
        beta_c = ex_ref[rows, (2 + dirn) * LANES:(3 + dirn) * LANES]
        rq_c = ex_ref[rows, 4 * LANES:5 * LANES]
        rk_c = ex_ref[rows, 5 * LANES:6 * LANES]
        cum_c = _dot_exact_lhs(tri[dirn], g_c)
        tot_r = cum_c[CHUNK - 1:CHUNK, :] if dirn == 0 else cum_c[0:1, :]
        cum_r = row_form(cum_c)
        beta_r = row_form(beta_c)
        rk_r = row_form(rk_c)

        kbd = _pair_blockdiag(k_raw.astype(F32)).astype(BF16)
        gram = _dot_nt(jnp.concatenate([k_raw, q_raw], axis=0), kbd)
        kk = gram[:CHUNK] * rk_c * rk_r
        qk = gram[CHUNK:] * rq_c * rk_r

        decay = jnp.exp(jnp.where(incl[dirn], cum_c - cum_r, -1e30))
        a = jnp.where(strict[dirn], kk * beta_c * decay, 0.0)
        inv = eye - a
        pw = a
        for _ in range(5):
            pw = _packed_product(pw, pw)
            inv = inv + _packed_product(inv, pw)
        t_u = inv * beta_r
        t_w = t_u * (jnp.exp(cum_r) * rk_r)
        e_c = rk_c * jnp.exp(tot_r - cum_c)
        vbd = _pair_blockdiag(v_raw.astype(F32)).astype(BF16)
        uu = _dot(jnp.concatenate([t_u, t_u * e_c], axis=0).astype(BF16), vbd)
        ww = _dot(jnp.concatenate([t_w, t_w * e_c], axis=0).astype(BF16), kbd)
        a_qk = qk * decay
        d_q = jnp.where(diag, rq_c * jnp.exp(cum_c), 0.0)

        state = st_ref[dirn]
        lhs = jnp.concatenate([ww.astype(BF16), q_raw], axis=0)
        prod = _dot(lhs, state.astype(BF16))
        v_new = uu[:CHUNK] - prod[:CHUNK]
        v_new_e = uu[CHUNK:] - prod[CHUNK:2 * CHUNK]
        qs = prod[2 * CHUNK:]
        rhs = jnp.concatenate([_pair_blockdiag(v_new), _pair_blockdiag(qs)], axis=0).astype(BF16)
        o_dst[rows, :] = _dot(jnp.concatenate([a_qk, d_q], axis=1).astype(BF16), rhs)

        tot8 = jnp.broadcast_to(tot_r, (SUBLANES_F32, LANES))
        tot2 = pltpu.roll(tot8, CHUNK, axis=1)[0:1, :]
        dec = jnp.exp(jnp.concatenate([jnp.where(lo_half, tot_r, tot2),
                                       jnp.where(lo_half, tot2, tot_r)], axis=1))
        upd = _dot_tn(k_raw, v_new_e.astype(BF16))
        st_ref[dirn] = state * dec + jnp.where(st_mask, upd, 0.0)

    def body(it, carry):
        chunk_step(it, 0, of_ref)
        chunk_step(n_chunks - 1 - it, 1, ob_ref)
        return carry

    lax.fori_loop(0, n_chunks, body, 0)

    def finish(rb, carry):
        r0 = pl.multiple_of(rb * blk, blk)
        rows = pl.ds(r0, blk)
        o_ref[rows, :] = (of_ref[rows, :] + ob_ref[rows, :]).astype(o_ref.dtype)
        return carry

    lax.fori_loop(0, seq_len // blk, finish, 0)


def _gdn(p_main, p_small, tab, esum, eexp, batch, seq_len):
    t = batch * seq_len
    n_pairs = GDN_HEADS // 2
    pw = 2 * GDN_DK
    kern = functools.partial(_gdn_kernel, seq_len=seq_len)
    return pl.pallas_call(
        kern,
        grid=(batch, n_pairs),
        in_specs=[
            pl.BlockSpec((seq_len, pw), lambda b, p: (b, p)),
            pl.BlockSpec((seq_len, pw), lambda b, p: (b, GDN_QK // pw + p)),
            pl.BlockSpec((seq_len, pw), lambda b, p: (b, 2 * GDN_QK // pw + p)),
            pl.BlockSpec((seq_len, SMALL_W), lambda b, p: (b, 0)),
            pl.BlockSpec(tab.shape, lambda b, p: (0, 0)),
            pl.BlockSpec(esum.shape, lambda b, p: (0, 0, 0)),
            pl.BlockSpec((1, SMALL_W, GDN_SCALARS * LANES), lambda b, p: (p, 0, 0)),
        ],
        out_specs=pl.BlockSpec((seq_len, pw), lambda b, p: (b, p)),
        out_shape=jax.ShapeDtypeStruct((t, GDN_V), BF16),
        scratch_shapes=[
            pltpu.VMEM((seq_len, GDN_SCALARS * LANES), F32),
            pltpu.VMEM((seq_len, pw), F32),
            pltpu.VMEM((seq_len, pw), F32),
            pltpu.VMEM((2, 2 * GDN_DK, 2 * GDN_DV), F32),
        ],
        compiler_params=pltpu.CompilerParams(dimension_semantics=("arbitrary", "arbitrary"),
                                             vmem_limit_bytes=VMEM_LIMIT),
        name="gdn_mixer",
    )(p_main, p_main, p_main, p_small, tab, esum, eexp)


def _gated_out_kernel(h_ref, o_ref_in, z_ref, gn_ref, wo_ref, o_ref):
    parts = []
    for h in range(GDN_HEADS):
        hs = slice(h * GDN_DV, (h + 1) * GDN_DV)
        o = o_ref_in[:, hs].astype(F32)
        y = _rms(o, gn_ref[...]) * _silu(z_ref[:, hs].astype(F32))
        parts.append(y.astype(BF16))
    o_ref[...] = h_ref[...] + _dot(jnp.concatenate(parts, axis=1), wo_ref[...])


def _gated_out(h2d, o_gdn, p_main, gnorm, w_out):
    t, d = h2d.shape
    z_blk = GDN_CONV_DIM // GDN_V
    return pl.pallas_call(
        _gated_out_kernel,
        grid=(t // ROW_TILE,),
        in_specs=[
            pl.BlockSpec((ROW_TILE, d), lambda i: (i, 0)),
            pl.BlockSpec((ROW_TILE, GDN_V), lambda i: (i, 0)),
            pl.BlockSpec((ROW_TILE, GDN_V), lambda i: (i, z_blk)),
            _resident(gnorm.shape), _resident(w_out.shape),
        ],
        out_specs=pl.BlockSpec((ROW_TILE, d), lambda i: (i, 0)),
        out_shape=jax.ShapeDtypeStruct((t, d), F32),
        compiler_params=pltpu.CompilerParams(dimension_semantics=("arbitrary",),
                                             vmem_limit_bytes=VMEM_LIMIT),
        name="gdn_out_proj",
    )(h2d, o_gdn, p_main, gnorm, w_out)


def _pad_cols(w, width):
    return jnp.pad(w, ((0, 0), (0, width - w.shape[1])))


def _gla_params(ab_w_in, w_gate_fwd, b_gate_fwd, w_gate_bwd, b_gate_bwd):
    n_wide = 2 * GLA_QK + 2 * GLA_V
    lr0 = n_wide
    sg0 = lr0 + 2 * GLA_LOWRANK
    wide = ab_w_in[:, :n_wide]
    sgu = ab_w_in[:, sg0:sg0 + 2 * SGU_DIM]
    small = _pad_cols(ab_w_in[:, lr0:sg0], SMALL_W)
    w = jnp.concatenate([wide, sgu, small], axis=1).astype(BF16)
    wg = jnp.zeros((SMALL_W, 2 * GLA_QK), F32)
    wg = wg.at[:GLA_LOWRANK, :GLA_QK].set(w_gate_fwd)
    wg = wg.at[GLA_LOWRANK:2 * GLA_LOWRANK, GLA_QK:].set(w_gate_bwd)
    bg = jnp.concatenate([b_gate_fwd, b_gate_bwd])[None, :]
    return w, wg, bg


def _gdn_params(gdn_w_in, a_log_fwd, dt_bias_fwd, a_log_bwd, dt_bias_bwd):
    n_main = GDN_CONV_DIM + GDN_V
    small = _pad_cols(gdn_w_in[:, n_main:], SMALL_W)
    w = jnp.concatenate([gdn_w_in[:, :n_main], small], axis=1).astype(BF16)
    pad = SMALL_W - 4 * GDN_HEADS
    zeros2 = jnp.zeros((2 * GDN_HEADS,), F32)
    a_exp = jnp.concatenate([zeros2, jnp.exp(a_log_fwd), jnp.exp(a_log_bwd), jnp.zeros((pad,), F32)])
    dt_b = jnp.concatenate([zeros2, dt_bias_fwd, dt_bias_bwd, jnp.zeros((pad,), F32)])
    tab = jnp.zeros((SUBLANES_F32, SMALL_W), F32).at[0].set(a_exp).at[1].set(dt_b)
    pw = 2 * GDN_DK
    ch_head = jnp.arange(pw) // GDN_DK
    col = jnp.arange(SMALL_W)
    esum_q = (col[None, :] == (4 * GDN_HEADS + ch_head)[:, None])
    esum_k = (col[None, :] == (4 * GDN_HEADS + 2 + ch_head)[:, None])
    esum = jnp.stack([esum_q, esum_k]).astype(BF16)
    n_pairs = GDN_HEADS // 2
    lane = jnp.arange(GDN_SCALARS * LANES)
    quant = lane // LANES
    sub = (lane % LANES) // CHUNK
    pair = jnp.arange(n_pairs)[:, None]
    head = 2 * pair + sub[None, :]
    src_gate = jnp.array([2 * GDN_HEADS, 3 * GDN_HEADS, 0, GDN_HEADS])
    src = jnp.where(quant[None, :] < 4, src_gate[jnp.minimum(quant, 3)][None, :] + head,
                    4 * GDN_HEADS + 2 * (quant[None, :] - 4) + sub[None, :])
    eexp = (col[None, :, None] == src[:, None, :]).astype(BF16)
    return w, tab, esum, eexp


def kernel(x, norm_mix, norm_ffn, norm_final, ab_w_in, gla_w_gate_fwd, gla_b_gate_fwd, gla_w_gate_bwd, gla_b_gate_bwd, gla_norm, sgu_ln_g, sgu_ln_b, sgu_w_s, sgu_b_s, ab_w_out, gdn_w_in, gdn_conv_w, gdn_a_log_fwd, gdn_dt_bias_fwd, gdn_a_log_bwd, gdn_dt_bias_bwd, gdn_norm, gdn_w_out, ffn_w_up, ffn_conv_w, ffn_conv_b, ffn_w_down):
    batch, seq_len, d = x.shape
    t = batch * seq_len
    assert seq_len % ROW_TILE == 0 and seq_len % (2 * CHUNK) == 0
    h = x.reshape(t, d)

    w0, wg, bg = _gla_params(ab_w_in[0], gla_w_gate_fwd[0], gla_b_gate_fwd[0],
                             gla_w_gate_bwd[0], gla_b_gate_bwd[0])
    n_main0 = 2 * GLA_QK + 2 * GLA_V + 2 * SGU_DIM
    p0, s0 = _norm_proj(h, norm_mix[0][None, :], w0, None, seq_len, n_main0, 0, "gla_sgu_in_proj")
    o_a = _gla(p0, s0, wg, bg, gla_norm[0][None, :], batch, seq_len)
    b_full = jnp.repeat(sgu_b_s[0].T, SGU_GROUP_DIM, axis=1)
    h = _sgu_out(h, o_a, p0, sgu_ln_g[0][None, :], sgu_ln_b[0][None, :],
                 sgu_w_s[0].astype(BF16), b_full, ab_w_out[0].astype(BF16))
    h = _ffn(h, norm_ffn[0][None, :], ffn_w_up[0].astype(BF16), ffn_conv_w[0],
             ffn_conv_b[0][None, :], ffn_w_down[0].astype(BF16), seq_len, None, "ffn0")

    w1, tab, esum, eexp = _gdn_params(gdn_w_in[0], gdn_a_log_fwd[0], gdn_dt_bias_fwd[0],
                                      gdn_a_log_bwd[0], gdn_dt_bias_bwd[0])
    n_main1 = GDN_CONV_DIM + GDN_V
    p1, s1 = _norm_proj(h, norm_mix[1][None, :], w1, gdn_conv_w[0], seq_len, n_main1,
                        GDN_CONV_DIM, "gdn_in_proj")
    o_g = _gdn(p1, s1, tab, esum, eexp, batch, seq_len)
    h = _gated_out(h, o_g, p1, gdn_norm[0][None, :], gdn_w_out[0].astype(BF16))
    h = _ffn(h, norm_ffn[1][None, :], ffn_w_up[1].astype(BF16), ffn_conv_w[1],
             ffn_conv_b[1][None, :], ffn_w_down[1].astype(BF16), seq_len,
             norm_final[None, :], "ffn1")
    return h.reshape(batch, seq_len, d)
```

```python
import functools

import jax
import jax.numpy as jnp
from jax import lax
from jax.experimental import pallas as pl
from jax.experimental.pallas import tpu as pltpu

F32 = jnp.float32
BF16 = jnp.bfloat16

NORM_EPS = 1e-6
GLA_HEADS = 4
GLA_DK = 64
GLA_DV = 128
GLA_QK = GLA_HEADS * GLA_DK
GLA_V = GLA_HEADS * GLA_DV
GLA_LOWRANK = 16
GLA_GATE_NORMALIZER = 16.0
SGU_GROUPS = 4
SGU_GROUP_DIM = 128
SGU_DIM = SGU_GROUPS * SGU_GROUP_DIM
SGU_CHUNK = 128
GDN_HEADS = 8
GDN_DK = 128
GDN_DV = 128
GDN_QK = GDN_HEADS * GDN_DK
GDN_V = GDN_HEADS * GDN_DV
GDN_CONV_DIM = 2 * GDN_QK + GDN_V
FFN_DIM = 2816

LANES = 128
SUBLANES_F32 = 8
CHUNK = 64
SMALL_W = LANES

ROW_TILE = 512
COL_TILE = 256
HALO = SUBLANES_F32
VMEM_LIMIT = 56 * 1024 * 1024


def _dot(a, b):
    return jnp.dot(a, b, preferred_element_type=F32)


def _dot_nt(a, b):
    return lax.dot_general(a, b, (((1,), (1,)), ((), ())), preferred_element_type=F32)


def _dot_tn(a, b):
    return lax.dot_general(a, b, (((0,), (0,)), ((), ())), preferred_element_type=F32)


def _split(a):
    hi = a.astype(BF16)
    lo = (a - hi.astype(F32)).astype(BF16)
    return hi, lo


def _dot_exact_lhs(l_bf16, a):
    hi, lo = _split(a)
    return _dot(l_bf16, hi) + _dot(l_bf16, lo)


def _dot_exact_rhs(a, r_bf16):
    hi, lo = _split(a)
    return _dot(hi, r_bf16) + _dot(lo, r_bf16)


def _dot3(a, b):
    ah, al = _split(a)
    bh, bl = _split(b)
    return _dot(ah, bh) + _dot(ah, bl) + _dot(al, bh)


def _rms(x, gain):
    ms = jnp.mean(x * x, axis=-1, keepdims=True)
    return x * lax.rsqrt(ms + NORM_EPS) * gain


def _sigmoid(x):
    return 1.0 / (1.0 + jnp.exp(-x))


def _silu(x):
    return x * _sigmoid(x)


def _softplus(x):
    return jnp.maximum(x, 0.0) + jnp.log(1.0 + jnp.exp(-jnp.abs(x)))


def _gelu_tanh(x):
    c = 0.7978845608028654
    return 0.5 * x * (1.0 + jnp.tanh(c * (x + 0.044715 * (x * x * x))))


def _iota(shape, dim):
    return lax.broadcasted_iota(jnp.int32, shape, dim)


def _shift_rows(g, first_row, last_row):
    n = g.shape[0]
    row = _iota(g.shape, 0)
    g_prev = jnp.where(row == 0, first_row, pltpu.roll(g, 1, axis=0))
    g_next = jnp.where(row == n - 1, last_row, pltpu.roll(g, n - 1, axis=0))
    return g_prev, g_next


def _halo_rows(gh, tiles_per_seq):
    i = pl.program_id(0)
    pos = i % tiles_per_seq
    keep_prev = (pos != 0).astype(F32)
    keep_next = (pos != tiles_per_seq - 1).astype(F32)
    prev_row = gh[HALO - 1:HALO, :] * keep_prev
    next_row = gh[HALO:HALO + 1, :] * keep_next
    return prev_row, next_row


def _normed_with_halo(x_ref, xp_ref, xn_ref, g_ref):
    g = g_ref[...]
    hn = _rms(x_ref[...], g).astype(BF16)
    halo = jnp.concatenate([xp_ref[...], xn_ref[...]], axis=0)
    hh = _rms(halo, g).astype(BF16)
    return hn, jnp.concatenate([hn, hh], axis=0)


def _norm_proj_kernel(*refs, n_main, conv_cols, tiles_per_seq):
    if conv_cols:
        x_ref, xp_ref, xn_ref, g_ref, w_ref, cw_ref, o_ref, s_ref = refs
        hn, hx = _normed_with_halo(x_ref, xp_ref, xn_ref, g_ref)
    else:
        x_ref, g_ref, w_ref, o_ref, s_ref = refs
        hn = _rms(x_ref[...], g_ref[...]).astype(BF16)
    tm = hn.shape[0]
    for c in range(0, n_main, COL_TILE):
        cs = slice(c, c + COL_TILE)
        if c < conv_cols:
            acc = _dot(hx, w_ref[:, cs])
            gp = acc[:tm]
            prev_row, next_row = _halo_rows(acc[tm:], tiles_per_seq)
            g_prev, g_next = _shift_rows(gp, prev_row, next_row)
            cw = cw_ref[:, cs]
            y = cw[0:1] * g_prev + cw[1:2] * gp + cw[2:3] * g_next
            o_ref[:, cs] = _silu(y).astype(o_ref.dtype)
        else:
            o_ref[:, cs] = _dot(hn, w_ref[:, cs]).astype(o_ref.dtype)
    s_ref[...] = _dot(hn, w_ref[:, n_main:])


def _halo_specs(d, n_rows):
    blocks_per_tile = ROW_TILE // HALO
    last = n_rows // HALO - 1
    prev = pl.BlockSpec((HALO, d), lambda i: (jnp.maximum(i * blocks_per_tile - 1, 0), 0))
    nxt = pl.BlockSpec((HALO, d), lambda i: (jnp.minimum((i + 1) * blocks_per_tile, last), 0))
    return prev, nxt


def _resident(shape):
    return pl.BlockSpec(shape, lambda i: (0,) * len(shape), pipeline_mode=pl.Buffered(1))


def _norm_proj(x2d, gain, w, conv_w, seq_len, n_main, conv_cols, name):
    t, d = x2d.shape
    n_all = w.shape[1]
    row = pl.BlockSpec((ROW_TILE, d), lambda i: (i, 0))
    in_specs = [row]
    args = [x2d]
    if conv_cols:
        prev, nxt = _halo_specs(d, t)
        in_specs += [prev, nxt]
        args += [x2d, x2d]
    in_specs += [_resident((1, d)), _resident((d, n_all))]
    args += [gain, w]
    if conv_cols:
        in_specs.append(_resident(conv_w.shape))
        args.append(conv_w)
    kern = functools.partial(_norm_proj_kernel, n_main=n_main, conv_cols=conv_cols,
                             tiles_per_seq=seq_len // ROW_TILE)
    return pl.pallas_call(
        kern,
        grid=(t // ROW_TILE,),
        in_specs=in_specs,
        out_specs=[pl.BlockSpec((ROW_TILE, n_main), lambda i: (i, 0)),
                   pl.BlockSpec((ROW_TILE, SMALL_W), lambda i: (i, 0))],
        out_shape=[jax.ShapeDtypeStruct((t, n_main), BF16),
                   jax.ShapeDtypeStruct((t, SMALL_W), F32)],
        compiler_params=pltpu.CompilerParams(dimension_semantics=("arbitrary",),
                                             vmem_limit_bytes=VMEM_LIMIT),
        name=name,
    )(*args)


def _ffn_kernel(*refs, tiles_per_seq, final_norm):
    if final_norm:
        x_ref, xp_ref, xn_ref, g_ref, wup_ref, cw_ref, cb_ref, wdn_ref, gf_ref, o_ref, act_ref = refs
    else:
        x_ref, xp_ref, xn_ref, g_ref, wup_ref, cw_ref, cb_ref, wdn_ref, o_ref, act_ref = refs
    hn, hx = _normed_with_halo(x_ref, xp_ref, xn_ref, g_ref)
    tm = hn.shape[0]
    for c in range(0, FFN_DIM, COL_TILE):
        cs = slice(c, c + COL_TILE)
        acc = _dot(hx, wup_ref[:, cs])
        up = _dot(hn, wup_ref[:, FFN_DIM + c:FFN_DIM + c + COL_TILE])
        gp = acc[:tm]
        prev_row, next_row = _halo_rows(acc[tm:], tiles_per_seq)
        g_prev, g_next = _shift_rows(gp, prev_row, next_row)
        cw = cw_ref[:, cs]
        gate = cw[0:1] * g_prev + cw[1:2] * gp + cw[2:3] * g_next + cb_ref[:, cs]
        act_ref[:, cs] = (_silu(gate) * up).astype(BF16)
    out = x_ref[...] + _dot(act_ref[...], wdn_ref[...])
    if final_norm:
        out = _rms(out, gf_ref[...])
    o_ref[...] = out


def _ffn(h2d, gain, w_up, conv_w, conv_b, w_down, seq_len, final_gain, name):
    t, d = h2d.shape
    row = pl.BlockSpec((ROW_TILE, d), lambda i: (i, 0))
    prev, nxt = _halo_specs(d, t)
    in_specs = [row, prev, nxt, _resident((1, d)), _resident(w_up.shape), _resident(conv_w.shape),
                _resident(conv_b.shape), _resident(w_down.shape)]
    args = [h2d, h2d, h2d, gain, w_up, conv_w, conv_b, w_down]
    if final_gain is not None:
        in_specs.append(_resident((1, d)))
        args.append(final_gain)
    kern = functools.partial(_ffn_kernel, tiles_per_seq=seq_len // ROW_TILE,
                             final_norm=final_gain is not None)
    return pl.pallas_call(
        kern,
        grid=(t // ROW_TILE,),
        in_specs=in_specs,
        out_specs=row,
        out_shape=jax.ShapeDtypeStruct((t, d), F32),
        scratch_shapes=[pltpu.VMEM((ROW_TILE, FFN_DIM), BF16)],
        compiler_params=pltpu.CompilerParams(dimension_semantics=("arbitrary",),
                                             vmem_limit_bytes=VMEM_LIMIT),
        name=name,
    )(*args)


def _pair_rows(x):
    lo = _iota(x.shape, 1) < CHUNK
    return jnp.concatenate([jnp.where(lo, x, 0.0), jnp.where(lo, 0.0, x)], axis=0)


def _pair_blockdiag(x):
    lo = _iota(x.shape, 1) < LANES
    return jnp.concatenate([jnp.where(lo, x, 0.0), jnp.where(lo, 0.0, x)], axis=0)


def _block_tri_ones(lower):
    n = 2 * CHUNK
    r = _iota((n, n), 0)
    c = _iota((n, n), 1)
    tri = jnp.where((c <= r) if lower else (c >= r), 1.0, 0.0)
    return jnp.where((r // CHUNK) == (c // CHUNK), tri, 0.0).astype(BF16)


def _packed_tri_mask(lower, strict=False):
    r = _iota((CHUNK, LANES), 0)
    c = _iota((CHUNK, LANES), 1) & (CHUNK - 1)
    if lower:
        return (c < r) if strict else (c <= r)
    return (c > r) if strict else (c >= r)


def _gla_kernel(q_ref, k_ref, v_ref, gate_ref, lr_ref, wg_ref, bg_ref, gn_ref, o_ref,
                la_ref, vt_ref, of_ref, ob_ref, st_ref, *, seq_len):
    n_pairs = GLA_HEADS // 2
    n_dbl = seq_len // (2 * CHUNK)
    blk = 2 * CHUNK

    def prep(rb, carry):
        r0 = pl.multiple_of(rb * blk, blk)
        rows = pl.ds(r0, blk)
        z = _dot3(lr_ref[rows, :], wg_ref[...]) + bg_ref[...]
        log_sig = jnp.minimum(z, 0.0) - jnp.log(1.0 + jnp.exp(-jnp.abs(z)))
        la_ref[rows, :] = log_sig * (1.0 / GLA_GATE_NORMALIZER)
        for cb in range(GLA_V // LANES):
            vblk = v_ref[rows, cb * LANES:(cb + 1) * LANES].astype(F32)
            vt_ref[rb, cb * LANES:(cb + 1) * LANES, :] = vblk.T.astype(BF16)
        return carry

    lax.fori_loop(0, n_dbl, prep, 0)
    st_ref[...] = jnp.zeros(st_ref.shape, F32)

    tri = (_block_tri_ones(True), _block_tri_ones(False))
    masks = (_packed_tri_mask(True), _packed_tri_mask(False))
    st_mask = (_iota((2 * GLA_DV, LANES), 0) // GLA_DV) == (_iota((2 * GLA_DV, LANES), 1) // GLA_DK)
    first_half = _iota((blk, GLA_QK), 0) < CHUNK
    o_refs = (of_ref, ob_ref)

    def body(it, carry):
        dbs = (it, n_dbl - 1 - it)
        loaded = []
        for dirn in range(2):
            rows = pl.ds(pl.multiple_of(dbs[dirn] * blk, blk), blk)
            loaded.append((
                la_ref[rows, dirn * GLA_QK:(dirn + 1) * GLA_QK],
                q_ref[rows, :], k_ref[rows, :], v_ref[rows, :], vt_ref[dbs[dirn]],
                [st_ref[dirn, p] for p in range(n_pairs)],
            ))
        chains = []
        for dirn in range(2):
            la, q2, k2, v2, vt, states = loaded[dirn]
            cum = _dot_exact_lhs(tri[dirn], la)
            if dirn == 0:
                tots = (cum[CHUNK - 1:CHUNK, :], cum[blk - 1:blk, :])
            else:
                tots = (cum[0:1, :], cum[CHUNK:CHUNK + 1, :])
            tot_rows = jnp.where(first_half, tots[0], tots[1])
            q = q2.astype(F32) * (GLA_DK ** -0.5)
            k = k2.astype(F32)
            q_dec = (q * jnp.exp(cum)).astype(BF16)
            k_inv = k * jnp.exp(-cum)
            k_end = k * jnp.exp(tot_rows - cum)
            v32 = v2.astype(F32)
            for p in range(n_pairs):
                ls = slice(p * LANES, (p + 1) * LANES)
                vs = slice(p * 2 * GLA_DV, (p + 1) * 2 * GLA_DV)
                chains.append(dict(dirn=dirn, p=p, order=(0, 1) if dirn == 0 else (1, 0),
                                   qd=q_dec[:, ls], k_inv=k_inv[:, ls], k_end=k_end[:, ls],
                                   v=v32[:, vs], vt=vt[vs, :], state=states[p],
                                   dec=[jnp.exp(tots[cc][:, ls]) for cc in range(2)],
                                   o=[None, None]))
        for ch in chains:
            for cc in range(2):
                rs = slice(cc * CHUNK, (cc + 1) * CHUNK)
                kbd = _pair_rows(ch["k_inv"][rs]).astype(BF16)
                sc = jnp.where(masks[ch["dirn"]], _dot_nt(ch["qd"][rs], kbd), 0.0).astype(BF16)
                ch["o"][cc] = _dot(sc, _pair_blockdiag(ch["v"][rs]).astype(BF16))
        for step in range(2):
            for ch in chains:
                cc = ch["order"][step]
                rs = slice(cc * CHUNK, (cc + 1) * CHUNK)
                state = ch["state"]
                ch["o"][cc] = ch["o"][cc] + _dot_nt(ch["qd"][rs], state.astype(BF16))
                in_chunk = (_iota((blk, LANES), 0) // CHUNK) == cc
                rhs = jnp.where(in_chunk, ch["k_end"], 0.0).astype(BF16)
                upd = _dot(ch["vt"], rhs)
                ch["state"] = state * ch["dec"][cc] + jnp.where(st_mask, upd, 0.0)
        for dirn in range(2):
            rows = pl.ds(pl.multiple_of(dbs[dirn] * blk, blk), blk)
            mine = [ch for ch in chains if ch["dirn"] == dirn]
            o_refs[dirn][rows, :] = jnp.concatenate(
                [jnp.concatenate(ch["o"], axis=0) for ch in mine], axis=1)
            for ch in mine:
                st_ref[dirn, ch["p"]] = ch["state"]
        return carry

    lax.fori_loop(0, n_dbl, body, 0)

    def finish(rb, carry):
        r0 = pl.multiple_of(rb * blk, blk)
        rows = pl.ds(r0, blk)
        for h in range(GLA_HEADS):
            hs = slice(h * GLA_DV, (h + 1) * GLA_DV)
            o = of_ref[rows, hs] + ob_ref[rows, hs]
            y = _rms(o, gn_ref[...]) * _silu(gate_ref[rows, hs].astype(F32))
            o_ref[rows, hs] = y.astype(o_ref.dtype)
        return carry

    lax.fori_loop(0, n_dbl, finish, 0)


def _gla(p_main, p_small, wg, bg, gnorm, batch, seq_len):
    t = batch * seq_len
    kern = functools.partial(_gla_kernel, seq_len=seq_len)
    qk_blk = GLA_QK
    return pl.pallas_call(
        kern,
        grid=(batch,),
        in_specs=[
            pl.BlockSpec((seq_len, GLA_QK), lambda b: (b, 0)),
            pl.BlockSpec((seq_len, GLA_QK), lambda b: (b, 1)),
            pl.BlockSpec((seq_len, GLA_V), lambda b: (b, (2 * qk_blk) // GLA_V)),
            pl.BlockSpec((seq_len, GLA_V), lambda b: (b, (2 * qk_blk) // GLA_V + 1)),
            pl.BlockSpec((seq_len, SMALL_W), lambda b: (b, 0)),
            _resident(wg.shape), _resident(bg.shape), _resident(gnorm.shape),
        ],
        out_specs=pl.BlockSpec((seq_len, GLA_V), lambda b: (b, 0)),
        out_shape=jax.ShapeDtypeStruct((t, GLA_V), BF16),
        scratch_shapes=[
            pltpu.VMEM((seq_len, 2 * GLA_QK), F32),
            pltpu.VMEM((seq_len // (2 * CHUNK), GLA_V, 2 * CHUNK), BF16),
            pltpu.VMEM((seq_len, GLA_V), F32),
            pltpu.VMEM((seq_len, GLA_V), F32),
            pltpu.VMEM((2, GLA_HEADS // 2, 2 * GLA_DV, LANES), F32),
        ],
        compiler_params=pltpu.CompilerParams(dimension_semantics=("arbitrary",),
                                             vmem_limit_bytes=VMEM_LIMIT),
        name="gla_mixer",
    )(p_main, p_main, p_main, p_main, p_small, wg, bg, gnorm)


def _sgu_out_kernel(h_ref, oa_ref, su_ref, sv_ref, lng_ref, lnb_ref, ws_ref, bs_ref, wo_ref, o_ref):
    tm = h_ref.shape[0]
    u = _gelu_tanh(su_ref[...].astype(F32))
    g = _gelu_tanh(sv_ref[...].astype(F32))
    mu = jnp.mean(g, axis=-1, keepdims=True)
    gc = g - mu
    var = jnp.mean(gc * gc, axis=-1, keepdims=True)
    vv = (gc * lax.rsqrt(var + NORM_EPS) * lng_ref[...] + lnb_ref[...]).astype(BF16)
    rows = []
    for c in range(tm // SGU_CHUNK):
        rs = slice(c * SGU_CHUNK, (c + 1) * SGU_CHUNK)
        cols = []
        for gi in range(SGU_GROUPS):
            gs = slice(gi * SGU_GROUP_DIM, (gi + 1) * SGU_GROUP_DIM)
            cols.append(_dot(ws_ref[gi], vv[rs, gs]))
        rows.append(jnp.concatenate(cols, axis=1) + bs_ref[...])
    mixed = jnp.concatenate(rows, axis=0)
    ob = (u * mixed).astype(BF16)
    acc = _dot(oa_ref[...], wo_ref[:GLA_V, :]) + _dot(ob, wo_ref[GLA_V:, :])
    o_ref[...] = h_ref[...] + acc


def _sgu_out(h2d, o_a, p_main, ln_g, ln_b, w_s, b_full, w_out):
    t, d = h2d.shape
    su_blk = (2 * GLA_QK + 2 * GLA_V) // SGU_DIM
    return pl.pallas_call(
        _sgu_out_kernel,
        grid=(t // ROW_TILE,),
        in_specs=[
            pl.BlockSpec((ROW_TILE, d), lambda i: (i, 0)),
            pl.BlockSpec((ROW_TILE, GLA_V), lambda i: (i, 0)),
            pl.BlockSpec((ROW_TILE, SGU_DIM), lambda i: (i, su_blk)),
            pl.BlockSpec((ROW_TILE, SGU_DIM), lambda i: (i, su_blk + 1)),
            _resident(ln_g.shape), _resident(ln_b.shape), _resident(w_s.shape),
            _resident(b_full.shape), _resident(w_out.shape),
        ],
        out_specs=pl.BlockSpec((ROW_TILE, d), lambda i: (i, 0)),
        out_shape=jax.ShapeDtypeStruct((t, d), F32),
        compiler_params=pltpu.CompilerParams(dimension_semantics=("arbitrary",),
                                             vmem_limit_bytes=VMEM_LIMIT),
        name="sgu_out_proj",
    )(h2d, o_a, p_main, p_main, ln_g, ln_b, w_s, b_full, w_out)


GDN_SCALARS = 6
GDN_PAIRS_PER_STEP = 2
GDN_STEP_W = GDN_PAIRS_PER_STEP * 2 * GDN_DK
GDN_NORM_COL = 4 * GDN_HEADS
GDN_STEP_HEADS = 2 * GDN_PAIRS_PER_STEP


def _packed_product(x, y):
    return _dot(x.astype(BF16), _pair_rows(y).astype(BF16))


def _gdn_kernel(q_ref, k_ref, v_ref, sm_ref, tab_ref, esum_ref, eexp_ref, o_ref,
                tt_ref, aqd_ref, dec_ref, of_ref, ob_ref, st_ref, *, seq_len):
    n_chunks = seq_len // CHUNK
    blk = 2 * CHUNK
    npp = GDN_PAIRS_PER_STEP
    pw = 2 * GDN_DK

    tri = (_block_tri_ones(True), _block_tri_ones(False))
    incl = (_packed_tri_mask(True), _packed_tri_mask(False))
    strict = (_packed_tri_mask(True, strict=True), _packed_tri_mask(False, strict=True))
    diag = _iota((CHUNK, LANES), 0) == (_iota((CHUNK, LANES), 1) & (CHUNK - 1))
    lo_half = _iota((SUBLANES_F32, LANES), 1) < CHUNK
    st_mask = (_iota((pw, pw), 0) // GDN_DK) == (_iota((pw, pw), 1) // GDN_DV)
    eye = jnp.where(diag, 1.0, 0.0)

    def row_form(col_form):
        return jnp.sum(jnp.where(diag, col_form, 0.0), axis=0, keepdims=True)

    def precompute(it, carry):
        rows = pl.ds(pl.multiple_of(it * blk, blk), blk)
        q2 = q_ref[rows, :]
        k2 = k_ref[rows, :]
        sm = sm_ref[rows, :]
        a_exp = jnp.exp(tab_ref[0:1, :])
        dt_b = tab_ref[1:2, :]

        lane = _iota(sm.shape, 1)
        gates = jnp.where(lane < 2 * GDN_HEADS, _sigmoid(sm), -a_exp * _softplus(sm + dt_b))
        qf = q2.astype(F32)
        kf = k2.astype(F32)
        ssq = _dot((qf * qf).astype(BF16), esum_ref[0]) + _dot((kf * kf).astype(BF16), esum_ref[1])
        inv = lax.rsqrt(ssq + NORM_EPS)
        is_qn = jnp.abs(2 * lane - (2 * GDN_NORM_COL + GDN_STEP_HEADS - 1)) < GDN_STEP_HEADS
        inv = jnp.where(is_qn, inv * (GDN_DK ** -0.5), inv)
        table = jnp.where(lane < GDN_NORM_COL, gates, inv)
        cum_f = _dot_exact_lhs(tri[0], table)
        cum_b = _dot_exact_lhs(tri[1], table)
        band = lane // GDN_HEADS
        table = jnp.where(band == 2, cum_f, jnp.where(band == 3, cum_b, table))
        hi, lo = _split(table)
        ex = _dot(hi, eexp_ref[0]) + _dot(lo, eexp_ref[0])

        chains = []
        for cc in range(2):
            rs = slice(cc * CHUNK, (cc + 1) * CHUNK)
            for pp in range(npp):
                ps = slice(pp * pw, (pp + 1) * pw)
                base = pp * GDN_SCALARS * LANES
                cols = [ex[rs, base + j * LANES:base + (j + 1) * LANES] for j in range(GDN_SCALARS)]
                rq_c = cols[4]
                rk_c = cols[5]
                rk_r = row_form(rk_c)
                kbd = _pair_blockdiag(kf[rs, ps]).astype(BF16)
                gram = _dot_nt(jnp.concatenate([k2[rs, ps], q2[rs, ps]], axis=0), kbd)
                kk = gram[:CHUNK] * rk_c * rk_r
                qk = gram[CHUNK:] * rq_c * rk_r
                for dirn in range(2):
                    cum_c = cols[dirn]
                    beta_c = cols[2 + dirn]
                    tot_r = cum_c[CHUNK - 1:CHUNK, :] if dirn == 0 else cum_c[0:1, :]
                    cum_r = row_form(cum_c)
                    beta_r = row_form(beta_c)
                    decay = jnp.exp(jnp.where(incl[dirn], cum_c - cum_r, -1e30))
                    a = jnp.where(strict[dirn], kk * beta_c * decay, 0.0)
                    chains.append(dict(cc=cc, pp=pp, dirn=dirn, rq_c=rq_c, rk_c=rk_c, rk_r=rk_r,
                                       qk=qk, cum_c=cum_c, cum_r=cum_r, beta_r=beta_r, tot_r=tot_r,
                                       decay=decay, pw_a=a, inv_m=eye - a))
        for _ in range(5):
            for ch in chains:
                ch["pw_a"] = _packed_product(ch["pw_a"], ch["pw_a"])
            for ch in chains:
                ch["inv_m"] = ch["inv_m"] + _packed_product(ch["inv_m"], ch["pw_a"])
        for ch in chains:
            t_u = ch["inv_m"] * ch["beta_r"]
            t_w = t_u * (jnp.exp(ch["cum_r"]) * ch["rk_r"])
            e_c = ch["rk_c"] * jnp.exp(ch["tot_r"] - ch["cum_c"])
            ch["tt"] = jnp.concatenate([t_u, t_u * e_c, t_w, t_w * e_c], axis=0).astype(BF16)
            d_q = jnp.where(diag, ch["rq_c"] * jnp.exp(ch["cum_c"]), 0.0)
            ch["aqd"] = jnp.concatenate([ch["qk"] * ch["decay"], d_q], axis=1).astype(BF16)
            tot8 = jnp.broadcast_to(ch["tot_r"], (SUBLANES_F32, LANES))
            tot8r = pltpu.roll(tot8, CHUNK, axis=1)
            ch["dec"] = jnp.exp(jnp.concatenate([jnp.where(lo_half, tot8, tot8r),
                                                 jnp.where(lo_half, tot8r, tot8)], axis=1))
        for ch in chains:
            ci = 2 * it + ch["cc"]
            tt_ref[ci, ch["dirn"], ch["pp"]] = ch["tt"]
            aqd_ref[ci, ch["dirn"], ch["pp"]] = ch["aqd"]
            dec_ref[ci, ch["dirn"], ch["pp"]] = ch["dec"]
        return carry

    lax.fori_loop(0, seq_len // blk, precompute, 0)
    st_ref[...] = jnp.zeros(st_ref.shape, F32)

    o_refs = (of_ref, ob_ref)

    def scan(it, carry):
        cis = (it, n_chunks - 1 - it)
        loaded = []
        for dirn in range(2):
            ci = cis[dirn]
            rows = pl.ds(pl.multiple_of(ci * CHUNK, CHUNK), CHUNK)
            per_pair = [(tt_ref[ci, dirn, pp], aqd_ref[ci, dirn, pp], dec_ref[ci, dirn, pp],
                         st_ref[dirn, pp]) for pp in range(npp)]
            loaded.append((q_ref[rows, :], k_ref[rows, :], v_ref[rows, :], per_pair))
        chains = []
        for dirn in range(2):
            q_c, k_c, v_c, per_pair = loaded[dirn]
            kf = k_c.astype(F32)
            vf = v_c.astype(F32)
            for pp in range(npp):
                ps = slice(pp * pw, (pp + 1) * pw)
                tt, aqd, dec, state = per_pair[pp]
                chains.append(dict(dirn=dirn, pp=pp, tt=tt, aqd=aqd, dec=dec, state=state,
                                   q=q_c[:, ps], k=k_c[:, ps],
                                   kbd=_pair_blockdiag(kf[:, ps]).astype(BF16),
                                   vbd=_pair_blockdiag(vf[:, ps]).astype(BF16)))
        for ch in chains:
            ch["uu"] = _dot(ch["tt"][:blk], ch["vbd"])
            ch["ww"] = _dot(ch["tt"][blk:], ch["kbd"])
        for ch in chains:
            lhs = jnp.concatenate([ch["ww"].astype(BF16), ch["q"]], axis=0)
            ch["prod"] = _dot(lhs, ch["state"].astype(BF16))
        for ch in chains:
            v_new_e = ch["uu"][CHUNK:] - ch["prod"][CHUNK:blk]
            upd = _dot_tn(ch["k"], v_new_e.astype(BF16))
            ch["new_state"] = ch["state"] * ch["dec"][0:1, :] + jnp.where(st_mask, upd, 0.0)
        for ch in chains:
            v_new = ch["uu"][:CHUNK] - ch["prod"][:CHUNK]
            rhs = jnp.concatenate([_pair_blockdiag(v_new), _pair_blockdiag(ch["prod"][blk:])], axis=0)
            ch["o"] = _dot(ch["aqd"], rhs.astype(BF16))
        for dirn in range(2):
            rows = pl.ds(pl.multiple_of(cis[dirn] * CHUNK, CHUNK), CHUNK)
            mine = [ch for ch in chains if ch["dirn"] == dirn]
            o_refs[dirn][rows, :] = jnp.concatenate([ch["o"] for ch in mine], axis=1)
            for ch in mine:
                st_ref[dirn, ch["pp"]] = ch["new_state"]
        return carry

    lax.fori_loop(0, n_chunks, scan, 0)

    def finish(rb, carry):
        r0 = pl.multiple_of(rb * blk, blk)
        rows = pl.ds(r0, blk)
        o_ref[rows, :] = (of_ref[rows, :] + ob_ref[rows, :]).astype(o_ref.dtype)
        return carry

    lax.fori_loop(0, seq_len // blk, finish, 0)


def _gdn(p_main, p_small, tab, esum, eexp, batch, seq_len):
    t = batch * seq_len
    n_steps = GDN_QK // GDN_STEP_W
    n_chunks = seq_len // CHUNK
    npp = GDN_PAIRS_PER_STEP
    pw = 2 * GDN_DK
    sw = GDN_STEP_W
    kern = functools.partial(_gdn_kernel, seq_len=seq_len)
    return pl.pallas_call(
        kern,
        grid=(batch, n_steps),
        in_specs=[
            pl.BlockSpec((seq_len, sw), lambda b, p: (b, p)),
            pl.BlockSpec((seq_len, sw), lambda b, p: (b, GDN_QK // sw + p)),
            pl.BlockSpec((seq_len, sw), lambda b, p: (b, 2 * GDN_QK // sw + p)),
            pl.BlockSpec((seq_len, SMALL_W), lambda b, p: (b, 0)),
            pl.BlockSpec(tab.shape, lambda b, p: (0, 0)),
            pl.BlockSpec(esum.shape, lambda b, p: (0, 0, 0)),
            pl.BlockSpec((1, SMALL_W, npp * GDN_SCALARS * LANES), lambda b, p: (p, 0, 0)),
        ],
        out_specs=pl.BlockSpec((seq_len, sw), lambda b, p: (b, p)),
        out_shape=jax.ShapeDtypeStruct((t, GDN_V), BF16),
        scratch_shapes=[
            pltpu.VMEM((n_chunks, 2, npp, 4 * CHUNK, LANES), BF16),
            pltpu.VMEM((n_chunks, 2, npp, CHUNK, 2 * LANES), BF16),
            pltpu.VMEM((n_chunks, 2, npp, SUBLANES_F32, pw), F32),
            pltpu.VMEM((seq_len, sw), F32),
            pltpu.VMEM((seq_len, sw), F32),
            pltpu.VMEM((2, npp, pw, pw), F32),
        ],
        compiler_params=pltpu.CompilerParams(dimension_semantics=("arbitrary", "arbitrary"),
                                             vmem_limit_bytes=VMEM_LIMIT),
        name="gdn_mixer",
    )(p_main, p_main, p_main, p_small, tab, esum, eexp)


def _gated_out_kernel(h_ref, o_ref_in, z_ref, gn_ref, wo_ref, o_ref):
    parts = []
    for h in range(GDN_HEADS):
        hs = slice(h * GDN_DV, (h + 1) * GDN_DV)
        o = o_ref_in[:, hs].astype(F32)
        y = _rms(o, gn_ref[...]) * _silu(z_ref[:, hs].astype(F32))
        parts.append(y.astype(BF16))
    o_ref[...] = h_ref[...] + _dot(jnp.concatenate(parts, axis=1), wo_ref[...])


def _gated_out(h2d, o_gdn, p_main, gnorm, w_out):
    t, d = h2d.shape
    z_blk = GDN_CONV_DIM // GDN_V
    return pl.pallas_call(
        _gated_out_kernel,
        grid=(t // ROW_TILE,),
        in_specs=[
            pl.BlockSpec((ROW_TILE, d), lambda i: (i, 0)),
            pl.BlockSpec((ROW_TILE, GDN_V), lambda i: (i, 0)),
            pl.BlockSpec((ROW_TILE, GDN_V), lambda i: (i, z_blk)),
            _resident(gnorm.shape), _resident(w_out.shape),
        ],
        out_specs=pl.BlockSpec((ROW_TILE, d), lambda i: (i, 0)),
        out_shape=jax.ShapeDtypeStruct((t, d), F32),
        compiler_params=pltpu.CompilerParams(dimension_semantics=("arbitrary",),
                                             vmem_limit_bytes=VMEM_LIMIT),
        name="gdn_out_proj",
    )(h2d, o_gdn, p_main, gnorm, w_out)


def _pad_cols(w, width):
    return jnp.pad(w, ((0, 0), (0, width - w.shape[1])))


def _gla_params(ab_w_in, w_gate_fwd, b_gate_fwd, w_gate_bwd, b_gate_bwd):
    n_wide = 2 * GLA_QK + 2 * GLA_V
    lr0 = n_wide
    sg0 = lr0 + 2 * GLA_LOWRANK
    wide = ab_w_in[:, :n_wide]
    sgu = ab_w_in[:, sg0:sg0 + 2 * SGU_DIM]
    small = _pad_cols(ab_w_in[:, lr0:sg0], SMALL_W)
    w = jnp.concatenate([wide, sgu, small], axis=1).astype(BF16)
    wg = jnp.zeros((SMALL_W, 2 * GLA_QK), F32)
    wg = wg.at[:GLA_LOWRANK, :GLA_QK].set(w_gate_fwd)
    wg = wg.at[GLA_LOWRANK:2 * GLA_LOWRANK, GLA_QK:].set(w_gate_bwd)
    bg = jnp.concatenate([b_gate_fwd, b_gate_bwd])[None, :]
    return w, wg, bg


def _gdn_params(gdn_w_in, a_log_fwd, dt_bias_fwd, a_log_bwd, dt_bias_bwd):
    n_main = GDN_CONV_DIM + GDN_V
    small = _pad_cols(gdn_w_in[:, n_main:], SMALL_W)
    w = jnp.concatenate([gdn_w_in[:, :n_main], small], axis=1).astype(BF16)
    pad = SMALL_W - 4 * GDN_HEADS
    zeros2 = jnp.zeros((2 * GDN_HEADS,), F32)
    a_log = jnp.concatenate([zeros2, a_log_fwd, a_log_bwd, jnp.zeros((pad,), F32)])
    dt_b = jnp.concatenate([zeros2, dt_bias_fwd, dt_bias_bwd, jnp.zeros((pad,), F32)])
    tab = jnp.zeros((SUBLANES_F32, SMALL_W), F32).at[0].set(a_log).at[1].set(dt_b)
    ch_head = jnp.arange(GDN_STEP_W) // GDN_DK
    col = jnp.arange(SMALL_W)
    esum_q = (col[None, :] == (GDN_NORM_COL + ch_head)[:, None])
    esum_k = (col[None, :] == (GDN_NORM_COL + GDN_STEP_HEADS + ch_head)[:, None])
    esum = jnp.stack([esum_q, esum_k]).astype(BF16)
    n_steps = GDN_QK // GDN_STEP_W
    lane = jnp.arange(GDN_PAIRS_PER_STEP * GDN_SCALARS * LANES)
    pair = lane // (GDN_SCALARS * LANES)
    quant = (lane // LANES) % GDN_SCALARS
    local_head = 2 * pair + (lane % LANES) // CHUNK
    head = GDN_STEP_HEADS * jnp.arange(n_steps)[:, None] + local_head[None, :]
    src_gate = jnp.array([2 * GDN_HEADS, 3 * GDN_HEADS, 0, GDN_HEADS])
    src_norm = GDN_NORM_COL + GDN_STEP_HEADS * (quant - 4) + local_head
    src = jnp.where(quant[None, :] < 4, src_gate[jnp.minimum(quant, 3)][None, :] + head,
                    src_norm[None, :])
    eexp = (col[None, :, None] == src[:, None, :]).astype(BF16)
    return w, tab, esum, eexp


def kernel(x, norm_mix, norm_ffn, norm_final, ab_w_in, gla_w_gate_fwd, gla_b_gate_fwd, gla_w_gate_bwd, gla_b_gate_bwd, gla_norm, sgu_ln_g, sgu_ln_b, sgu_w_s, sgu_b_s, ab_w_out, gdn_w_in, gdn_conv_w, gdn_a_log_fwd, gdn_dt_bias_fwd, gdn_a_log_bwd, gdn_dt_bias_bwd, gdn_norm, gdn_w_out, ffn_w_up, ffn_conv_w, ffn_conv_b, ffn_w_down):
    batch, seq_len, d = x.shape
    t = batch * seq_len
    assert seq_len % ROW_TILE == 0 and seq_len % (2 * CHUNK) == 0
    h = x.reshape(t, d)

    w0, wg, bg = _gla_params(ab_w_in[0], gla_w_gate_fwd[0], gla_b_gate_fwd[0],
                             gla_w_gate_bwd[0], gla_b_gate_bwd[0])
    n_main0 = 2 * GLA_QK + 2 * GLA_V + 2 * SGU_DIM
    p0, s0 = _norm_proj(h, norm_mix[0][None, :], w0, None, seq_len, n_main0, 0, "gla_sgu_in_proj")
    o_a = _gla(p0, s0, wg, bg, gla_norm[0][None, :], batch, seq_len)
    b_full = jnp.repeat(sgu_b_s[0].T, SGU_GROUP_DIM, axis=1)
    h = _sgu_out(h, o_a, p0, sgu_ln_g[0][None, :], sgu_ln_b[0][None, :],
                 sgu_w_s[0].astype(BF16), b_full, ab_w_out[0].astype(BF16))
    h = _ffn(h, norm_ffn[0][None, :], ffn_w_up[0].astype(BF16), ffn_conv_w[0],
             ffn_conv_b[0][None, :], ffn_w_down[0].astype(BF16), seq_len, None, "ffn0")

    w1, tab, esum, eexp = _gdn_params(gdn_w_in[0], gdn_a_log_fwd[0], gdn_dt_bias_fwd[0],
                                      gdn_a_log_bwd[0], gdn_dt_bias_bwd[0])
    n_main1 = GDN_CONV_DIM + GDN_V
    p1, s1 = _norm_proj(h, norm_mix[1][None, :], w1, gdn_conv_w[0], seq_len, n_main1,
                        GDN_CONV_DIM, "gdn_in_proj")
    o_g = _gdn(p1, s1, tab, esum, eexp, batch, seq_len)
    h = _gated_out(h, o_g, p1, gdn_norm[0][None, :], gdn_w_out[0].astype(BF16))
    h = _ffn(h, norm_ffn[1][None, :], ffn_w_up[1].astype(BF16), ffn_conv_w[1],
             ffn_conv_b[1][None, :], ffn_w_down[1].astype(BF16), seq_len,
             norm_final[None, :], "ffn1")
    return h.reshape(batch, seq_len, d)
```

```python
import functools

import jax
import jax.numpy as jnp
from jax import lax
from jax.experimental import pallas as pl
from jax.experimental.pallas import tpu as pltpu

F32 = jnp.float32
BF16 = jnp.bfloat16

NORM_EPS = 1e-6
GLA_HEADS = 4
GLA_DK = 64
GLA_DV = 128
GLA_QK = GLA_HEADS * GLA_DK
GLA_V = GLA_HEADS * GLA_DV
GLA_LOWRANK = 16
GLA_GATE_NORMALIZER = 16.0
SGU_GROUPS = 4
SGU_GROUP_DIM = 128
SGU_DIM = SGU_GROUPS * SGU_GROUP_DIM
SGU_CHUNK = 128
GDN_HEADS = 8
GDN_DK = 128
GDN_DV = 128
GDN_QK = GDN_HEADS * GDN_DK
GDN_V = GDN_HEADS * GDN_DV
GDN_CONV_DIM = 2 * GDN_QK + GDN_V
FFN_DIM = 2816

LANES = 128
SUBLANES_F32 = 8
CHUNK = 64
SMALL_W = LANES

ROW_TILE = 512
COL_TILE = 256
HALO = SUBLANES_F32
HALO_BF16 = 2 * SUBLANES_F32
VMEM_LIMIT = 56 * 1024 * 1024


def _dot(a, b):
    return jnp.dot(a, b, preferred_element_type=F32)


def _dot_nt(a, b):
    return lax.dot_general(a, b, (((1,), (1,)), ((), ())), preferred_element_type=F32)


def _dot_tn(a, b):
    return lax.dot_general(a, b, (((0,), (0,)), ((), ())), preferred_element_type=F32)


def _split(a):
    hi = a.astype(BF16)
    lo = (a - hi.astype(F32)).astype(BF16)
    return hi, lo


def _dot_exact_lhs(l_bf16, a):
    hi, lo = _split(a)
    return _dot(l_bf16, hi) + _dot(l_bf16, lo)


def _rms(x, gain):
    ms = jnp.mean(x * x, axis=-1, keepdims=True)
    return x * lax.rsqrt(ms + NORM_EPS) * gain


def _sigmoid(x):
    return 1.0 / (1.0 + jnp.exp(-x))


def _silu(x):
    return x * _sigmoid(x)


def _softplus(x):
    return jnp.maximum(x, 0.0) + jnp.log(1.0 + jnp.exp(-jnp.abs(x)))


def _gelu_tanh(x):
    c = 0.7978845608028654
    return 0.5 * x * (1.0 + jnp.tanh(c * (x + 0.044715 * (x * x * x))))


def _iota(shape, dim):
    return lax.broadcasted_iota(jnp.int32, shape, dim)


def _shift_rows(g, first_row, last_row):
    n = g.shape[0]
    row = _iota(g.shape, 0)
    g_prev = jnp.where(row == 0, first_row, pltpu.roll(g, 1, axis=0))
    g_next = jnp.where(row == n - 1, last_row, pltpu.roll(g, n - 1, axis=0))
    return g_prev, g_next


def _halo_rows(gh, tiles_per_seq):
    i = pl.program_id(0)
    pos = i % tiles_per_seq
    keep_prev = (pos != 0).astype(F32)
    keep_next = (pos != tiles_per_seq - 1).astype(F32)
    prev_row = gh[HALO - 1:HALO, :] * keep_prev
    next_row = gh[HALO:HALO + 1, :] * keep_next
    return prev_row, next_row


def _normed_with_halo(x_ref, xp_ref, xn_ref, g_ref):
    g = g_ref[...]
    hn = _rms(x_ref[...], g).astype(BF16)
    halo = jnp.concatenate([xp_ref[...], xn_ref[...]], axis=0)
    hh = _rms(halo, g).astype(BF16)
    return hn, jnp.concatenate([hn, hh], axis=0)


def _norm_proj_kernel(*refs, n_main, conv_cols, tiles_per_seq):
    if conv_cols:
        x_ref, xp_ref, xn_ref, g_ref, w_ref, cw_ref, o_ref, s_ref = refs
        hn, hx = _normed_with_halo(x_ref, xp_ref, xn_ref, g_ref)
    else:
        x_ref, g_ref, w_ref, o_ref, s_ref = refs
        hn = _rms(x_ref[...], g_ref[...]).astype(BF16)
    tm = hn.shape[0]
    for c in range(0, n_main, COL_TILE):
        cs = slice(c, c + COL_TILE)
        if c < conv_cols:
            acc = _dot(hx, w_ref[:, cs])
            gp = acc[:tm]
            prev_row, next_row = _halo_rows(acc[tm:], tiles_per_seq)
            g_prev, g_next = _shift_rows(gp, prev_row, next_row)
            cw = cw_ref[:, cs]
            y = cw[0:1] * g_prev + cw[1:2] * gp + cw[2:3] * g_next
            o_ref[:, cs] = _silu(y).astype(o_ref.dtype)
        else:
            o_ref[:, cs] = _dot(hn, w_ref[:, cs]).astype(o_ref.dtype)
    s_ref[...] = _dot(hn, w_ref[:, n_main:])


def _halo_specs(d, n_rows, halo, col_blk=0):
    blocks_per_tile = ROW_TILE // halo
    last = n_rows // halo - 1
    prev = pl.BlockSpec((halo, d), lambda i: (jnp.maximum(i * blocks_per_tile - 1, 0), col_blk))
    nxt = pl.BlockSpec((halo, d), lambda i: (jnp.minimum((i + 1) * blocks_per_tile, last), col_blk))
    return prev, nxt


def _resident(shape):
    return pl.BlockSpec(shape, lambda i: (0,) * len(shape), pipeline_mode=pl.Buffered(1))


def _norm_proj(x2d, gain, w, conv_w, seq_len, n_main, conv_cols, name):
    t, d = x2d.shape
    n_all = w.shape[1]
    row = pl.BlockSpec((ROW_TILE, d), lambda i: (i, 0))
    in_specs = [row]
    args = [x2d]
    if conv_cols:
        prev, nxt = _halo_specs(d, t, HALO)
        in_specs += [prev, nxt]
        args += [x2d, x2d]
    in_specs += [_resident((1, d)), _resident((d, n_all))]
    args += [gain, w]
    if conv_cols:
        in_specs.append(_resident(conv_w.shape))
        args.append(conv_w)
    kern = functools.partial(_norm_proj_kernel, n_main=n_main, conv_cols=conv_cols,
                             tiles_per_seq=seq_len // ROW_TILE)
    return pl.pallas_call(
        kern,
        grid=(t // ROW_TILE,),
        in_specs=in_specs,
        out_specs=[pl.BlockSpec((ROW_TILE, n_main), lambda i: (i, 0)),
                   pl.BlockSpec((ROW_TILE, SMALL_W), lambda i: (i, 0))],
        out_shape=[jax.ShapeDtypeStruct((t, n_main), BF16),
                   jax.ShapeDtypeStruct((t, SMALL_W), F32)],
        compiler_params=pltpu.CompilerParams(dimension_semantics=("arbitrary",),
                                             vmem_limit_bytes=VMEM_LIMIT),
        name=name,
    )(*args)


def _with_halo_bf16(m_ref, mp_ref, mn_ref):
    return jnp.concatenate([m_ref[...].astype(F32), mp_ref[...].astype(F32)[HALO_BF16 - HALO:],
                            mn_ref[...].astype(F32)[:HALO]], axis=0)


def _ffn_kernel(*refs, tiles_per_seq, final_norm, gdn_out):
    refs = list(refs)
    x_ref, xp_ref, xn_ref = refs[:3]
    del refs[:3]
    x_ext = jnp.concatenate([x_ref[...], xp_ref[...], xn_ref[...]], axis=0)
    tm = x_ref.shape[0]
    if gdn_out:
        m_ref, mp_ref, mn_ref, z_ref, zp_ref, zn_ref, gn_ref, wo_ref = refs[:8]
        del refs[:8]
        m_ext = _with_halo_bf16(m_ref, mp_ref, mn_ref)
        z_ext = _with_halo_bf16(z_ref, zp_ref, zn_ref)
        parts = []
        for h in range(GDN_HEADS):
            hs = slice(h * GDN_DV, (h + 1) * GDN_DV)
            parts.append((_rms(m_ext[:, hs], gn_ref[...]) * _silu(z_ext[:, hs])).astype(BF16))
        x_ext = x_ext + _dot(jnp.concatenate(parts, axis=1), wo_ref[...])
    g_ref, wup_ref, cw_ref, cb_ref, wdn_ref = refs[:5]
    del refs[:5]
    if final_norm:
        gf_ref = refs.pop(0)
    o_ref, act_ref = refs
    hx = _rms(x_ext, g_ref[...]).astype(BF16)
    hn = hx[:tm]
    for c in range(0, FFN_DIM, COL_TILE):
        cs = slice(c, c + COL_TILE)
        acc = _dot(hx, wup_ref[:, cs])
        up = _dot(hn, wup_ref[:, FFN_DIM + c:FFN_DIM + c + COL_TILE])
        gp = acc[:tm]
        prev_row, next_row = _halo_rows(acc[tm:], tiles_per_seq)
        g_prev, g_next = _shift_rows(gp, prev_row, next_row)
        cw = cw_ref[:, cs]
        gate = cw[0:1] * g_prev + cw[1:2] * gp + cw[2:3] * g_next + cb_ref[:, cs]
        act_ref[:, cs] = (_silu(gate) * up).astype(BF16)
    out = x_ext[:tm] + _dot(act_ref[...], wdn_ref[...])
    if final_norm:
        out = _rms(out, gf_ref[...])
    o_ref[...] = out


def _ffn(h2d, gain, w_up, conv_w, conv_b, w_down, seq_len, final_gain, name, gdn_out=None):
    t, d = h2d.shape
    row = pl.BlockSpec((ROW_TILE, d), lambda i: (i, 0))
    prev, nxt = _halo_specs(d, t, HALO)
    in_specs = [row, prev, nxt]
    args = [h2d, h2d, h2d]
    if gdn_out is not None:
        o_gdn, p_main, gnorm, w_out = gdn_out
        z_blk = GDN_CONV_DIM // GDN_V
        for arr, cb in ((o_gdn, 0), (p_main, z_blk)):
            prev_b, nxt_b = _halo_specs(GDN_V, t, HALO_BF16, cb)
            in_specs += [pl.BlockSpec((ROW_TILE, GDN_V), lambda i, cb=cb: (i, cb)), prev_b, nxt_b]
            args += [arr, arr, arr]
        in_specs += [_resident(gnorm.shape), _resident(w_out.shape)]
        args += [gnorm, w_out]
    in_specs += [_resident((1, d)), _resident(w_up.shape), _resident(conv_w.shape),
                 _resident(conv_b.shape), _resident(w_down.shape)]
    args += [gain, w_up, conv_w, conv_b, w_down]
    if final_gain is not None:
        in_specs.append(_resident((1, d)))
        args.append(final_gain)
    kern = functools.partial(_ffn_kernel, tiles_per_seq=seq_len // ROW_TILE,
                             final_norm=final_gain is not None, gdn_out=gdn_out is not None)
    return pl.pallas_call(
        kern,
        grid=(t // ROW_TILE,),
        in_specs=in_specs,
        out_specs=row,
        out_shape=jax.ShapeDtypeStruct((t, d), F32),
        scratch_shapes=[pltpu.VMEM((ROW_TILE, FFN_DIM), BF16)],
        compiler_params=pltpu.CompilerParams(dimension_semantics=("arbitrary",),
                                             vmem_limit_bytes=VMEM_LIMIT),
        name=name,
    )(*args)


def _pair_rows(x):
    lo = _iota(x.shape, 1) < CHUNK
    return jnp.concatenate([jnp.where(lo, x, 0.0), jnp.where(lo, 0.0, x)], axis=0)


def _pair_blockdiag(x):
    lo = _iota(x.shape, 1) < LANES
    return jnp.concatenate([jnp.where(lo, x, 0.0), jnp.where(lo, 0.0, x)], axis=0)


def _block_tri_ones(lower):
    n = 2 * CHUNK
    r = _iota((n, n), 0)
    c = _iota((n, n), 1)
    tri = jnp.where((c <= r) if lower else (c >= r), 1.0, 0.0)
    return jnp.where((r // CHUNK) == (c // CHUNK), tri, 0.0).astype(BF16)


def _packed_tri_mask(lower, strict=False):
    r = _iota((CHUNK, LANES), 0)
    c = _iota((CHUNK, LANES), 1) & (CHUNK - 1)
    if lower:
        return (c < r) if strict else (c <= r)
    return (c > r) if strict else (c >= r)


def _gla_kernel(q_ref, k_ref, v_ref, gate_ref, lr_ref, wg_ref, bg_ref, gn_ref, o_ref,
                la_ref, vt_ref, of_ref, ob_ref, st_ref, *, seq_len):
    n_pairs = GLA_HEADS // 2
    n_dbl = seq_len // (2 * CHUNK)
    blk = 2 * CHUNK

    def prep(rb, carry):
        r0 = pl.multiple_of(rb * blk, blk)
        rows = pl.ds(r0, blk)
        z = _dot(lr_ref[rows, :].astype(BF16), wg_ref[...]) + bg_ref[...]
        log_sig = jnp.minimum(z, 0.0) - jnp.log(1.0 + jnp.exp(-jnp.abs(z)))
        la_ref[rows, :] = log_sig * (1.0 / GLA_GATE_NORMALIZER)
        for cb in range(GLA_V // LANES):
            vblk = v_ref[rows, cb * LANES:(cb + 1) * LANES].astype(F32)
            vt_ref[rb, cb * LANES:(cb + 1) * LANES, :] = vblk.T.astype(BF16)
        return carry

    lax.fori_loop(0, n_dbl, prep, 0)
    st_ref[...] = jnp.zeros(st_ref.shape, F32)

    tri = (_block_tri_ones(True), _block_tri_ones(False))
    masks = (_packed_tri_mask(True), _packed_tri_mask(False))
    st_mask = (_iota((2 * GLA_DV, LANES), 0) // GLA_DV) == (_iota((2 * GLA_DV, LANES), 1) // GLA_DK)
    first_half = _iota((blk, GLA_QK), 0) < CHUNK
    o_refs = (of_ref, ob_ref)

    def body(it, carry):
        dbs = (it, n_dbl - 1 - it)
        loaded = []
        for dirn in range(2):
            rows = pl.ds(pl.multiple_of(dbs[dirn] * blk, blk), blk)
            loaded.append((
                la_ref[rows, dirn * GLA_QK:(dirn + 1) * GLA_QK],
                q_ref[rows, :], k_ref[rows, :], v_ref[rows, :], vt_ref[dbs[dirn]],
                [st_ref[dirn, p] for p in range(n_pairs)],
            ))
        chains = []
        for dirn in range(2):
            la, q2, k2, v2, vt, states = loaded[dirn]
            cum = _dot_exact_lhs(tri[dirn], la)
            if dirn == 0:
                tots = (cum[CHUNK - 1:CHUNK, :], cum[blk - 1:blk, :])
            else:
                tots = (cum[0:1, :], cum[CHUNK:CHUNK + 1, :])
            tot_rows = jnp.where(first_half, tots[0], tots[1])
            q = q2.astype(F32) * (GLA_DK ** -0.5)
            k = k2.astype(F32)
            q_dec = (q * jnp.exp(cum)).astype(BF16)
            k_inv = k * jnp.exp(-cum)
            k_end = k * jnp.exp(tot_rows - cum)
            v32 = v2.astype(F32)
            for p in range(n_pairs):
                ls = slice(p * LANES, (p + 1) * LANES)
                vs = slice(p * 2 * GLA_DV, (p + 1) * 2 * GLA_DV)
                chains.append(dict(dirn=dirn, p=p, order=(0, 1) if dirn == 0 else (1, 0),
                                   qd=q_dec[:, ls], k_inv=k_inv[:, ls], k_end=k_end[:, ls],
                                   v=v32[:, vs], vt=vt[vs, :], state=states[p],
                                   dec=[jnp.exp(tots[cc][:, ls]) for cc in range(2)],
                                   o=[None, None]))
        for ch in chains:
            for cc in range(2):
                rs = slice(cc * CHUNK, (cc + 1) * CHUNK)
                kbd = _pair_rows(ch["k_inv"][rs]).astype(BF16)
                sc = jnp.where(masks[ch["dirn"]], _dot_nt(ch["qd"][rs], kbd), 0.0).astype(BF16)
                ch["o"][cc] = _dot(sc, _pair_blockdiag(ch["v"][rs]).astype(BF16))
        for step in range(2):
            for ch in chains:
                cc = ch["order"][step]
                rs = slice(cc * CHUNK, (cc + 1) * CHUNK)
                state = ch["state"]
                ch["o"][cc] = ch["o"][cc] + _dot_nt(ch["qd"][rs], state.astype(BF16))
                in_chunk = (_iota((blk, LANES), 0) // CHUNK) == cc
                rhs = jnp.where(in_chunk, ch["k_end"], 0.0).astype(BF16)
                upd = _dot(ch["vt"], rhs)
                ch["state"] = state * ch["dec"][cc] + jnp.where(st_mask, upd, 0.0)
        for dirn in range(2):
            rows = pl.ds(pl.multiple_of(dbs[dirn] * blk, blk), blk)
            mine = [ch for ch in chains if ch["dirn"] == dirn]
            o_refs[dirn][rows, :] = jnp.concatenate(
                [jnp.concatenate(ch["o"], axis=0) for ch in mine], axis=1)
            for ch in mine:
                st_ref[dirn, ch["p"]] = ch["state"]
        return carry

    lax.fori_loop(0, n_dbl, body, 0)

    def finish(rb, carry):
        r0 = pl.multiple_of(rb * blk, blk)
        rows = pl.ds(r0, blk)
        for h in range(GLA_HEADS):
            hs = slice(h * GLA_DV, (h + 1) * GLA_DV)
            o = of_ref[rows, hs] + ob_ref[rows, hs]
            y = _rms(o, gn_ref[...]) * _silu(gate_ref[rows, hs].astype(F32))
            o_ref[rows, hs] = y.astype(o_ref.dtype)
        return carry

    lax.fori_loop(0, n_dbl, finish, 0)


def _gla(p_main, p_small, wg, bg, gnorm, batch, seq_len):
    t = batch * seq_len
    kern = functools.partial(_gla_kernel, seq_len=seq_len)
    qk_blk = GLA_QK
    return pl.pallas_call(
        kern,
        grid=(batch,),
        in_specs=[
            pl.BlockSpec((seq_len, GLA_QK), lambda b: (b, 0)),
            pl.BlockSpec((seq_len, GLA_QK), lambda b: (b, 1)),
            pl.BlockSpec((seq_len, GLA_V), lambda b: (b, (2 * qk_blk) // GLA_V)),
            pl.BlockSpec((seq_len, GLA_V), lambda b: (b, (2 * qk_blk) // GLA_V + 1)),
            pl.BlockSpec((seq_len, SMALL_W), lambda b: (b, 0)),
            _resident(wg.shape), _resident(bg.shape), _resident(gnorm.shape),
        ],
        out_specs=pl.BlockSpec((seq_len, GLA_V), lambda b: (b, 0)),
        out_shape=jax.ShapeDtypeStruct((t, GLA_V), BF16),
        scratch_shapes=[
            pltpu.VMEM((seq_len, 2 * GLA_QK), F32),
            pltpu.VMEM((seq_len // (2 * CHUNK), GLA_V, 2 * CHUNK), BF16),
            pltpu.VMEM((seq_len, GLA_V), F32),
            pltpu.VMEM((seq_len, GLA_V), F32),
            pltpu.VMEM((2, GLA_HEADS // 2, 2 * GLA_DV, LANES), F32),
        ],
        compiler_params=pltpu.CompilerParams(dimension_semantics=("arbitrary",),
                                             vmem_limit_bytes=VMEM_LIMIT),
        name="gla_mixer",
    )(p_main, p_main, p_main, p_main, p_small, wg, bg, gnorm)


def _sgu_out_kernel(h_ref, oa_ref, su_ref, sv_ref, lng_ref, lnb_ref, ws_ref, bs_ref, wo_ref, o_ref):
    tm = h_ref.shape[0]
    u = _gelu_tanh(su_ref[...].astype(F32))
    g = _gelu_tanh(sv_ref[...].astype(F32))
    mu = jnp.mean(g, axis=-1, keepdims=True)
    gc = g - mu
    var = jnp.mean(gc * gc, axis=-1, keepdims=True)
    vv = (gc * lax.rsqrt(var + NORM_EPS) * lng_ref[...] + lnb_ref[...]).astype(BF16)
    rows = []
    for c in range(tm // SGU_CHUNK):
        rs = slice(c * SGU_CHUNK, (c + 1) * SGU_CHUNK)
        cols = []
        for gi in range(SGU_GROUPS):
            gs = slice(gi * SGU_GROUP_DIM, (gi + 1) * SGU_GROUP_DIM)
            cols.append(_dot(ws_ref[gi], vv[rs, gs]))
        rows.append(jnp.concatenate(cols, axis=1) + bs_ref[...])
    mixed = jnp.concatenate(rows, axis=0)
    ob = (u * mixed).astype(BF16)
    acc = _dot(oa_ref[...], wo_ref[:GLA_V, :]) + _dot(ob, wo_ref[GLA_V:, :])
    o_ref[...] = h_ref[...] + acc


def _sgu_out(h2d, o_a, p_main, ln_g, ln_b, w_s, b_full, w_out):
    t, d = h2d.shape
    su_blk = (2 * GLA_QK + 2 * GLA_V) // SGU_DIM
    return pl.pallas_call(
        _sgu_out_kernel,
        grid=(t // ROW_TILE,),
        in_specs=[
            pl.BlockSpec((ROW_TILE, d), lambda i: (i, 0)),
            pl.BlockSpec((ROW_TILE, GLA_V), lambda i: (i, 0)),
            pl.BlockSpec((ROW_TILE, SGU_DIM), lambda i: (i, su_blk)),
            pl.BlockSpec((ROW_TILE, SGU_DIM), lambda i: (i, su_blk + 1)),
            _resident(ln_g.shape), _resident(ln_b.shape), _resident(w_s.shape),
            _resident(b_full.shape), _resident(w_out.shape),
        ],
        out_specs=pl.BlockSpec((ROW_TILE, d), lambda i: (i, 0)),
        out_shape=jax.ShapeDtypeStruct((t, d), F32),
        compiler_params=pltpu.CompilerParams(dimension_semantics=("arbitrary",),
                                             vmem_limit_bytes=VMEM_LIMIT),
        name="sgu_out_proj",
    )(h2d, o_a, p_main, p_main, ln_g, ln_b, w_s, b_full, w_out)


GDN_SCALARS = 6
GDN_PAIRS_PER_STEP = 2
GDN_STEP_W = GDN_PAIRS_PER_STEP * 2 * GDN_DK
GDN_NORM_COL = 4 * GDN_HEADS
GDN_STEP_HEADS = 2 * GDN_PAIRS_PER_STEP


def _packed_product(x, y):
    return _dot(x.astype(BF16), _pair_rows(y).astype(BF16))


def _gdn_kernel(q_ref, k_ref, v_ref, sm_ref, tab_ref, esum_ref, eexp_ref, o_ref,
                tt_ref, aqd_ref, dec_ref, of_ref, ob_ref, st_ref, *, seq_len):
    n_chunks = seq_len // CHUNK
    blk = 2 * CHUNK
    npp = GDN_PAIRS_PER_STEP
    pw = 2 * GDN_DK

    tri = (_block_tri_ones(True), _block_tri_ones(False))
    incl = (_packed_tri_mask(True), _packed_tri_mask(False))
    strict = (_packed_tri_mask(True, strict=True), _packed_tri_mask(False, strict=True))
    diag = _iota((CHUNK, LANES), 0) == (_iota((CHUNK, LANES), 1) & (CHUNK - 1))
    lo_half = _iota((SUBLANES_F32, LANES), 1) < CHUNK
    eye = jnp.where(diag, 1.0, 0.0)

    def row_form(col_form):
        return jnp.sum(jnp.where(diag, col_form, 0.0), axis=0, keepdims=True)

    def precompute(it, carry):
        rows = pl.ds(pl.multiple_of(it * blk, blk), blk)
        q2 = q_ref[rows, :]
        k2 = k_ref[rows, :]
        sm = sm_ref[rows, :]
        a_exp = jnp.exp(tab_ref[0:1, :])
        dt_b = tab_ref[1:2, :]

        lane = _iota(sm.shape, 1)
        gates = jnp.where(lane < 2 * GDN_HEADS, _sigmoid(sm), -a_exp * _softplus(sm + dt_b))
        qf = q2.astype(F32)
        kf = k2.astype(F32)
        ssq = _dot((qf * qf).astype(BF16), esum_ref[0]) + _dot((kf * kf).astype(BF16), esum_ref[1])
        inv = lax.rsqrt(ssq + NORM_EPS)
        is_qn = jnp.abs(2 * lane - (2 * GDN_NORM_COL + GDN_STEP_HEADS - 1)) < GDN_STEP_HEADS
        inv = jnp.where(is_qn, inv * (GDN_DK ** -0.5), inv)
        table = jnp.where(lane < GDN_NORM_COL, gates, inv)
        cum_f = _dot_exact_lhs(tri[0], table)
        cum_b = _dot_exact_lhs(tri[1], table)
        band = lane // GDN_HEADS
        table = jnp.where(band == 2, cum_f, jnp.where(band == 3, cum_b, table))
        hi, lo = _split(table)
        n_cum = 2 * npp * LANES
        e_cum = eexp_ref[0, :, :n_cum]
        ex_cum = _dot(hi, e_cum) + _dot(lo, e_cum)
        ex_rest = _dot(hi, eexp_ref[0, :, n_cum:])

        chains = []
        for cc in range(2):
            rs = slice(cc * CHUNK, (cc + 1) * CHUNK)
            for pp in range(npp):
                ps = slice(pp * pw, (pp + 1) * pw)
                cols = [ex_cum[rs, (2 * pp + j) * LANES:(2 * pp + j + 1) * LANES] for j in range(2)]
                cols += [ex_rest[rs, (4 * pp + j) * LANES:(4 * pp + j + 1) * LANES] for j in range(4)]
                rq_c = cols[4]
                rk_c = cols[5]
                rk_r = row_form(rk_c)
                kbd = _pair_blockdiag(kf[rs, ps]).astype(BF16)
                gram = _dot_nt(jnp.concatenate([k2[rs, ps], q2[rs, ps]], axis=0), kbd)
                kk = gram[:CHUNK] * rk_c * rk_r
                qk = gram[CHUNK:] * rq_c * rk_r
                for dirn in range(2):
                    cum_c = cols[dirn]
                    beta_c = cols[2 + dirn]
                    tot_r = cum_c[CHUNK - 1:CHUNK, :] if dirn == 0 else cum_c[0:1, :]
                    cum_r = row_form(cum_c)
                    beta_r = row_form(beta_c)
                    decay = jnp.exp(jnp.where(incl[dirn], cum_c - cum_r, -1e30))
                    a = jnp.where(strict[dirn], kk * beta_c * decay, 0.0)
                    chains.append(dict(cc=cc, pp=pp, dirn=dirn, rq_c=rq_c, rk_c=rk_c, rk_r=rk_r,
                                       qk=qk, cum_c=cum_c, cum_r=cum_r, beta_r=beta_r, tot_r=tot_r,
                                       decay=decay, pw_a=a, inv_m=eye - a))
        for ch in chains:
            ch["pw_a"] = _packed_product(ch["pw_a"], ch["pw_a"])
        for _ in range(4):
            for ch in chains:
                lhs = jnp.concatenate([ch["inv_m"], ch["pw_a"]], axis=0)
                both = _packed_product(lhs, ch["pw_a"])
                ch["inv_m"] = ch["inv_m"] + both[:CHUNK]
                ch["pw_a"] = both[CHUNK:]
        for ch in chains:
            ch["inv_m"] = ch["inv_m"] + _packed_product(ch["inv_m"], ch["pw_a"])
        for ch in chains:
            t_u = ch["inv_m"] * ch["beta_r"]
            t_w = t_u * (jnp.exp(ch["cum_r"]) * ch["rk_r"])
            e_c = ch["rk_c"] * jnp.exp(ch["tot_r"] - ch["cum_c"])
            ch["tt"] = jnp.concatenate([t_u, t_u * e_c, t_w, t_w * e_c], axis=0).astype(BF16)
            d_q = jnp.where(diag, ch["rq_c"] * jnp.exp(ch["cum_c"]), 0.0)
            ch["aqd"] = jnp.concatenate([ch["qk"] * ch["decay"], d_q], axis=1).astype(BF16)
            tot8 = jnp.broadcast_to(ch["tot_r"], (SUBLANES_F32, LANES))
            tot8r = pltpu.roll(tot8, CHUNK, axis=1)
            ch["dec"] = jnp.exp(jnp.concatenate([jnp.where(lo_half, tot8, tot8r),
                                                 jnp.where(lo_half, tot8r, tot8)], axis=1))
        for ch in chains:
            ci = 2 * it + ch["cc"]
            tt_ref[ci, ch["dirn"], ch["pp"]] = ch["tt"]
            aqd_ref[ci, ch["dirn"], ch["pp"]] = ch["aqd"]
            dec_ref[ci, ch["dirn"], ch["pp"]] = ch["dec"]
        return carry

    lax.fori_loop(0, seq_len // blk, precompute, 0)
    st_ref[...] = jnp.zeros(st_ref.shape, F32)

    o_refs = (of_ref, ob_ref)

    def scan(it, carry):
        cis = (it, n_chunks - 1 - it)
        loaded = []
        for dirn in range(2):
            ci = cis[dirn]
            rows = pl.ds(pl.multiple_of(ci * CHUNK, CHUNK), CHUNK)
            per_pair = [(tt_ref[ci, dirn, pp], aqd_ref[ci, dirn, pp], dec_ref[ci, dirn, pp],
                         [st_ref[dirn, pp, j] for j in range(2)]) for pp in range(npp)]
            loaded.append((q_ref[rows, :], k_ref[rows, :], v_ref[rows, :], per_pair))
        chains = []
        for dirn in range(2):
            q_c, k_c, v_c, per_pair = loaded[dirn]
            kf = k_c.astype(F32)
            vf = v_c.astype(F32)
            for pp in range(npp):
                ps = slice(pp * pw, (pp + 1) * pw)
                tt, aqd, dec, state = per_pair[pp]
                chains.append(dict(dirn=dirn, pp=pp, tt=tt, aqd=aqd, dec=dec, state=state,
                                   q=q_c[:, ps], k=k_c[:, ps],
                                   kbd=_pair_blockdiag(kf[:, ps]).astype(BF16),
                                   vbd=_pair_blockdiag(vf[:, ps]).astype(BF16)))
        for ch in chains:
            ch["uu"] = _dot(ch["tt"][:blk], ch["vbd"])
            ch["ww"] = _dot(ch["tt"][blk:], ch["kbd"])
        heads = (slice(0, GDN_DV), slice(GDN_DV, 2 * GDN_DV))
        for ch in chains:
            lhs = jnp.concatenate([ch["ww"].astype(BF16), ch["q"]], axis=0)
            ch["prod"] = jnp.concatenate(
                [_dot(lhs[:, hs], ch["state"][j].astype(BF16)) for j, hs in enumerate(heads)], axis=1)
        for ch in chains:
            v_new_e = (ch["uu"][CHUNK:] - ch["prod"][CHUNK:blk]).astype(BF16)
            ch["new_state"] = [
                ch["state"][j] * ch["dec"][0:1, hs] + _dot_tn(ch["k"][:, hs], v_new_e[:, hs])
                for j, hs in enumerate(heads)]
        for ch in chains:
            v_new = ch["uu"][:CHUNK] - ch["prod"][:CHUNK]
            rhs = jnp.concatenate([_pair_blockdiag(v_new), _pair_blockdiag(ch["prod"][blk:])], axis=0)
            ch["o"] = _dot(ch["aqd"], rhs.astype(BF16))
        for dirn in range(2):
            rows = pl.ds(pl.multiple_of(cis[dirn] * CHUNK, CHUNK), CHUNK)
            mine = [ch for ch in chains if ch["dirn"] == dirn]
            o_refs[dirn][rows, :] = jnp.concatenate([ch["o"] for ch in mine], axis=1)
            for ch in mine:
                for j in range(2):
                    st_ref[dirn, ch["pp"], j] = ch["new_state"][j]
        return carry

    lax.fori_loop(0, n_chunks, scan, 0)

    def finish(rb, carry):
        r0 = pl.multiple_of(rb * blk, blk)
        rows = pl.ds(r0, blk)
        o_ref[rows, :] = (of_ref[rows, :] + ob_ref[rows, :]).astype(o_ref.dtype)
        return carry

    lax.fori_loop(0, seq_len // blk, finish, 0)


def _gdn(p_main, p_small, tab, esum, eexp, batch, seq_len):
    t = batch * seq_len
    n_steps = GDN_QK // GDN_STEP_W
    n_chunks = seq_len // CHUNK
    npp = GDN_PAIRS_PER_STEP
    pw = 2 * GDN_DK
    sw = GDN_STEP_W
    kern = functools.partial(_gdn_kernel, seq_len=seq_len)
    return pl.pallas_call(
        kern,
        grid=(batch, n_steps),
        in_specs=[
            pl.BlockSpec((seq_len, sw), lambda b, p: (b, p)),
            pl.BlockSpec((seq_len, sw), lambda b, p: (b, GDN_QK // sw + p)),
            pl.BlockSpec((seq_len, sw), lambda b, p: (b, 2 * GDN_QK // sw + p)),
            pl.BlockSpec((seq_len, SMALL_W), lambda b, p: (b, 0)),
            pl.BlockSpec(tab.shape, lambda b, p: (0, 0)),
            pl.BlockSpec(esum.shape, lambda b, p: (0, 0, 0)),
            pl.BlockSpec((1, SMALL_W, npp * GDN_SCALARS * LANES), lambda b, p: (p, 0, 0)),
        ],
        out_specs=pl.BlockSpec((seq_len, sw), lambda b, p: (b, p)),
        out_shape=jax.ShapeDtypeStruct((t, GDN_V), BF16),
        scratch_shapes=[
            pltpu.VMEM((n_chunks, 2, npp, 4 * CHUNK, LANES), BF16),
            pltpu.VMEM((n_chunks, 2, npp, CHUNK, 2 * LANES), BF16),
            pltpu.VMEM((n_chunks, 2, npp, SUBLANES_F32, pw), F32),
            pltpu.VMEM((seq_len, sw), F32),
            pltpu.VMEM((seq_len, sw), F32),
            pltpu.VMEM((2, npp, 2, GDN_DK, GDN_DV), F32),
        ],
        compiler_params=pltpu.CompilerParams(dimension_semantics=("arbitrary", "arbitrary"),
                                             vmem_limit_bytes=VMEM_LIMIT),
        name="gdn_mixer",
    )(p_main, p_main, p_main, p_small, tab, esum, eexp)


def _pad_cols(w, width):
    return jnp.pad(w, ((0, 0), (0, width - w.shape[1])))


def _gla_params(ab_w_in, w_gate_fwd, b_gate_fwd, w_gate_bwd, b_gate_bwd):
    n_wide = 2 * GLA_QK + 2 * GLA_V
    lr0 = n_wide
    sg0 = lr0 + 2 * GLA_LOWRANK
    wide = ab_w_in[:, :n_wide]
    sgu = ab_w_in[:, sg0:sg0 + 2 * SGU_DIM]
    small = _pad_cols(ab_w_in[:, lr0:sg0], SMALL_W)
    w = jnp.concatenate([wide, sgu, small], axis=1).astype(BF16)
    wg = jnp.zeros((SMALL_W, 2 * GLA_QK), F32)
    wg = wg.at[:GLA_LOWRANK, :GLA_QK].set(w_gate_fwd)
    wg = wg.at[GLA_LOWRANK:2 * GLA_LOWRANK, GLA_QK:].set(w_gate_bwd)
    bg = jnp.concatenate([b_gate_fwd, b_gate_bwd])[None, :]
    return w, wg.astype(BF16), bg


def _gdn_params(gdn_w_in, a_log_fwd, dt_bias_fwd, a_log_bwd, dt_bias_bwd):
    n_main = GDN_CONV_DIM + GDN_V
    small = _pad_cols(gdn_w_in[:, n_main:], SMALL_W)
    w = jnp.concatenate([gdn_w_in[:, :n_main], small], axis=1).astype(BF16)
    pad = SMALL_W - 4 * GDN_HEADS
    zeros2 = jnp.zeros((2 * GDN_HEADS,), F32)
    a_log = jnp.concatenate([zeros2, a_log_fwd, a_log_bwd, jnp.zeros((pad,), F32)])
    dt_b = jnp.concatenate([zeros2, dt_bias_fwd, dt_bias_bwd, jnp.zeros((pad,), F32)])
    tab = jnp.zeros((SUBLANES_F32, SMALL_W), F32).at[0].set(a_log).at[1].set(dt_b)
    ch_head = jnp.arange(GDN_STEP_W) // GDN_DK
    col = jnp.arange(SMALL_W)
    esum_q = (col[None, :] == (GDN_NORM_COL + ch_head)[:, None])
    esum_k = (col[None, :] == (GDN_NORM_COL + GDN_STEP_HEADS + ch_head)[:, None])
    esum = jnp.stack([esum_q, esum_k]).astype(BF16)
    n_steps = GDN_QK // GDN_STEP_W
    lane = jnp.arange(GDN_PAIRS_PER_STEP * GDN_SCALARS * LANES)
    blk = lane // LANES
    n_cum_blk = 2 * GDN_PAIRS_PER_STEP
    pair = jnp.where(blk < n_cum_blk, blk // 2, (blk - n_cum_blk) // 4)
    quant = jnp.where(blk < n_cum_blk, blk % 2, 2 + (blk - n_cum_blk) % 4)
    local_head = 2 * pair + (lane % LANES) // CHUNK
    head = GDN_STEP_HEADS * jnp.arange(n_steps)[:, None] + local_head[None, :]
    src_gate = jnp.array([2 * GDN_HEADS, 3 * GDN_HEADS, 0, GDN_HEADS])
    src_norm = GDN_NORM_COL + GDN_STEP_HEADS * (quant - 4) + local_head
    src = jnp.where(quant[None, :] < 4, src_gate[jnp.minimum(quant, 3)][None, :] + head,
                    src_norm[None, :])
    eexp = (col[None, :, None] == src[:, None, :]).astype(BF16)
    return w, tab, esum, eexp


def kernel(x, norm_mix, norm_ffn, norm_final, ab_w_in, gla_w_gate_fwd, gla_b_gate_fwd, gla_w_gate_bwd, gla_b_gate_bwd, gla_norm, sgu_ln_g, sgu_ln_b, sgu_w_s, sgu_b_s, ab_w_out, gdn_w_in, gdn_conv_w, gdn_a_log_fwd, gdn_dt_bias_fwd, gdn_a_log_bwd, gdn_dt_bias_bwd, gdn_norm, gdn_w_out, ffn_w_up, ffn_conv_w, ffn_conv_b, ffn_w_down):
    batch, seq_len, d = x.shape
    t = batch * seq_len
    assert seq_len % ROW_TILE == 0 and seq_len % (2 * CHUNK) == 0
    h = x.reshape(t, d)

    w0, wg, bg = _gla_params(ab_w_in[0], gla_w_gate_fwd[0], gla_b_gate_fwd[0],
                             gla_w_gate_bwd[0], gla_b_gate_bwd[0])
    n_main0 = 2 * GLA_QK + 2 * GLA_V + 2 * SGU_DIM
    p0, s0 = _norm_proj(h, norm_mix[0][None, :], w0, None, seq_len, n_main0, 0, "gla_sgu_in_proj")
    o_a = _gla(p0, s0, wg, bg, gla_norm[0][None, :], batch, seq_len)
    b_full = jnp.repeat(sgu_b_s[0].T, SGU_GROUP_DIM, axis=1)
    h = _sgu_out(h, o_a, p0, sgu_ln_g[0][None, :], sgu_ln_b[0][None, :],
                 sgu_w_s[0].astype(BF16), b_full, ab_w_out[0].astype(BF16))
    h = _ffn(h, norm_ffn[0][None, :], ffn_w_up[0].astype(BF16), ffn_conv_w[0],
             ffn_conv_b[0][None, :], ffn_w_down[0].astype(BF16), seq_len, None, "ffn0")

    w1, tab, esum, eexp = _gdn_params(gdn_w_in[0], gdn_a_log_fwd[0], gdn_dt_bias_fwd[0],
                                      gdn_a_log_bwd[0], gdn_dt_bias_bwd[0])
    n_main1 = GDN_CONV_DIM + GDN_V
    p1, s1 = _norm_proj(h, norm_mix[1][None, :], w1, gdn_conv_w[0], seq_len, n_main1,
                        GDN_CONV_DIM, "gdn_in_proj")
    o_g = _gdn(p1, s1, tab, esum, eexp, batch, seq_len)
    h = _ffn(h, norm_ffn[1][None, :], ffn_w_up[1].astype(BF16), ffn_conv_w[1],
             ffn_conv_b[1][None, :], ffn_w_down[1].astype(BF16), seq_len,
             norm_final[None, :], "gdn_out_ffn1",
             gdn_out=(o_g, p1, gdn_norm[0][None, :], gdn_w_out[0].astype(BF16)))
    return h.reshape(batch, seq_len, d)
```

```python
import functools

import jax
import jax.numpy as jnp
from jax import lax
from jax.experimental import pallas as pl
from jax.experimental.pallas import tpu as pltpu

F32 = jnp.float32
BF16 = jnp.bfloat16

NORM_EPS = 1e-6
GLA_HEADS = 4
GLA_DK = 64
GLA_DV = 128
GLA_QK = GLA_HEADS * GLA_DK
GLA_V = GLA_HEADS * GLA_DV
GLA_LOWRANK = 16
GLA_GATE_NORMALIZER = 16.0
SGU_GROUPS = 4
SGU_GROUP_DIM = 128
SGU_DIM = SGU_GROUPS * SGU_GROUP_DIM
SGU_CHUNK = 128
GDN_HEADS = 8
GDN_DK = 128
GDN_DV = 128
GDN_QK = GDN_HEADS * GDN_DK
GDN_V = GDN_HEADS * GDN_DV
GDN_CONV_DIM = 2 * GDN_QK + GDN_V
FFN_DIM = 2816

LANES = 128
SUBLANES_F32 = 8
CHUNK = 64
SMALL_W = LANES

ROW_TILE = 512
COL_TILE = 256
HALO = SUBLANES_F32
HALO_BF16 = 2 * SUBLANES_F32
VMEM_LIMIT = 56 * 1024 * 1024
GLA_UNROLL = 2


def _dot(a, b):
    return jnp.dot(a, b, preferred_element_type=F32)


def _dot_nt(a, b):
    return lax.dot_general(a, b, (((1,), (1,)), ((), ())), preferred_element_type=F32)


def _dot_tn(a, b):
    return lax.dot_general(a, b, (((0,), (0,)), ((), ())), preferred_element_type=F32)


def _split(a):
    hi = a.astype(BF16)
    lo = (a - hi.astype(F32)).astype(BF16)
    return hi, lo


def _dot_exact_lhs(l_bf16, a):
    hi, lo = _split(a)
    return _dot(l_bf16, hi) + _dot(l_bf16, lo)


def _rms(x, gain):
    ms = jnp.mean(x * x, axis=-1, keepdims=True)
    return x * lax.rsqrt(ms + NORM_EPS) * gain


def _sigmoid(x):
    return 1.0 / (1.0 + jnp.exp(-x))


def _silu(x):
    return x * _sigmoid(x)


def _softplus(x):
    return jnp.maximum(x, 0.0) + jnp.log(1.0 + jnp.exp(-jnp.abs(x)))


def _gelu_tanh(x):
    c = 0.7978845608028654
    return 0.5 * x * (1.0 + jnp.tanh(c * (x + 0.044715 * (x * x * x))))


def _iota(shape, dim):
    return lax.broadcasted_iota(jnp.int32, shape, dim)


def _shift_rows(g, first_row, last_row):
    n = g.shape[0]
    row = _iota(g.shape, 0)
    g_prev = jnp.where(row == 0, first_row, pltpu.roll(g, 1, axis=0))
    g_next = jnp.where(row == n - 1, last_row, pltpu.roll(g, n - 1, axis=0))
    return g_prev, g_next


def _halo_rows(gh, tiles_per_seq):
    i = pl.program_id(0)
    pos = i % tiles_per_seq
    keep_prev = (pos != 0).astype(F32)
    keep_next = (pos != tiles_per_seq - 1).astype(F32)
    prev_row = gh[HALO - 1:HALO, :] * keep_prev
    next_row = gh[HALO:HALO + 1, :] * keep_next
    return prev_row, next_row


def _normed_with_halo(x_ref, xp_ref, xn_ref, g_ref):
    g = g_ref[...]
    hn = _rms(x_ref[...], g).astype(BF16)
    halo = jnp.concatenate([xp_ref[...], xn_ref[...]], axis=0)
    hh = _rms(halo, g).astype(BF16)
    return hn, jnp.concatenate([hn, hh], axis=0)


def _norm_proj_kernel(*refs, n_main, conv_cols, tiles_per_seq):
    if conv_cols:
        x_ref, xp_ref, xn_ref, g_ref, w_ref, cw_ref, o_ref, s_ref = refs
        hn, hx = _normed_with_halo(x_ref, xp_ref, xn_ref, g_ref)
    else:
        x_ref, g_ref, w_ref, o_ref, s_ref = refs
        hn = _rms(x_ref[...], g_ref[...]).astype(BF16)
    tm = hn.shape[0]
    for c in range(0, n_main, COL_TILE):
        cs = slice(c, c + COL_TILE)
        if c < conv_cols:
            acc = _dot(hx, w_ref[:, cs])
            gp = acc[:tm]
            prev_row, next_row = _halo_rows(acc[tm:], tiles_per_seq)
            g_prev, g_next = _shift_rows(gp, prev_row, next_row)
            cw = cw_ref[:, cs]
            y = cw[0:1] * g_prev + cw[1:2] * gp + cw[2:3] * g_next
            o_ref[:, cs] = _silu(y).astype(o_ref.dtype)
        else:
            o_ref[:, cs] = _dot(hn, w_ref[:, cs]).astype(o_ref.dtype)
    s_ref[...] = _dot(hn, w_ref[:, n_main:])


def _halo_specs(d, n_rows, halo, col_blk=0):
    blocks_per_tile = ROW_TILE // halo
    last = n_rows // halo - 1
    prev = pl.BlockSpec((halo, d), lambda i: (jnp.maximum(i * blocks_per_tile - 1, 0), col_blk))
    nxt = pl.BlockSpec((halo, d), lambda i: (jnp.minimum((i + 1) * blocks_per_tile, last), col_blk))
    return prev, nxt


def _resident(shape):
    return pl.BlockSpec(shape, lambda i: (0,) * len(shape), pipeline_mode=pl.Buffered(1))


def _norm_proj(x2d, gain, w, conv_w, seq_len, n_main, conv_cols, name):
    t, d = x2d.shape
    n_all = w.shape[1]
    row = pl.BlockSpec((ROW_TILE, d), lambda i: (i, 0))
    in_specs = [row]
    args = [x2d]
    if conv_cols:
        prev, nxt = _halo_specs(d, t, HALO)
        in_specs += [prev, nxt]
        args += [x2d, x2d]
    in_specs += [_resident((1, d)), _resident((d, n_all))]
    args += [gain, w]
    if conv_cols:
        in_specs.append(_resident(conv_w.shape))
        args.append(conv_w)
    kern = functools.partial(_norm_proj_kernel, n_main=n_main, conv_cols=conv_cols,
                             tiles_per_seq=seq_len // ROW_TILE)
    return pl.pallas_call(
        kern,
        grid=(t // ROW_TILE,),
        in_specs=in_specs,
        out_specs=[pl.BlockSpec((ROW_TILE, n_main), lambda i: (i, 0)),
                   pl.BlockSpec((ROW_TILE, SMALL_W), lambda i: (i, 0))],
        out_shape=[jax.ShapeDtypeStruct((t, n_main), BF16),
                   jax.ShapeDtypeStruct((t, SMALL_W), F32)],
        compiler_params=pltpu.CompilerParams(dimension_semantics=("arbitrary",),
                                             vmem_limit_bytes=VMEM_LIMIT),
        name=name,
    )(*args)


def _with_halo_bf16(m_ref, mp_ref, mn_ref):
    return jnp.concatenate([m_ref[...].astype(F32), mp_ref[...].astype(F32)[HALO_BF16 - HALO:],
                            mn_ref[...].astype(F32)[:HALO]], axis=0)


def _ffn_kernel(*refs, tiles_per_seq, final_norm, gdn_out):
    refs = list(refs)
    x_ref, xp_ref, xn_ref = refs[:3]
    del refs[:3]
    x_ext = jnp.concatenate([x_ref[...], xp_ref[...], xn_ref[...]], axis=0)
    tm = x_ref.shape[0]
    if gdn_out:
        m_ref, mp_ref, mn_ref, z_ref, zp_ref, zn_ref, gn_ref, wo_ref = refs[:8]
        del refs[:8]
        m_ext = _with_halo_bf16(m_ref, mp_ref, mn_ref)
        z_ext = _with_halo_bf16(z_ref, zp_ref, zn_ref)
        parts = []
        for h in range(GDN_HEADS):
            hs = slice(h * GDN_DV, (h + 1) * GDN_DV)
            parts.append((_rms(m_ext[:, hs], gn_ref[...]) * _silu(z_ext[:, hs])).astype(BF16))
        x_ext = x_ext + _dot(jnp.concatenate(parts, axis=1), wo_ref[...])
    g_ref, wup_ref, cw_ref, cb_ref, wdn_ref = refs[:5]
    del refs[:5]
    if final_norm:
        gf_ref = refs.pop(0)
    o_ref, act_ref = refs
    hx = _rms(x_ext, g_ref[...]).astype(BF16)
    hn = hx[:tm]
    for c in range(0, FFN_DIM, COL_TILE):
        cs = slice(c, c + COL_TILE)
        acc = _dot(hx, wup_ref[:, cs])
        up = _dot(hn, wup_ref[:, FFN_DIM + c:FFN_DIM + c + COL_TILE])
        gp = acc[:tm]
        prev_row, next_row = _halo_rows(acc[tm:], tiles_per_seq)
        g_prev, g_next = _shift_rows(gp, prev_row, next_row)
        cw = cw_ref[:, cs]
        gate = cw[0:1] * g_prev + cw[1:2] * gp + cw[2:3] * g_next + cb_ref[:, cs]
        act_ref[:, cs] = (_silu(gate) * up).astype(BF16)
    out = x_ext[:tm] + _dot(act_ref[...], wdn_ref[...])
    if final_norm:
        out = _rms(out, gf_ref[...])
    o_ref[...] = out


def _ffn(h2d, gain, w_up, conv_w, conv_b, w_down, seq_len, final_gain, name, gdn_out=None):
    t, d = h2d.shape
    row = pl.BlockSpec((ROW_TILE, d), lambda i: (i, 0))
    prev, nxt = _halo_specs(d, t, HALO)
    in_specs = [row, prev, nxt]
    args = [h2d, h2d, h2d]
    if gdn_out is not None:
        o_gdn, p_main, gnorm, w_out = gdn_out
        z_blk = GDN_CONV_DIM // GDN_V
        for arr, cb in ((o_gdn, 0), (p_main, z_blk)):
            prev_b, nxt_b = _halo_specs(GDN_V, t, HALO_BF16, cb)
            in_specs += [pl.BlockSpec((ROW_TILE, GDN_V), lambda i, cb=cb: (i, cb)), prev_b, nxt_b]
            args += [arr, arr, arr]
        in_specs += [_resident(gnorm.shape), _resident(w_out.shape)]
        args += [gnorm, w_out]
    in_specs += [_resident((1, d)), _resident(w_up.shape), _resident(conv_w.shape),
                 _resident(conv_b.shape), _resident(w_down.shape)]
    args += [gain, w_up, conv_w, conv_b, w_down]
    if final_gain is not None:
        in_specs.append(_resident((1, d)))
        args.append(final_gain)
    kern = functools.partial(_ffn_kernel, tiles_per_seq=seq_len // ROW_TILE,
                             final_norm=final_gain is not None, gdn_out=gdn_out is not None)
    return pl.pallas_call(
        kern,
        grid=(t // ROW_TILE,),
        in_specs=in_specs,
        out_specs=row,
        out_shape=jax.ShapeDtypeStruct((t, d), F32),
        scratch_shapes=[pltpu.VMEM((ROW_TILE, FFN_DIM), BF16)],
        compiler_params=pltpu.CompilerParams(dimension_semantics=("arbitrary",),
                                             vmem_limit_bytes=VMEM_LIMIT),
        name=name,
    )(*args)


def _pair_rows(x):
    lo = _iota(x.shape, 1) < CHUNK
    return jnp.concatenate([jnp.where(lo, x, 0.0), jnp.where(lo, 0.0, x)], axis=0)


def _pair_blockdiag(x):
    lo = _iota(x.shape, 1) < LANES
    return jnp.concatenate([jnp.where(lo, x, 0.0), jnp.where(lo, 0.0, x)], axis=0)


def _block_tri_ones(lower, n=2 * CHUNK):
    r = _iota((n, n), 0)
    c = _iota((n, n), 1)
    tri = jnp.where((c <= r) if lower else (c >= r), 1.0, 0.0)
    return jnp.where((r // CHUNK) == (c // CHUNK), tri, 0.0).astype(BF16)


def _packed_tri_mask(lower, strict=False):
    r = _iota((CHUNK, LANES), 0)
    c = _iota((CHUNK, LANES), 1) & (CHUNK - 1)
    if lower:
        return (c < r) if strict else (c <= r)
    return (c > r) if strict else (c >= r)


def _gla_kernel(q_ref, k_ref, v_ref, gate_ref, lr_ref, wg_ref, bg_ref, gn_ref, o_ref,
                la_ref, vt_ref, of_ref, ob_ref, st_ref, *, seq_len):
    n_pairs = GLA_HEADS // 2
    n_dbl = seq_len // (2 * CHUNK)
    blk = 2 * CHUNK

    def prep(rb, carry):
        r0 = pl.multiple_of(rb * blk, blk)
        rows = pl.ds(r0, blk)
        z = _dot(lr_ref[rows, :].astype(BF16), wg_ref[...]) + bg_ref[...]
        log_sig = jnp.minimum(z, 0.0) - jnp.log(1.0 + jnp.exp(-jnp.abs(z)))
        la_ref[rows, :] = log_sig * (1.0 / GLA_GATE_NORMALIZER)
        for cb in range(GLA_V // LANES):
            vblk = v_ref[rows, cb * LANES:(cb + 1) * LANES].astype(F32)
            vt_ref[rb, cb * LANES:(cb + 1) * LANES, :] = vblk.T.astype(BF16)
        return carry

    lax.fori_loop(0, n_dbl, prep, 0)
    st_ref[...] = jnp.zeros(st_ref.shape, F32)

    tri = (_block_tri_ones(True), _block_tri_ones(False))
    masks = (_packed_tri_mask(True), _packed_tri_mask(False))
    st_mask = (_iota((2 * GLA_DV, LANES), 0) // GLA_DV) == (_iota((2 * GLA_DV, LANES), 1) // GLA_DK)
    first_half = _iota((blk, GLA_QK), 0) < CHUNK
    o_refs = (of_ref, ob_ref)

    def body(it, carry):
        steps = []
        for sub in range(GLA_UNROLL):
            df = GLA_UNROLL * it + sub
            dbs = (df, n_dbl - 1 - df)
            chains = []
            for dirn in range(2):
                rows = pl.ds(pl.multiple_of(dbs[dirn] * blk, blk), blk)
                la = la_ref[rows, dirn * GLA_QK:(dirn + 1) * GLA_QK]
                q2, k2, v2, vt = q_ref[rows, :], k_ref[rows, :], v_ref[rows, :], vt_ref[dbs[dirn]]
                cum = _dot_exact_lhs(tri[dirn], la)
                if dirn == 0:
                    tots = (cum[CHUNK - 1:CHUNK, :], cum[blk - 1:blk, :])
                else:
                    tots = (cum[0:1, :], cum[CHUNK:CHUNK + 1, :])
                tot_rows = jnp.where(first_half, tots[0], tots[1])
                q = q2.astype(F32) * (GLA_DK ** -0.5)
                k = k2.astype(F32)
                q_dec = (q * jnp.exp(cum)).astype(BF16)
                k_inv = k * jnp.exp(-cum)
                k_end = k * jnp.exp(tot_rows - cum)
                v32 = v2.astype(F32)
                for p in range(n_pairs):
                    ls = slice(p * LANES, (p + 1) * LANES)
                    vs = slice(p * 2 * GLA_DV, (p + 1) * 2 * GLA_DV)
                    chains.append(dict(dirn=dirn, p=p, rows=rows,
                                       order=(0, 1) if dirn == 0 else (1, 0),
                                       qd=q_dec[:, ls], k_inv=k_inv[:, ls], k_end=k_end[:, ls],
                                       v=v32[:, vs], vt=vt[vs, :],
                                       dec=[jnp.exp(tots[cc][:, ls]) for cc in range(2)],
                                       o=[None, None]))
            steps.append(chains)
        states = {(dirn, p): st_ref[dirn, p] for dirn in range(2) for p in range(n_pairs)}
        for chains in steps:
            for ch in chains:
                for cc in range(2):
                    rs = slice(cc * CHUNK, (cc + 1) * CHUNK)
                    kbd = _pair_rows(ch["k_inv"][rs]).astype(BF16)
                    sc = jnp.where(masks[ch["dirn"]], _dot_nt(ch["qd"][rs], kbd), 0.0).astype(BF16)
                    ch["o"][cc] = _dot(sc, _pair_blockdiag(ch["v"][rs]).astype(BF16))
        for chains in steps:
            for step in range(2):
                for ch in chains:
                    cc = ch["order"][step]
                    rs = slice(cc * CHUNK, (cc + 1) * CHUNK)
                    state = states[(ch["dirn"], ch["p"])]
                    ch["o"][cc] = ch["o"][cc] + _dot_nt(ch["qd"][rs], state.astype(BF16))
                    in_chunk = (_iota((blk, LANES), 0) // CHUNK) == cc
                    rhs = jnp.where(in_chunk, ch["k_end"], 0.0).astype(BF16)
                    upd = _dot(ch["vt"], rhs)
                    states[(ch["dirn"], ch["p"])] = state * ch["dec"][cc] + jnp.where(st_mask, upd, 0.0)
        for chains in steps:
            for dirn in range(2):
                mine = [ch for ch in chains if ch["dirn"] == dirn]
                o_refs[dirn][mine[0]["rows"], :] = jnp.concatenate(
                    [jnp.concatenate(ch["o"], axis=0) for ch in mine], axis=1)
        for (dirn, p), state in states.items():
            st_ref[dirn, p] = state
        return carry

    lax.fori_loop(0, n_dbl // GLA_UNROLL, body, 0)

    def finish(rb, carry):
        r0 = pl.multiple_of(rb * blk, blk)
        rows = pl.ds(r0, blk)
        for h in range(GLA_HEADS):
            hs = slice(h * GLA_DV, (h + 1) * GLA_DV)
            o = of_ref[rows, hs] + ob_ref[rows, hs]
            y = _rms(o, gn_ref[...]) * _silu(gate_ref[rows, hs].astype(F32))
            o_ref[rows, hs] = y.astype(o_ref.dtype)
        return carry

    lax.fori_loop(0, n_dbl, finish, 0)


def _gla(p_main, p_small, wg, bg, gnorm, batch, seq_len):
    t = batch * seq_len
    kern = functools.partial(_gla_kernel, seq_len=seq_len)
    qk_blk = GLA_QK
    return pl.pallas_call(
        kern,
        grid=(batch,),
        in_specs=[
            pl.BlockSpec((seq_len, GLA_QK), lambda b: (b, 0)),
            pl.BlockSpec((seq_len, GLA_QK), lambda b: (b, 1)),
            pl.BlockSpec((seq_len, GLA_V), lambda b: (b, (2 * qk_blk) // GLA_V)),
            pl.BlockSpec((seq_len, GLA_V), lambda b: (b, (2 * qk_blk) // GLA_V + 1)),
            pl.BlockSpec((seq_len, SMALL_W), lambda b: (b, 0)),
            _resident(wg.shape), _resident(bg.shape), _resident(gnorm.shape),
        ],
        out_specs=pl.BlockSpec((seq_len, GLA_V), lambda b: (b, 0)),
        out_shape=jax.ShapeDtypeStruct((t, GLA_V), BF16),
        scratch_shapes=[
            pltpu.VMEM((seq_len, 2 * GLA_QK), F32),
            pltpu.VMEM((seq_len // (2 * CHUNK), GLA_V, 2 * CHUNK), BF16),
            pltpu.VMEM((seq_len, GLA_V), F32),
            pltpu.VMEM((seq_len, GLA_V), F32),
            pltpu.VMEM((2, GLA_HEADS // 2, 2 * GLA_DV, LANES), F32),
        ],
        compiler_params=pltpu.CompilerParams(dimension_semantics=("arbitrary",),
                                             vmem_limit_bytes=VMEM_LIMIT),
        name="gla_mixer",
    )(p_main, p_main, p_main, p_main, p_small, wg, bg, gnorm)


def _sgu_out_kernel(h_ref, oa_ref, su_ref, sv_ref, lng_ref, lnb_ref, ws_ref, bs_ref, wo_ref, o_ref):
    tm = h_ref.shape[0]
    u = _gelu_tanh(su_ref[...].astype(F32))
    g = _gelu_tanh(sv_ref[...].astype(F32))
    mu = jnp.mean(g, axis=-1, keepdims=True)
    gc = g - mu
    var = jnp.mean(gc * gc, axis=-1, keepdims=True)
    vv = (gc * lax.rsqrt(var + NORM_EPS) * lng_ref[...] + lnb_ref[...]).astype(BF16)
    rows = []
    for c in range(tm // SGU_CHUNK):
        rs = slice(c * SGU_CHUNK, (c + 1) * SGU_CHUNK)
        cols = []
        for gi in range(SGU_GROUPS):
            gs = slice(gi * SGU_GROUP_DIM, (gi + 1) * SGU_GROUP_DIM)
            cols.append(_dot(ws_ref[gi], vv[rs, gs]))
        rows.append(jnp.concatenate(cols, axis=1) + bs_ref[...])
    mixed = jnp.concatenate(rows, axis=0)
    ob = (u * mixed).astype(BF16)
    acc = _dot(oa_ref[...], wo_ref[:GLA_V, :]) + _dot(ob, wo_ref[GLA_V:, :])
    o_ref[...] = h_ref[...] + acc


def _sgu_out(h2d, o_a, p_main, ln_g, ln_b, w_s, b_full, w_out):
    t, d = h2d.shape
    su_blk = (2 * GLA_QK + 2 * GLA_V) // SGU_DIM
    return pl.pallas_call(
        _sgu_out_kernel,
        grid=(t // ROW_TILE,),
        in_specs=[
            pl.BlockSpec((ROW_TILE, d), lambda i: (i, 0)),
            pl.BlockSpec((ROW_TILE, GLA_V), lambda i: (i, 0)),
            pl.BlockSpec((ROW_TILE, SGU_DIM), lambda i: (i, su_blk)),
            pl.BlockSpec((ROW_TILE, SGU_DIM), lambda i: (i, su_blk + 1)),
            _resident(ln_g.shape), _resident(ln_b.shape), _resident(w_s.shape),
            _resident(b_full.shape), _resident(w_out.shape),
        ],
        out_specs=pl.BlockSpec((ROW_TILE, d), lambda i: (i, 0)),
        out_shape=jax.ShapeDtypeStruct((t, d), F32),
        compiler_params=pltpu.CompilerParams(dimension_semantics=("arbitrary",),
                                             vmem_limit_bytes=VMEM_LIMIT),
        name="sgu_out_proj",
    )(h2d, o_a, p_main, p_main, ln_g, ln_b, w_s, b_full, w_out)


GDN_SCALARS = 6
GDN_PAIRS_PER_STEP = 2
GDN_STEP_W = GDN_PAIRS_PER_STEP * 2 * GDN_DK
GDN_NORM_COL = 4 * GDN_HEADS
GDN_STEP_HEADS = 2 * GDN_PAIRS_PER_STEP
GDN_SCAN_UNROLL = 4
GDN_PRE_CHUNKS = 4


def _packed_product(x, y):
    return _dot(x.astype(BF16), _pair_rows(y).astype(BF16))


def _gdn_kernel(q_ref, k_ref, v_ref, sm_ref, tab_ref, esum_ref, eexp_ref, o_ref,
                tt_ref, aqd_ref, dec_ref, of_ref, ob_ref, st_ref, *, seq_len):
    n_chunks = seq_len // CHUNK
    blk = 2 * CHUNK
    npp = GDN_PAIRS_PER_STEP
    pw = 2 * GDN_DK

    pblk = GDN_PRE_CHUNKS * CHUNK
    tri = (_block_tri_ones(True, pblk), _block_tri_ones(False, pblk))
    incl = (_packed_tri_mask(True), _packed_tri_mask(False))
    strict = (_packed_tri_mask(True, strict=True), _packed_tri_mask(False, strict=True))
    diag = _iota((CHUNK, LANES), 0) == (_iota((CHUNK, LANES), 1) & (CHUNK - 1))
    lo_half = _iota((SUBLANES_F32, LANES), 1) < CHUNK
    eye = jnp.where(diag, 1.0, 0.0)

    def row_form(col_form):
        return jnp.sum(jnp.where(diag, col_form, 0.0), axis=0, keepdims=True)

    def precompute(it, carry):
        rows = pl.ds(pl.multiple_of(it * pblk, pblk), pblk)
        q2 = q_ref[rows, :]
        k2 = k_ref[rows, :]
        sm = sm_ref[rows, :]
        a_exp = jnp.exp(tab_ref[0:1, :])
        dt_b = tab_ref[1:2, :]

        lane = _iota(sm.shape, 1)
        gates = jnp.where(lane < 2 * GDN_HEADS, _sigmoid(sm), -a_exp * _softplus(sm + dt_b))
        qf = q2.astype(F32)
        kf = k2.astype(F32)
        ssq = _dot((qf * qf).astype(BF16), esum_ref[0]) + _dot((kf * kf).astype(BF16), esum_ref[1])
        inv = lax.rsqrt(ssq + NORM_EPS)
        is_qn = jnp.abs(2 * lane - (2 * GDN_NORM_COL + GDN_STEP_HEADS - 1)) < GDN_STEP_HEADS
        inv = jnp.where(is_qn, inv * (GDN_DK ** -0.5), inv)
        table = jnp.where(lane < GDN_NORM_COL, gates, inv)
        cum_f = _dot_exact_lhs(tri[0], table)
        cum_b = _dot_exact_lhs(tri[1], table)
        band = lane // GDN_HEADS
        table = jnp.where(band == 2, cum_f, jnp.where(band == 3, cum_b, table))
        hi, lo = _split(table)
        n_cum = 2 * npp * LANES
        e_cum = eexp_ref[0, :, :n_cum]
        ex_cum = _dot(hi, e_cum) + _dot(lo, e_cum)
        ex_rest = _dot(hi, eexp_ref[0, :, n_cum:])

        chains = []
        for cc in range(GDN_PRE_CHUNKS):
            rs = slice(cc * CHUNK, (cc + 1) * CHUNK)
            for pp in range(npp):
                ps = slice(pp * pw, (pp + 1) * pw)
                cols = [ex_cum[rs, (2 * pp + j) * LANES:(2 * pp + j + 1) * LANES] for j in range(2)]
                cols += [ex_rest[rs, (4 * pp + j) * LANES:(4 * pp + j + 1) * LANES] for j in range(4)]
                rq_c = cols[4]
                rk_c = cols[5]
                rk_r = row_form(rk_c)
                kbd = _pair_blockdiag(kf[rs, ps]).astype(BF16)
                gram = _dot_nt(jnp.concatenate([k2[rs, ps], q2[rs, ps]], axis=0), kbd)
                kk = gram[:CHUNK] * rk_c * rk_r
                qk = gram[CHUNK:] * rq_c * rk_r
                for dirn in range(2):
                    cum_c = cols[dirn]
                    beta_c = cols[2 + dirn]
                    tot_r = cum_c[CHUNK - 1:CHUNK, :] if dirn == 0 else cum_c[0:1, :]
                    cum_r = row_form(cum_c)
                    beta_r = row_form(beta_c)
                    decay = jnp.exp(jnp.where(incl[dirn], cum_c - cum_r, -1e30))
                    a = jnp.where(strict[dirn], kk * beta_c * decay, 0.0)
                    chains.append(dict(cc=cc, pp=pp, dirn=dirn, rq_c=rq_c, rk_c=rk_c, rk_r=rk_r,
                                       qk=qk, cum_c=cum_c, cum_r=cum_r, beta_r=beta_r, tot_r=tot_r,
                                       decay=decay, pw_a=a, inv_m=eye - a))
        for ch in chains:
            ch["pw_a"] = _packed_product(ch["pw_a"], ch["pw_a"])
        for _ in range(4):
            for ch in chains:
                lhs = jnp.concatenate([ch["inv_m"], ch["pw_a"]], axis=0)
                both = _packed_product(lhs, ch["pw_a"])
                ch["inv_m"] = ch["inv_m"] + both[:CHUNK]
                ch["pw_a"] = both[CHUNK:]
        for ch in chains:
            ch["inv_m"] = ch["inv_m"] + _packed_product(ch["inv_m"], ch["pw_a"])
        for ch in chains:
            t_u = ch["inv_m"] * ch["beta_r"]
            t_w = t_u * (jnp.exp(ch["cum_r"]) * ch["rk_r"])
            e_c = ch["rk_c"] * jnp.exp(ch["tot_r"] - ch["cum_c"])
            ch["tt"] = jnp.concatenate([t_u, t_u * e_c, t_w, t_w * e_c], axis=0).astype(BF16)
            d_q = jnp.where(diag, ch["rq_c"] * jnp.exp(ch["cum_c"]), 0.0)
            ch["aqd"] = jnp.concatenate([ch["qk"] * ch["decay"], d_q], axis=1).astype(BF16)
            tot8 = jnp.broadcast_to(ch["tot_r"], (SUBLANES_F32, LANES))
            tot8r = pltpu.roll(tot8, CHUNK, axis=1)
            ch["dec"] = jnp.exp(jnp.concatenate([jnp.where(lo_half, tot8, tot8r),
                                                 jnp.where(lo_half, tot8r, tot8)], axis=1))
        for ch in chains:
            ci = GDN_PRE_CHUNKS * it + ch["cc"]
            tt_ref[ci, ch["dirn"], ch["pp"]] = ch["tt"]
            aqd_ref[ci, ch["dirn"], ch["pp"]] = ch["aqd"]
            dec_ref[ci, ch["dirn"], ch["pp"]] = ch["dec"]
        return carry

    lax.fori_loop(0, seq_len // pblk, precompute, 0)
    st_ref[...] = jnp.zeros(st_ref.shape, F32)

    o_refs = (of_ref, ob_ref)

    def scan(it, carry):
        heads = (slice(0, GDN_DV), slice(GDN_DV, 2 * GDN_DV))
        steps = []
        for sub in range(GDN_SCAN_UNROLL):
            cf = GDN_SCAN_UNROLL * it + sub
            cis = (cf, n_chunks - 1 - cf)
            chains = []
            for dirn in range(2):
                ci = cis[dirn]
                rows = pl.ds(pl.multiple_of(ci * CHUNK, CHUNK), CHUNK)
                q_c, k_c, v_c = q_ref[rows, :], k_ref[rows, :], v_ref[rows, :]
                for pp in range(npp):
                    ps = slice(pp * pw, (pp + 1) * pw)
                    chains.append(dict(dirn=dirn, pp=pp, rows=rows, tt=tt_ref[ci, dirn, pp],
                                       aqd=aqd_ref[ci, dirn, pp], dec=dec_ref[ci, dirn, pp],
                                       q=q_c[:, ps], k=k_c[:, ps], v=v_c[:, ps]))
            steps.append(chains)
        states = {(dirn, pp): [st_ref[dirn, pp, j] for j in range(2)]
                  for dirn in range(2) for pp in range(npp)}
        for chains in steps:
            for ch in chains:
                kbd = _pair_blockdiag(ch["k"].astype(F32)).astype(BF16)
                vbd = _pair_blockdiag(ch["v"].astype(F32)).astype(BF16)
                ch["uu"] = _dot(ch["tt"][:blk], vbd)
                ch["ww"] = _dot(ch["tt"][blk:], kbd)
        for chains in steps:
            for ch in chains:
                state = states[(ch["dirn"], ch["pp"])]
                lhs = jnp.concatenate([ch["ww"].astype(BF16), ch["q"]], axis=0)
                ch["prod"] = jnp.concatenate(
                    [_dot(lhs[:, hs], state[j].astype(BF16)) for j, hs in enumerate(heads)], axis=1)
            for ch in chains:
                state = states[(ch["dirn"], ch["pp"])]
                v_new_e = (ch["uu"][CHUNK:] - ch["prod"][CHUNK:blk]).astype(BF16)
                states[(ch["dirn"], ch["pp"])] = [
                    state[j] * ch["dec"][0:1, hs] + _dot_tn(ch["k"][:, hs], v_new_e[:, hs])
                    for j, hs in enumerate(heads)]
            for ch in chains:
                v_new = ch["uu"][:CHUNK] - ch["prod"][:CHUNK]
                rhs = jnp.concatenate([_pair_blockdiag(v_new), _pair_blockdiag(ch["prod"][blk:])], axis=0)
                ch["o"] = _dot(ch["aqd"], rhs.astype(BF16))
        for chains in steps:
            for dirn in range(2):
                mine = [ch for ch in chains if ch["dirn"] == dirn]
                o_refs[dirn][mine[0]["rows"], :] = jnp.concatenate([ch["o"] for ch in mine], axis=1)
        for (dirn, pp), state in states.items():
            for j in range(2):
                st_ref[dirn, pp, j] = state[j]
        return carry

    lax.fori_loop(0, n_chunks // GDN_SCAN_UNROLL, scan, 0)

    def finish(rb, carry):
        r0 = pl.multiple_of(rb * blk, blk)
        rows = pl.ds(r0, blk)
        o_ref[rows, :] = (of_ref[rows, :] + ob_ref[rows, :]).astype(o_ref.dtype)
        return carry

    lax.fori_loop(0, seq_len // blk, finish, 0)


def _gdn(p_main, p_small, tab, esum, eexp, batch, seq_len):
    t = batch * seq_len
    n_steps = GDN_QK // GDN_STEP_W
    n_chunks = seq_len // CHUNK
    npp = GDN_PAIRS_PER_STEP
    pw = 2 * GDN_DK
    sw = GDN_STEP_W
    kern = functools.partial(_gdn_kernel, seq_len=seq_len)
    return pl.pallas_call(
        kern,
        grid=(batch, n_steps),
        in_specs=[
            pl.BlockSpec((seq_len, sw), lambda b, p: (b, p)),
            pl.BlockSpec((seq_len, sw), lambda b, p: (b, GDN_QK // sw + p)),
            pl.BlockSpec((seq_len, sw), lambda b, p: (b, 2 * GDN_QK // sw + p)),
            pl.BlockSpec((seq_len, SMALL_W), lambda b, p: (b, 0)),
            pl.BlockSpec(tab.shape, lambda b, p: (0, 0)),
            pl.BlockSpec(esum.shape, lambda b, p: (0, 0, 0)),
            pl.BlockSpec((1, SMALL_W, npp * GDN_SCALARS * LANES), lambda b, p: (p, 0, 0)),
        ],
        out_specs=pl.BlockSpec((seq_len, sw), lambda b, p: (b, p)),
        out_shape=jax.ShapeDtypeStruct((t, GDN_V), BF16),
        scratch_shapes=[
            pltpu.VMEM((n_chunks, 2, npp, 4 * CHUNK, LANES), BF16),
            pltpu.VMEM((n_chunks, 2, npp, CHUNK, 2 * LANES), BF16),
            pltpu.VMEM((n_chunks, 2, npp, SUBLANES_F32, pw), F32),
            pltpu.VMEM((seq_len, sw), F32),
            pltpu.VMEM((seq_len, sw), F32),
            pltpu.VMEM((2, npp, 2, GDN_DK, GDN_DV), F32),
        ],
        compiler_params=pltpu.CompilerParams(dimension_semantics=("arbitrary", "arbitrary"),
                                             vmem_limit_bytes=VMEM_LIMIT),
        name="gdn_mixer",
    )(p_main, p_main, p_main, p_small, tab, esum, eexp)


def _pad_cols(w, width):
    return jnp.pad(w, ((0, 0), (0, width - w.shape[1])))


def _gla_params(ab_w_in, w_gate_fwd, b_gate_fwd, w_gate_bwd, b_gate_bwd):
    n_wide = 2 * GLA_QK + 2 * GLA_V
    lr0 = n_wide
    sg0 = lr0 + 2 * GLA_LOWRANK
    wide = ab_w_in[:, :n_wide]
    sgu = ab_w_in[:, sg0:sg0 + 2 * SGU_DIM]
    small = _pad_cols(ab_w_in[:, lr0:sg0], SMALL_W)
    w = jnp.concatenate([wide, sgu, small], axis=1).astype(BF16)
    wg = jnp.zeros((SMALL_W, 2 * GLA_QK), F32)
    wg = wg.at[:GLA_LOWRANK, :GLA_QK].set(w_gate_fwd)
    wg = wg.at[GLA_LOWRANK:2 * GLA_LOWRANK, GLA_QK:].set(w_gate_bwd)
    bg = jnp.concatenate([b_gate_fwd, b_gate_bwd])[None, :]
    return w, wg.astype(BF16), bg


def _gdn_params(gdn_w_in, a_log_fwd, dt_bias_fwd, a_log_bwd, dt_bias_bwd):
    n_main = GDN_CONV_DIM + GDN_V
    small = _pad_cols(gdn_w_in[:, n_main:], SMALL_W)
    w = jnp.concatenate([gdn_w_in[:, :n_main], small], axis=1).astype(BF16)
    pad = SMALL_W - 4 * GDN_HEADS
    zeros2 = jnp.zeros((2 * GDN_HEADS,), F32)
    a_log = jnp.concatenate([zeros2, a_log_fwd, a_log_bwd, jnp.zeros((pad,), F32)])
    dt_b = jnp.concatenate([zeros2, dt_bias_fwd, dt_bias_bwd, jnp.zeros((pad,), F32)])
    tab = jnp.zeros((SUBLANES_F32, SMALL_W), F32).at[0].set(a_log).at[1].set(dt_b)
    ch_head = jnp.arange(GDN_STEP_W) // GDN_DK
    col = jnp.arange(SMALL_W)
    esum_q = (col[None, :] == (GDN_NORM_COL + ch_head)[:, None])
    esum_k = (col[None, :] == (GDN_NORM_COL + GDN_STEP_HEADS + ch_head)[:, None])
    esum = jnp.stack([esum_q, esum_k]).astype(BF16)
    n_steps = GDN_QK // GDN_STEP_W
    lane = jnp.arange(GDN_PAIRS_PER_STEP * GDN_SCALARS * LANES)
    blk = lane // LANES
    n_cum_blk = 2 * GDN_PAIRS_PER_STEP
    pair = jnp.where(blk < n_cum_blk, blk // 2, (blk - n_cum_blk) // 4)
    quant = jnp.where(blk < n_cum_blk, blk % 2, 2 + (blk - n_cum_blk) % 4)
    local_head = 2 * pair + (lane % LANES) // CHUNK
    head = GDN_STEP_HEADS * jnp.arange(n_steps)[:, None] + local_head[None, :]
    src_gate = jnp.array([2 * GDN_HEADS, 3 * GDN_HEADS, 0, GDN_HEADS])
    src_norm = GDN_NORM_COL + GDN_STEP_HEADS * (quant - 4) + local_head
    src = jnp.where(quant[None, :] < 4, src_gate[jnp.minimum(quant, 3)][None, :] + head,
                    src_norm[None, :])
    eexp = (col[None, :, None] == src[:, None, :]).astype(BF16)
    return w, tab, esum, eexp


def kernel(x, norm_mix, norm_ffn, norm_final, ab_w_in, gla_w_gate_fwd, gla_b_gate_fwd, gla_w_gate_bwd, gla_b_gate_bwd, gla_norm, sgu_ln_g, sgu_ln_b, sgu_w_s, sgu_b_s, ab_w_out, gdn_w_in, gdn_conv_w, gdn_a_log_fwd, gdn_dt_bias_fwd, gdn_a_log_bwd, gdn_dt_bias_bwd, gdn_norm, gdn_w_out, ffn_w_up, ffn_conv_w, ffn_conv_b, ffn_w_down):
    batch, seq_len, d = x.shape
    t = batch * seq_len
    assert seq_len % ROW_TILE == 0 and seq_len % (2 * CHUNK) == 0
    h = x.reshape(t, d)

    w0, wg, bg = _gla_params(ab_w_in[0], gla_w_gate_fwd[0], gla_b_gate_fwd[0],
                             gla_w_gate_bwd[0], gla_b_gate_bwd[0])
    n_main0 = 2 * GLA_QK + 2 * GLA_V + 2 * SGU_DIM
    p0, s0 = _norm_proj(h, norm_mix[0][None, :], w0, None, seq_len, n_main0, 0, "gla_sgu_in_proj")
    o_a = _gla(p0, s0, wg, bg, gla_norm[0][None, :], batch, seq_len)
    b_full = jnp.repeat(sgu_b_s[0].T, SGU_GROUP_DIM, axis=1)
    h = _sgu_out(h, o_a, p0, sgu_ln_g[0][None, :], sgu_ln_b[0][None, :],
                 sgu_w_s[0].astype(BF16), b_full, ab_w_out[0].astype(BF16))
    h = _ffn(h, norm_ffn[0][None, :], ffn_w_up[0].astype(BF16), ffn_conv_w[0],
             ffn_conv_b[0][None, :], ffn_w_down[0].astype(BF16), seq_len, None, "ffn0")

    w1, tab, esum, eexp = _gdn_params(gdn_w_in[0], gdn_a_log_fwd[0], gdn_dt_bias_fwd[0],
                                      gdn_a_log_bwd[0], gdn_dt_bias_bwd[0])
    n_main1 = GDN_CONV_DIM + GDN_V
    p1, s1 = _norm_proj(h, norm_mix[1][None, :], w1, gdn_conv_w[0], seq_len, n_main1,
                        GDN_CONV_DIM, "gdn_in_proj")
    o_g = _gdn(p1, s1, tab, esum, eexp, batch, seq_len)
    h = _ffn(h, norm_ffn[1][None, :], ffn_w_up[1].astype(BF16), ffn_conv_w[1],
             ffn_conv_b[1][None, :], ffn_w_down[1].astype(BF16), seq_len,
             norm_final[None, :], "gdn_out_ffn1",
             gdn_out=(o_g, p1, gdn_norm[0][None, :], gdn_w_out[0].astype(BF16)))
    return h.reshape(batch, seq_len, d)
```

```python
import functools

import jax
import jax.numpy as jnp
from jax import lax
from jax.experimental import pallas as pl
from jax.experimental.pallas import tpu as pltpu

F32 = jnp.float32
BF16 = jnp.bfloat16

NORM_EPS = 1e-6
GLA_HEADS = 4
GLA_DK = 64
GLA_DV = 128
GLA_QK = GLA_HEADS * GLA_DK
GLA_V = GLA_HEADS * GLA_DV
GLA_LOWRANK = 16
GLA_GATE_NORMALIZER = 16.0
SGU_GROUPS = 4
SGU_GROUP_DIM = 128
SGU_DIM = SGU_GROUPS * SGU_GROUP_DIM
SGU_CHUNK = 128
GDN_HEADS = 8
GDN_DK = 128
GDN_DV = 128
GDN_QK = GDN_HEADS * GDN_DK
GDN_V = GDN_HEADS * GDN_DV
GDN_CONV_DIM = 2 * GDN_QK + GDN_V
FFN_DIM = 2816

LANES = 128
SUBLANES_F32 = 8
CHUNK = 64
SMALL_W = LANES

ROW_TILE = 1024
COL_TILE = 256
HALO = SUBLANES_F32
HALO_BF16 = 2 * SUBLANES_F32
VMEM_LIMIT = 56 * 1024 * 1024
GLA_UNROLL = 2


def _dot(a, b):
    return jnp.dot(a, b, preferred_element_type=F32)


def _dot_nt(a, b):
    return lax.dot_general(a, b, (((1,), (1,)), ((), ())), preferred_element_type=F32)


def _dot_tn(a, b):
    return lax.dot_general(a, b, (((0,), (0,)), ((), ())), preferred_element_type=F32)


def _split(a):
    hi = a.astype(BF16)
    lo = (a - hi.astype(F32)).astype(BF16)
    return hi, lo


def _dot_exact_lhs(l_bf16, a):
    hi, lo = _split(a)
    return _dot(l_bf16, hi) + _dot(l_bf16, lo)


def _rms(x, gain):
    ms = jnp.mean(x * x, axis=-1, keepdims=True)
    return x * lax.rsqrt(ms + NORM_EPS) * gain


def _sigmoid(x):
    return 1.0 / (1.0 + jnp.exp(-x))


def _silu(x):
    return x * _sigmoid(x)


def _softplus(x):
    return jnp.maximum(x, 0.0) + jnp.log(1.0 + jnp.exp(-jnp.abs(x)))


def _gelu_tanh(x):
    c = 0.7978845608028654
    return 0.5 * x * (1.0 + jnp.tanh(c * (x + 0.044715 * (x * x * x))))


def _iota(shape, dim):
    return lax.broadcasted_iota(jnp.int32, shape, dim)


def _shift_rows(g, first_row, last_row):
    n = g.shape[0]
    row = _iota(g.shape, 0)
    g_prev = jnp.where(row == 0, first_row, pltpu.roll(g, 1, axis=0))
    g_next = jnp.where(row == n - 1, last_row, pltpu.roll(g, n - 1, axis=0))
    return g_prev, g_next


def _halo_rows(gh, tiles_per_seq):
    i = pl.program_id(0)
    pos = i % tiles_per_seq
    keep_prev = (pos != 0).astype(F32)
    keep_next = (pos != tiles_per_seq - 1).astype(F32)
    prev_row = gh[HALO - 1:HALO, :] * keep_prev
    next_row = gh[HALO:HALO + 1, :] * keep_next
    return prev_row, next_row


def _normed_with_halo(x_ref, xp_ref, xn_ref, g_ref):
    g = g_ref[...]
    hn = _rms(x_ref[...], g).astype(BF16)
    halo = jnp.concatenate([xp_ref[...], xn_ref[...]], axis=0)
    hh = _rms(halo, g).astype(BF16)
    return hn, jnp.concatenate([hn, hh], axis=0)


def _norm_proj_kernel(*refs, n_main, conv_cols, tiles_per_seq):
    if conv_cols:
        x_ref, xp_ref, xn_ref, g_ref, w_ref, cw_ref, o_ref, s_ref = refs
        hn, hx = _normed_with_halo(x_ref, xp_ref, xn_ref, g_ref)
    else:
        x_ref, g_ref, w_ref, o_ref, s_ref = refs
        hn = _rms(x_ref[...], g_ref[...]).astype(BF16)
    tm = hn.shape[0]
    for c in range(0, n_main, COL_TILE):
        cs = slice(c, c + COL_TILE)
        if c < conv_cols:
            acc = _dot(hx, w_ref[:, cs])
            gp = acc[:tm]
            prev_row, next_row = _halo_rows(acc[tm:], tiles_per_seq)
            g_prev, g_next = _shift_rows(gp, prev_row, next_row)
            cw = cw_ref[:, cs]
            y = cw[0:1] * g_prev + cw[1:2] * gp + cw[2:3] * g_next
            o_ref[:, cs] = _silu(y).astype(o_ref.dtype)
        else:
            o_ref[:, cs] = _dot(hn, w_ref[:, cs]).astype(o_ref.dtype)
    s_ref[...] = _dot(hn, w_ref[:, n_main:])


def _halo_specs(d, n_rows, halo, col_blk=0):
    blocks_per_tile = ROW_TILE // halo
    last = n_rows // halo - 1
    prev = pl.BlockSpec((halo, d), lambda i: (jnp.maximum(i * blocks_per_tile - 1, 0), col_blk))
    nxt = pl.BlockSpec((halo, d), lambda i: (jnp.minimum((i + 1) * blocks_per_tile, last), col_blk))
    return prev, nxt


def _resident(shape):
    return pl.BlockSpec(shape, lambda i: (0,) * len(shape), pipeline_mode=pl.Buffered(1))


def _norm_proj(x2d, gain, w, conv_w, seq_len, n_main, conv_cols, name):
    t, d = x2d.shape
    n_all = w.shape[1]
    row = pl.BlockSpec((ROW_TILE, d), lambda i: (i, 0))
    in_specs = [row]
    args = [x2d]
    if conv_cols:
        prev, nxt = _halo_specs(d, t, HALO)
        in_specs += [prev, nxt]
        args += [x2d, x2d]
    in_specs += [_resident((1, d)), _resident((d, n_all))]
    args += [gain, w]
    if conv_cols:
        in_specs.append(_resident(conv_w.shape))
        args.append(conv_w)
    kern = functools.partial(_norm_proj_kernel, n_main=n_main, conv_cols=conv_cols,
                             tiles_per_seq=seq_len // ROW_TILE)
    return pl.pallas_call(
        kern,
        grid=(t // ROW_TILE,),
        in_specs=in_specs,
        out_specs=[pl.BlockSpec((ROW_TILE, n_main), lambda i: (i, 0)),
                   pl.BlockSpec((ROW_TILE, SMALL_W), lambda i: (i, 0))],
        out_shape=[jax.ShapeDtypeStruct((t, n_main), BF16),
                   jax.ShapeDtypeStruct((t, SMALL_W), F32)],
        compiler_params=pltpu.CompilerParams(dimension_semantics=("arbitrary",),
                                             vmem_limit_bytes=VMEM_LIMIT),
        name=name,
    )(*args)


def _with_halo_bf16(m_ref, mp_ref, mn_ref):
    return jnp.concatenate([m_ref[...].astype(F32), mp_ref[...].astype(F32)[HALO_BF16 - HALO:],
                            mn_ref[...].astype(F32)[:HALO]], axis=0)


def _ffn_kernel(*refs, tiles_per_seq, final_norm, gdn_out):
    refs = list(refs)
    x_ref, xp_ref, xn_ref = refs[:3]
    del refs[:3]
    x_ext = jnp.concatenate([x_ref[...], xp_ref[...], xn_ref[...]], axis=0)
    tm = x_ref.shape[0]
    if gdn_out:
        m_ref, mp_ref, mn_ref, z_ref, zp_ref, zn_ref, gn_ref, wo_ref = refs[:8]
        del refs[:8]
        m_ext = _with_halo_bf16(m_ref, mp_ref, mn_ref)
        z_ext = _with_halo_bf16(z_ref, zp_ref, zn_ref)
        parts = []
        for h in range(GDN_HEADS):
            hs = slice(h * GDN_DV, (h + 1) * GDN_DV)
            parts.append((_rms(m_ext[:, hs], gn_ref[...]) * _silu(z_ext[:, hs])).astype(BF16))
        x_ext = x_ext + _dot(jnp.concatenate(parts, axis=1), wo_ref[...])
    g_ref, wup_ref, cw_ref, cb_ref, wdn_ref = refs[:5]
    del refs[:5]
    if final_norm:
        gf_ref = refs.pop(0)
    o_ref, act_ref = refs
    hx = _rms(x_ext, g_ref[...]).astype(BF16)
    hn = hx[:tm]
    for c in range(0, FFN_DIM, COL_TILE):
        cs = slice(c, c + COL_TILE)
        acc = _dot(hx, wup_ref[:, cs])
        up = _dot(hn, wup_ref[:, FFN_DIM + c:FFN_DIM + c + COL_TILE])
        gp = acc[:tm]
        prev_row, next_row = _halo_rows(acc[tm:], tiles_per_seq)
        g_prev, g_next = _shift_rows(gp, prev_row, next_row)
        cw = cw_ref[:, cs]
        gate = cw[0:1] * g_prev + cw[1:2] * gp + cw[2:3] * g_next + cb_ref[:, cs]
        act_ref[:, cs] = (_silu(gate) * up).astype(BF16)
    out = x_ext[:tm] + _dot(act_ref[...], wdn_ref[...])
    if final_norm:
        out = _rms(out, gf_ref[...])
    o_ref[...] = out


def _layer_resident(stacked, layer):
    shape = stacked.shape[1:]
    return pl.BlockSpec((None,) + shape, lambda i: (layer,) + (0,) * len(shape),
                        pipeline_mode=pl.Buffered(1))


def _ffn(h2d, gain, w_up, conv_w, conv_b, w_down, layer, seq_len, final_gain, name, gdn_out=None):
    t, d = h2d.shape
    row = pl.BlockSpec((ROW_TILE, d), lambda i: (i, 0))
    prev, nxt = _halo_specs(d, t, HALO)
    in_specs = [row, prev, nxt]
    args = [h2d, h2d, h2d]
    if gdn_out is not None:
        o_gdn, p_main, gnorm, w_out = gdn_out
        z_blk = GDN_CONV_DIM // GDN_V
        for arr, cb in ((o_gdn, 0), (p_main, z_blk)):
            prev_b, nxt_b = _halo_specs(GDN_V, t, HALO_BF16, cb)
            in_specs += [pl.BlockSpec((ROW_TILE, GDN_V), lambda i, cb=cb: (i, cb)), prev_b, nxt_b]
            args += [arr, arr, arr]
        in_specs += [_resident(gnorm.shape), _resident(w_out.shape)]
        args += [gnorm, w_out]
    in_specs += [_resident((1, d)), _layer_resident(w_up, layer), _resident(conv_w.shape),
                 _resident(conv_b.shape), _layer_resident(w_down, layer)]
    args += [gain, w_up, conv_w, conv_b, w_down]
    if final_gain is not None:
        in_specs.append(_resident((1, d)))
        args.append(final_gain)
    kern = functools.partial(_ffn_kernel, tiles_per_seq=seq_len // ROW_TILE,
                             final_norm=final_gain is not None, gdn_out=gdn_out is not None)
    return pl.pallas_call(
        kern,
        grid=(t // ROW_TILE,),
        in_specs=in_specs,
        out_specs=row,
        out_shape=jax.ShapeDtypeStruct((t, d), F32),
        scratch_shapes=[pltpu.VMEM((ROW_TILE, FFN_DIM), BF16)],
        compiler_params=pltpu.CompilerParams(dimension_semantics=("arbitrary",),
                                             vmem_limit_bytes=VMEM_LIMIT),
        name=name,
    )(*args)


def _pair_rows(x):
    lo = _iota(x.shape, 1) < CHUNK
    return jnp.concatenate([jnp.where(lo, x, 0.0), jnp.where(lo, 0.0, x)], axis=0)


def _pair_blockdiag(x):
    lo = _iota(x.shape, 1) < LANES
    return jnp.concatenate([jnp.where(lo, x, 0.0), jnp.where(lo, 0.0, x)], axis=0)


def _block_tri_ones(lower, n=2 * CHUNK):
    r = _iota((n, n), 0)
    c = _iota((n, n), 1)
    tri = jnp.where((c <= r) if lower else (c >= r), 1.0, 0.0)
    return jnp.where((r // CHUNK) == (c // CHUNK), tri, 0.0).astype(BF16)


def _packed_tri_mask(lower, strict=False):
    r = _iota((CHUNK, LANES), 0)
    c = _iota((CHUNK, LANES), 1) & (CHUNK - 1)
    if lower:
        return (c < r) if strict else (c <= r)
    return (c > r) if strict else (c >= r)


def _gla_kernel(q_ref, k_ref, v_ref, gate_ref, lr_ref, wg_ref, bg_ref, gn_ref, o_ref,
                la_ref, vt_ref, of_ref, ob_ref, st_ref, *, seq_len):
    n_pairs = GLA_HEADS // 2
    n_dbl = seq_len // (2 * CHUNK)
    blk = 2 * CHUNK

    def prep(rb, carry):
        r0 = pl.multiple_of(rb * blk, blk)
        rows = pl.ds(r0, blk)
        z = _dot(lr_ref[rows, :].astype(BF16), wg_ref[...]) + bg_ref[...]
        log_sig = jnp.minimum(z, 0.0) - jnp.log(1.0 + jnp.exp(-jnp.abs(z)))
        la_ref[rows, :] = log_sig * (1.0 / GLA_GATE_NORMALIZER)
        for cb in range(GLA_V // LANES):
            vblk = v_ref[rows, cb * LANES:(cb + 1) * LANES].astype(F32)
            vt_ref[rb, cb * LANES:(cb + 1) * LANES, :] = vblk.T.astype(BF16)
        return carry

    lax.fori_loop(0, n_dbl, prep, 0)
    st_ref[...] = jnp.zeros(st_ref.shape, F32)

    tri = (_block_tri_ones(True), _block_tri_ones(False))
    masks = (_packed_tri_mask(True), _packed_tri_mask(False))
    st_mask = (_iota((2 * GLA_DV, LANES), 0) // GLA_DV) == (_iota((2 * GLA_DV, LANES), 1) // GLA_DK)
    first_half = _iota((blk, GLA_QK), 0) < CHUNK
    o_refs = (of_ref, ob_ref)

    def body(it, carry):
        steps = []
        for sub in range(GLA_UNROLL):
            df = GLA_UNROLL * it + sub
            dbs = (df, n_dbl - 1 - df)
            chains = []
            for dirn in range(2):
                rows = pl.ds(pl.multiple_of(dbs[dirn] * blk, blk), blk)
                la = la_ref[rows, dirn * GLA_QK:(dirn + 1) * GLA_QK]
                q2, k2, v2, vt = q_ref[rows, :], k_ref[rows, :], v_ref[rows, :], vt_ref[dbs[dirn]]
                cum = _dot_exact_lhs(tri[dirn], la)
                if dirn == 0:
                    tots = (cum[CHUNK - 1:CHUNK, :], cum[blk - 1:blk, :])
                else:
                    tots = (cum[0:1, :], cum[CHUNK:CHUNK + 1, :])
                tot_rows = jnp.where(first_half, tots[0], tots[1])
                q = q2.astype(F32) * (GLA_DK ** -0.5)
                k = k2.astype(F32)
                q_dec = (q * jnp.exp(cum)).astype(BF16)
                k_inv = k * jnp.exp(-cum)
                k_end = k * jnp.exp(tot_rows - cum)
                v32 = v2.astype(F32)
                for p in range(n_pairs):
                    ls = slice(p * LANES, (p + 1) * LANES)
                    vs = slice(p * 2 * GLA_DV, (p + 1) * 2 * GLA_DV)
                    chains.append(dict(dirn=dirn, p=p, rows=rows,
                                       order=(0, 1) if dirn == 0 else (1, 0),
                                       qd=q_dec[:, ls], k_inv=k_inv[:, ls], k_end=k_end[:, ls],
                                       v=v32[:, vs], vt=vt[vs, :],
                                       dec=[jnp.exp(tots[cc][:, ls]) for cc in range(2)],
                                       o=[None, None]))
            steps.append(chains)
        states = {(dirn, p): st_ref[dirn, p] for dirn in range(2) for p in range(n_pairs)}
        for chains in steps:
            for ch in chains:
                for cc in range(2):
                    rs = slice(cc * CHUNK, (cc + 1) * CHUNK)
                    kbd = _pair_rows(ch["k_inv"][rs]).astype(BF16)
                    sc = jnp.where(masks[ch["dirn"]], _dot_nt(ch["qd"][rs], kbd), 0.0).astype(BF16)
                    ch["o"][cc] = _dot(sc, _pair_blockdiag(ch["v"][rs]).astype(BF16))
        for chains in steps:
            for step in range(2):
                for ch in chains:
                    cc = ch["order"][step]
                    rs = slice(cc * CHUNK, (cc + 1) * CHUNK)
                    state = states[(ch["dirn"], ch["p"])]
                    ch["o"][cc] = ch["o"][cc] + _dot_nt(ch["qd"][rs], state.astype(BF16))
                    in_chunk = (_iota((blk, LANES), 0) // CHUNK) == cc
                    rhs = jnp.where(in_chunk, ch["k_end"], 0.0).astype(BF16)
                    upd = _dot(ch["vt"], rhs)
                    states[(ch["dirn"], ch["p"])] = state * ch["dec"][cc] + jnp.where(st_mask, upd, 0.0)
        for chains in steps:
            for dirn in range(2):
                mine = [ch for ch in chains if ch["dirn"] == dirn]
                o_refs[dirn][mine[0]["rows"], :] = jnp.concatenate(
                    [jnp.concatenate(ch["o"], axis=0) for ch in mine], axis=1)
        for (dirn, p), state in states.items():
            st_ref[dirn, p] = state
        return carry

    lax.fori_loop(0, n_dbl // GLA_UNROLL, body, 0)

    def finish(rb, carry):
        r0 = pl.multiple_of(rb * blk, blk)
        rows = pl.ds(r0, blk)
        for h in range(GLA_HEADS):
            hs = slice(h * GLA_DV, (h + 1) * GLA_DV)
            o = of_ref[rows, hs] + ob_ref[rows, hs]
            y = _rms(o, gn_ref[...]) * _silu(gate_ref[rows, hs].astype(F32))
            o_ref[rows, hs] = y.astype(o_ref.dtype)
        return carry

    lax.fori_loop(0, n_dbl, finish, 0)


def _gla(p_main, p_small, wg, bg, gnorm, batch, seq_len):
    t = batch * seq_len
    kern = functools.partial(_gla_kernel, seq_len=seq_len)
    qk_blk = GLA_QK
    return pl.pallas_call(
        kern,
        grid=(batch,),
        in_specs=[
            pl.BlockSpec((seq_len, GLA_QK), lambda b: (b, 0)),
            pl.BlockSpec((seq_len, GLA_QK), lambda b: (b, 1)),
            pl.BlockSpec((seq_len, GLA_V), lambda b: (b, (2 * qk_blk) // GLA_V)),
            pl.BlockSpec((seq_len, GLA_V), lambda b: (b, (2 * qk_blk) // GLA_V + 1)),
            pl.BlockSpec((seq_len, SMALL_W), lambda b: (b, 0)),
            _resident(wg.shape), _resident(bg.shape), _resident(gnorm.shape),
        ],
        out_specs=pl.BlockSpec((seq_len, GLA_V), lambda b: (b, 0)),
        out_shape=jax.ShapeDtypeStruct((t, GLA_V), BF16),
        scratch_shapes=[
            pltpu.VMEM((seq_len, 2 * GLA_QK), F32),
            pltpu.VMEM((seq_len // (2 * CHUNK), GLA_V, 2 * CHUNK), BF16),
            pltpu.VMEM((seq_len, GLA_V), F32),
            pltpu.VMEM((seq_len, GLA_V), F32),
            pltpu.VMEM((2, GLA_HEADS // 2, 2 * GLA_DV, LANES), F32),
        ],
        compiler_params=pltpu.CompilerParams(dimension_semantics=("arbitrary",),
                                             vmem_limit_bytes=VMEM_LIMIT),
        name="gla_mixer",
    )(p_main, p_main, p_main, p_main, p_small, wg, bg, gnorm)


def _sgu_out_kernel(h_ref, oa_ref, su_ref, sv_ref, lng_ref, lnb_ref, ws_ref, bs_ref, wo_ref, o_ref):
    tm = h_ref.shape[0]
    u = _gelu_tanh(su_ref[...].astype(F32))
    g = _gelu_tanh(sv_ref[...].astype(F32))
    mu = jnp.mean(g, axis=-1, keepdims=True)
    gc = g - mu
    var = jnp.mean(gc * gc, axis=-1, keepdims=True)
    vv = (gc * lax.rsqrt(var + NORM_EPS) * lng_ref[...] + lnb_ref[...]).astype(BF16)
    rows = []
    for c in range(tm // SGU_CHUNK):
        rs = slice(c * SGU_CHUNK, (c + 1) * SGU_CHUNK)
        cols = []
        for gi in range(SGU_GROUPS):
            gs = slice(gi * SGU_GROUP_DIM, (gi + 1) * SGU_GROUP_DIM)
            cols.append(_dot(ws_ref[gi], vv[rs, gs]))
        rows.append(jnp.concatenate(cols, axis=1) + bs_ref[...])
    mixed = jnp.concatenate(rows, axis=0)
    ob = (u * mixed).astype(BF16)
    acc = _dot(oa_ref[...], wo_ref[:GLA_V, :]) + _dot(ob, wo_ref[GLA_V:, :])
    o_ref[...] = h_ref[...] + acc


def _sgu_out(h2d, o_a, p_main, ln_g, ln_b, w_s, b_full, w_out):
    t, d = h2d.shape
    su_blk = (2 * GLA_QK + 2 * GLA_V) // SGU_DIM
    return pl.pallas_call(
        _sgu_out_kernel,
        grid=(t // ROW_TILE,),
        in_specs=[
            pl.BlockSpec((ROW_TILE, d), lambda i: (i, 0)),
            pl.BlockSpec((ROW_TILE, GLA_V), lambda i: (i, 0)),
            pl.BlockSpec((ROW_TILE, SGU_DIM), lambda i: (i, su_blk)),
            pl.BlockSpec((ROW_TILE, SGU_DIM), lambda i: (i, su_blk + 1)),
            _resident(ln_g.shape), _resident(ln_b.shape), _resident(w_s.shape),
            _resident(b_full.shape), _resident(w_out.shape),
        ],
        out_specs=pl.BlockSpec((ROW_TILE, d), lambda i: (i, 0)),
        out_shape=jax.ShapeDtypeStruct((t, d), F32),
        compiler_params=pltpu.CompilerParams(dimension_semantics=("arbitrary",),
                                             vmem_limit_bytes=VMEM_LIMIT),
        name="sgu_out_proj",
    )(h2d, o_a, p_main, p_main, ln_g, ln_b, w_s, b_full, w_out)


GDN_SCALARS = 6
GDN_PAIRS_PER_STEP = 2
GDN_STEP_W = GDN_PAIRS_PER_STEP * 2 * GDN_DK
GDN_NORM_COL = 4 * GDN_HEADS
GDN_STEP_HEADS = 2 * GDN_PAIRS_PER_STEP
GDN_SCAN_UNROLL = 4
GDN_PRE_CHUNKS = 4


def _packed_product(x, y):
    return _dot(x.astype(BF16), _pair_rows(y).astype(BF16))


def _gdn_kernel(q_ref, k_ref, v_ref, sm_ref, tab_ref, esum_ref, eexp_ref, o_ref,
                tt_ref, aqd_ref, dec_ref, of_ref, ob_ref, st_ref, *, seq_len):
    n_chunks = seq_len // CHUNK
    blk = 2 * CHUNK
    npp = GDN_PAIRS_PER_STEP
    pw = 2 * GDN_DK

    pblk = GDN_PRE_CHUNKS * CHUNK
    tri = (_block_tri_ones(True, pblk), _block_tri_ones(False, pblk))
    incl = (_packed_tri_mask(True), _packed_tri_mask(False))
    strict = (_packed_tri_mask(True, strict=True), _packed_tri_mask(False, strict=True))
    diag = _iota((CHUNK, LANES), 0) == (_iota((CHUNK, LANES), 1) & (CHUNK - 1))
    lo_half = _iota((SUBLANES_F32, LANES), 1) < CHUNK
    eye = jnp.where(diag, 1.0, 0.0)

    def row_form(col_form):
        return jnp.sum(jnp.where(diag, col_form, 0.0), axis=0, keepdims=True)

    def precompute(it, carry):
        rows = pl.ds(pl.multiple_of(it * pblk, pblk), pblk)
        q2 = q_ref[rows, :]
        k2 = k_ref[rows, :]
        sm = sm_ref[rows, :]
        a_exp = jnp.exp(tab_ref[0:1, :])
        dt_b = tab_ref[1:2, :]

        lane = _iota(sm.shape, 1)
        gates = jnp.where(lane < 2 * GDN_HEADS, _sigmoid(sm), -a_exp * _softplus(sm + dt_b))
        qf = q2.astype(F32)
        kf = k2.astype(F32)
        ssq = _dot((qf * qf).astype(BF16), esum_ref[0]) + _dot((kf * kf).astype(BF16), esum_ref[1])
        inv = lax.rsqrt(ssq + NORM_EPS)
        is_qn = jnp.abs(2 * lane - (2 * GDN_NORM_COL + GDN_STEP_HEADS - 1)) < GDN_STEP_HEADS
        inv = jnp.where(is_qn, inv * (GDN_DK ** -0.5), inv)
        table = jnp.where(lane < GDN_NORM_COL, gates, inv)
        cum_f = _dot_exact_lhs(tri[0], table)
        cum_b = _dot_exact_lhs(tri[1], table)
        band = lane // GDN_HEADS
        table = jnp.where(band == 2, cum_f, jnp.where(band == 3, cum_b, table))
        hi, lo = _split(table)
        n_cum = 2 * npp * LANES
        e_cum = eexp_ref[0, :, :n_cum]
        ex_cum = _dot(hi, e_cum) + _dot(lo, e_cum)
        ex_rest = _dot(hi, eexp_ref[0, :, n_cum:])

        chains = []
        for cc in range(GDN_PRE_CHUNKS):
            rs = slice(cc * CHUNK, (cc + 1) * CHUNK)
            for pp in range(npp):
                ps = slice(pp * pw, (pp + 1) * pw)
                cols = [ex_cum[rs, (2 * pp + j) * LANES:(2 * pp + j + 1) * LANES] for j in range(2)]
                cols += [ex_rest[rs, (4 * pp + j) * LANES:(4 * pp + j + 1) * LANES] for j in range(4)]
                rq_c = cols[4]
                rk_c = cols[5]
                rk_r = row_form(rk_c)
                kbd = _pair_blockdiag(kf[rs, ps]).astype(BF16)
                gram = _dot_nt(jnp.concatenate([k2[rs, ps], q2[rs, ps]], axis=0), kbd)
                kk = gram[:CHUNK] * rk_c * rk_r
                qk = gram[CHUNK:] * rq_c * rk_r
                for dirn in range(2):
                    cum_c = cols[dirn]
                    beta_c = cols[2 + dirn]
                    tot_r = cum_c[CHUNK - 1:CHUNK, :] if dirn == 0 else cum_c[0:1, :]
                    cum_r = row_form(cum_c)
                    beta_r = row_form(beta_c)
                    decay = jnp.exp(jnp.where(incl[dirn], cum_c - cum_r, -1e30))
                    a = jnp.where(strict[dirn], kk * beta_c * decay, 0.0)
                    chains.append(dict(cc=cc, pp=pp, dirn=dirn, rq_c=rq_c, rk_c=rk_c, rk_r=rk_r,
                                       qk=qk, cum_c=cum_c, cum_r=cum_r, beta_r=beta_r, tot_r=tot_r,
                                       decay=decay, pw_a=a, inv_m=eye - a))
        for ch in chains:
            ch["pw_a"] = _packed_product(ch["pw_a"], ch["pw_a"])
        for _ in range(4):
            for ch in chains:
                lhs = jnp.concatenate([ch["inv_m"], ch["pw_a"]], axis=0)
                both = _packed_product(lhs, ch["pw_a"])
                ch["inv_m"] = ch["inv_m"] + both[:CHUNK]
                ch["pw_a"] = both[CHUNK:]
        for ch in chains:
            ch["inv_m"] = ch["inv_m"] + _packed_product(ch["inv_m"], ch["pw_a"])
        for ch in chains:
            t_u = ch["inv_m"] * ch["beta_r"]
            t_w = t_u * (jnp.exp(ch["cum_r"]) * ch["rk_r"])
            e_c = ch["rk_c"] * jnp.exp(ch["tot_r"] - ch["cum_c"])
            ch["tt"] = jnp.concatenate([t_u, t_u * e_c, t_w, t_w * e_c], axis=0).astype(BF16)
            d_q = jnp.where(diag, ch["rq_c"] * jnp.exp(ch["cum_c"]), 0.0)
            ch["aqd"] = jnp.concatenate([ch["qk"] * ch["decay"], d_q], axis=1).astype(BF16)
            tot8 = jnp.broadcast_to(ch["tot_r"], (SUBLANES_F32, LANES))
            tot8r = pltpu.roll(tot8, CHUNK, axis=1)
            ch["dec"] = jnp.exp(jnp.concatenate([jnp.where(lo_half, tot8, tot8r),
                                                 jnp.where(lo_half, tot8r, tot8)], axis=1))
        for ch in chains:
            ci = GDN_PRE_CHUNKS * it + ch["cc"]
            tt_ref[ci, ch["dirn"], ch["pp"]] = ch["tt"]
            aqd_ref[ci, ch["dirn"], ch["pp"]] = ch["aqd"]
            dec_ref[ci, ch["dirn"], ch["pp"]] = ch["dec"]
        return carry

    lax.fori_loop(0, seq_len // pblk, precompute, 0)
    st_ref[...] = jnp.zeros(st_ref.shape, F32)

    o_refs = (of_ref, ob_ref)

    def scan(it, carry):
        heads = (slice(0, GDN_DV), slice(GDN_DV, 2 * GDN_DV))
        steps = []
        for sub in range(GDN_SCAN_UNROLL):
            cf = GDN_SCAN_UNROLL * it + sub
            cis = (cf, n_chunks - 1 - cf)
            chains = []
            for dirn in range(2):
                ci = cis[dirn]
                rows = pl.ds(pl.multiple_of(ci * CHUNK, CHUNK), CHUNK)
                q_c, k_c, v_c = q_ref[rows, :], k_ref[rows, :], v_ref[rows, :]
                for pp in range(npp):
                    ps = slice(pp * pw, (pp + 1) * pw)
                    chains.append(dict(dirn=dirn, pp=pp, rows=rows, tt=tt_ref[ci, dirn, pp],
                                       aqd=aqd_ref[ci, dirn, pp], dec=dec_ref[ci, dirn, pp],
                                       q=q_c[:, ps], k=k_c[:, ps], v=v_c[:, ps]))
            steps.append(chains)
        states = {(dirn, pp): [st_ref[dirn, pp, j] for j in range(2)]
                  for dirn in range(2) for pp in range(npp)}
        for chains in steps:
            for ch in chains:
                kbd = _pair_blockdiag(ch["k"].astype(F32)).astype(BF16)
                vbd = _pair_blockdiag(ch["v"].astype(F32)).astype(BF16)
                ch["uu"] = _dot(ch["tt"][:blk], vbd)
                ch["ww"] = _dot(ch["tt"][blk:], kbd)
        for chains in steps:
            for ch in chains:
                state = states[(ch["dirn"], ch["pp"])]
                lhs = jnp.concatenate([ch["ww"].astype(BF16), ch["q"]], axis=0)
                ch["prod"] = jnp.concatenate(
                    [_dot(lhs[:, hs], state[j].astype(BF16)) for j, hs in enumerate(heads)], axis=1)
            for ch in chains:
                state = states[(ch["dirn"], ch["pp"])]
                v_new_e = (ch["uu"][CHUNK:] - ch["prod"][CHUNK:blk]).astype(BF16)
                states[(ch["dirn"], ch["pp"])] = [
                    state[j] * ch["dec"][0:1, hs] + _dot_tn(ch["k"][:, hs], v_new_e[:, hs])
                    for j, hs in enumerate(heads)]
            for ch in chains:
                v_new = ch["uu"][:CHUNK] - ch["prod"][:CHUNK]
                rhs = jnp.concatenate([_pair_blockdiag(v_new), _pair_blockdiag(ch["prod"][blk:])], axis=0)
                ch["o"] = _dot(ch["aqd"], rhs.astype(BF16))
        for chains in steps:
            for dirn in range(2):
                mine = [ch for ch in chains if ch["dirn"] == dirn]
                o_refs[dirn][mine[0]["rows"], :] = jnp.concatenate([ch["o"] for ch in mine], axis=1)
        for (dirn, pp), state in states.items():
            for j in range(2):
                st_ref[dirn, pp, j] = state[j]
        return carry

    lax.fori_loop(0, n_chunks // GDN_SCAN_UNROLL, scan, 0)

    def finish(rb, carry):
        r0 = pl.multiple_of(rb * blk, blk)
        rows = pl.ds(r0, blk)
        o_ref[rows, :] = (of_ref[rows, :] + ob_ref[rows, :]).astype(o_ref.dtype)
        return carry

    lax.fori_loop(0, seq_len // blk, finish, 0)


def _gdn(p_main, p_small, tab, esum, eexp, batch, seq_len):
    t = batch * seq_len
    n_steps = GDN_QK // GDN_STEP_W
    n_chunks = seq_len // CHUNK
    npp = GDN_PAIRS_PER_STEP
    pw = 2 * GDN_DK
    sw = GDN_STEP_W
    kern = functools.partial(_gdn_kernel, seq_len=seq_len)
    return pl.pallas_call(
        kern,
        grid=(batch, n_steps),
        in_specs=[
            pl.BlockSpec((seq_len, sw), lambda b, p: (b, p)),
            pl.BlockSpec((seq_len, sw), lambda b, p: (b, GDN_QK // sw + p)),
            pl.BlockSpec((seq_len, sw), lambda b, p: (b, 2 * GDN_QK // sw + p)),
            pl.BlockSpec((seq_len, SMALL_W), lambda b, p: (b, 0)),
            pl.BlockSpec(tab.shape, lambda b, p: (0, 0)),
            pl.BlockSpec(esum.shape, lambda b, p: (0, 0, 0)),
            pl.BlockSpec((1, SMALL_W, npp * GDN_SCALARS * LANES), lambda b, p: (p, 0, 0)),
        ],
        out_specs=pl.BlockSpec((seq_len, sw), lambda b, p: (b, p)),
        out_shape=jax.ShapeDtypeStruct((t, GDN_V), BF16),
        scratch_shapes=[
            pltpu.VMEM((n_chunks, 2, npp, 4 * CHUNK, LANES), BF16),
            pltpu.VMEM((n_chunks, 2, npp, CHUNK, 2 * LANES), BF16),
            pltpu.VMEM((n_chunks, 2, npp, SUBLANES_F32, pw), F32),
            pltpu.VMEM((seq_len, sw), F32),
            pltpu.VMEM((seq_len, sw), F32),
            pltpu.VMEM((2, npp, 2, GDN_DK, GDN_DV), F32),
        ],
        compiler_params=pltpu.CompilerParams(dimension_semantics=("arbitrary", "arbitrary"),
                                             vmem_limit_bytes=VMEM_LIMIT),
        name="gdn_mixer",
    )(p_main, p_main, p_main, p_small, tab, esum, eexp)


def _pad_cols(w, width):
    return jnp.pad(w, ((0, 0), (0, width - w.shape[1])))


def _gla_params(ab_w_in, w_gate_fwd, b_gate_fwd, w_gate_bwd, b_gate_bwd):
    n_wide = 2 * GLA_QK + 2 * GLA_V
    lr0 = n_wide
    sg0 = lr0 + 2 * GLA_LOWRANK
    wide = ab_w_in[:, :n_wide]
    sgu = ab_w_in[:, sg0:sg0 + 2 * SGU_DIM]
    small = _pad_cols(ab_w_in[:, lr0:sg0], SMALL_W)
    w = jnp.concatenate([wide, sgu, small], axis=1).astype(BF16)
    wg = jnp.zeros((SMALL_W, 2 * GLA_QK), F32)
    wg = wg.at[:GLA_LOWRANK, :GLA_QK].set(w_gate_fwd)
    wg = wg.at[GLA_LOWRANK:2 * GLA_LOWRANK, GLA_QK:].set(w_gate_bwd)
    bg = jnp.concatenate([b_gate_fwd, b_gate_bwd])[None, :]
    return w, wg.astype(BF16), bg


def _gdn_params(gdn_w_in, a_log_fwd, dt_bias_fwd, a_log_bwd, dt_bias_bwd):
    n_main = GDN_CONV_DIM + GDN_V
    small = _pad_cols(gdn_w_in[:, n_main:], SMALL_W)
    w = jnp.concatenate([gdn_w_in[:, :n_main], small], axis=1).astype(BF16)
    pad = SMALL_W - 4 * GDN_HEADS
    zeros2 = jnp.zeros((2 * GDN_HEADS,), F32)
    a_log = jnp.concatenate([zeros2, a_log_fwd, a_log_bwd, jnp.zeros((pad,), F32)])
    dt_b = jnp.concatenate([zeros2, dt_bias_fwd, dt_bias_bwd, jnp.zeros((pad,), F32)])
    tab = jnp.zeros((SUBLANES_F32, SMALL_W), F32).at[0].set(a_log).at[1].set(dt_b)
    ch_head = jnp.arange(GDN_STEP_W) // GDN_DK
    col = jnp.arange(SMALL_W)
    esum_q = (col[None, :] == (GDN_NORM_COL + ch_head)[:, None])
    esum_k = (col[None, :] == (GDN_NORM_COL + GDN_STEP_HEADS + ch_head)[:, None])
    esum = jnp.stack([esum_q, esum_k]).astype(BF16)
    n_steps = GDN_QK // GDN_STEP_W
    lane = jnp.arange(GDN_PAIRS_PER_STEP * GDN_SCALARS * LANES)
    blk = lane // LANES
    n_cum_blk = 2 * GDN_PAIRS_PER_STEP
    pair = jnp.where(blk < n_cum_blk, blk // 2, (blk - n_cum_blk) // 4)
    quant = jnp.where(blk < n_cum_blk, blk % 2, 2 + (blk - n_cum_blk) % 4)
    local_head = 2 * pair + (lane % LANES) // CHUNK
    head = GDN_STEP_HEADS * jnp.arange(n_steps)[:, None] + local_head[None, :]
    src_gate = jnp.array([2 * GDN_HEADS, 3 * GDN_HEADS, 0, GDN_HEADS])
    src_norm = GDN_NORM_COL + GDN_STEP_HEADS * (quant - 4) + local_head
    src = jnp.where(quant[None, :] < 4, src_gate[jnp.minimum(quant, 3)][None, :] + head,
                    src_norm[None, :])
    eexp = (col[None, :, None] == src[:, None, :]).astype(BF16)
    return w, tab, esum, eexp


def kernel(x, norm_mix, norm_ffn, norm_final, ab_w_in, gla_w_gate_fwd, gla_b_gate_fwd, gla_w_gate_bwd, gla_b_gate_bwd, gla_norm, sgu_ln_g, sgu_ln_b, sgu_w_s, sgu_b_s, ab_w_out, gdn_w_in, gdn_conv_w, gdn_a_log_fwd, gdn_dt_bias_fwd, gdn_a_log_bwd, gdn_dt_bias_bwd, gdn_norm, gdn_w_out, ffn_w_up, ffn_conv_w, ffn_conv_b, ffn_w_down):
    batch, seq_len, d = x.shape
    t = batch * seq_len
    assert seq_len % ROW_TILE == 0 and seq_len % (2 * CHUNK) == 0
    h = x.reshape(t, d)

    w0, wg, bg = _gla_params(ab_w_in[0], gla_w_gate_fwd[0], gla_b_gate_fwd[0],
                             gla_w_gate_bwd[0], gla_b_gate_bwd[0])
    n_main0 = 2 * GLA_QK + 2 * GLA_V + 2 * SGU_DIM
    p0, s0 = _norm_proj(h, norm_mix[0][None, :], w0, None, seq_len, n_main0, 0, "gla_sgu_in_proj")
    o_a = _gla(p0, s0, wg, bg, gla_norm[0][None, :], batch, seq_len)
    b_full = jnp.repeat(sgu_b_s[0].T, SGU_GROUP_DIM, axis=1)
    h = _sgu_out(h, o_a, p0, sgu_ln_g[0][None, :], sgu_ln_b[0][None, :],
                 sgu_w_s[0].astype(BF16), b_full, ab_w_out[0].astype(BF16))
    w_up_all = ffn_w_up.astype(BF16)
    w_down_all = ffn_w_down.astype(BF16)
    h = _ffn(h, norm_ffn[0][None, :], w_up_all, ffn_conv_w[0], ffn_conv_b[0][None, :], w_down_all,
             0, seq_len, None, "ffn0")

    w1, tab, esum, eexp = _gdn_params(gdn_w_in[0], gdn_a_log_fwd[0], gdn_dt_bias_fwd[0],
                                      gdn_a_log_bwd[0], gdn_dt_bias_bwd[0])
    n_main1 = GDN_CONV_DIM + GDN_V
    p1, s1 = _norm_proj(h, norm_mix[1][None, :], w1, gdn_conv_w[0], seq_len, n_main1,
                        GDN_CONV_DIM, "gdn_in_proj")
    o_g = _gdn(p1, s1, tab, esum, eexp, batch, seq_len)
    h = _ffn(h, norm_ffn[1][None, :], w_up_all, ffn_conv_w[1], ffn_conv_b[1][None, :], w_down_all,
             1, seq_len, norm_final[None, :], "gdn_out_ffn1",
             gdn_out=(o_g, p1, gdn_norm[0][None, :], gdn_w_out[0].astype(BF16)))
    return h.reshape(batch, seq_len, d)
```

```python
import functools

import jax
import jax.numpy as jnp
from jax import lax
from jax.experimental import pallas as pl
from jax.experimental.pallas import tpu as pltpu

F32 = jnp.float32
BF16 = jnp.bfloat16

NORM_EPS = 1e-6
GLA_HEADS = 4
GLA_DK = 64
GLA_DV = 128
GLA_QK = GLA_HEADS * GLA_DK
GLA_V = GLA_HEADS * GLA_DV
GLA_LOWRANK = 16
GLA_GATE_NORMALIZER = 16.0
SGU_GROUPS = 4
SGU_GROUP_DIM = 128
SGU_DIM = SGU_GROUPS * SGU_GROUP_DIM
SGU_CHUNK = 128
GDN_HEADS = 8
GDN_DK = 128
GDN_DV = 128
GDN_QK = GDN_HEADS * GDN_DK
GDN_V = GDN_HEADS * GDN_DV
GDN_CONV_DIM = 2 * GDN_QK + GDN_V
FFN_DIM = 2816

LANES = 128
SUBLANES_F32 = 8
CHUNK = 64
SMALL_W = LANES

ROW_TILE = 1024
COL_TILE = 256
HALO = SUBLANES_F32
HALO_BF16 = 2 * SUBLANES_F32
VMEM_LIMIT = 56 * 1024 * 1024
GLA_UNROLL = 2


def _dot(a, b):
    return jnp.dot(a, b, preferred_element_type=F32)


def _dot_nt(a, b):
    return lax.dot_general(a, b, (((1,), (1,)), ((), ())), preferred_element_type=F32)


def _dot_tn(a, b):
    return lax.dot_general(a, b, (((0,), (0,)), ((), ())), preferred_element_type=F32)


def _split(a):
    hi = a.astype(BF16)
    lo = (a - hi.astype(F32)).astype(BF16)
    return hi, lo


def _dot_exact_lhs(l_bf16, a):
    hi, lo = _split(a)
    return _dot(l_bf16, hi) + _dot(l_bf16, lo)


def _rms(x, gain):
    ms = jnp.mean(x * x, axis=-1, keepdims=True)
    return x * lax.rsqrt(ms + NORM_EPS) * gain


def _sigmoid(x):
    return 1.0 / (1.0 + jnp.exp(-x))


def _silu(x):
    return x * _sigmoid(x)


def _softplus(x):
    return jnp.maximum(x, 0.0) + jnp.log(1.0 + jnp.exp(-jnp.abs(x)))


def _gelu_tanh(x):
    c = 0.7978845608028654
    return 0.5 * x * (1.0 + jnp.tanh(c * (x + 0.044715 * (x * x * x))))


def _iota(shape, dim):
    return lax.broadcasted_iota(jnp.int32, shape, dim)


def _shift_rows(g, first_row, last_row):
    n = g.shape[0]
    row = _iota(g.shape, 0)
    g_prev = jnp.where(row == 0, first_row, pltpu.roll(g, 1, axis=0))
    g_next = jnp.where(row == n - 1, last_row, pltpu.roll(g, n - 1, axis=0))
    return g_prev, g_next


def _halo_rows(gh, tiles_per_seq):
    i = pl.program_id(0)
    pos = i % tiles_per_seq
    keep_prev = (pos != 0).astype(F32)
    keep_next = (pos != tiles_per_seq - 1).astype(F32)
    prev_row = gh[HALO - 1:HALO, :] * keep_prev
    next_row = gh[HALO:HALO + 1, :] * keep_next
    return prev_row, next_row


def _normed_with_halo(x_ref, xp_ref, xn_ref, g_ref):
    g = g_ref[...]
    hn = _rms(x_ref[...], g).astype(BF16)
    halo = jnp.concatenate([xp_ref[...], xn_ref[...]], axis=0)
    hh = _rms(halo, g).astype(BF16)
    return hn, jnp.concatenate([hn, hh], axis=0)


def _norm_proj_kernel(*refs, n_main, conv_cols, tiles_per_seq):
    if conv_cols:
        x_ref, xp_ref, xn_ref, g_ref, w_ref, ws_ref, cw_ref, o_ref, s_ref = refs
        hn, hx = _normed_with_halo(x_ref, xp_ref, xn_ref, g_ref)
    else:
        x_ref, g_ref, w_ref, ws_ref, o_ref, s_ref = refs
        hn = _rms(x_ref[...], g_ref[...]).astype(BF16)
    tm = hn.shape[0]
    for c in range(0, n_main, COL_TILE):
        cs = slice(c, c + COL_TILE)
        if c < conv_cols:
            acc = _dot(hx, w_ref[:, cs])
            gp = acc[:tm]
            prev_row, next_row = _halo_rows(acc[tm:], tiles_per_seq)
            g_prev, g_next = _shift_rows(gp, prev_row, next_row)
            cw = cw_ref[:, cs]
            y = cw[0:1] * g_prev + cw[1:2] * gp + cw[2:3] * g_next
            o_ref[:, cs] = _silu(y).astype(o_ref.dtype)
        else:
            o_ref[:, cs] = _dot(hn, w_ref[:, cs]).astype(o_ref.dtype)
    s_ref[...] = _dot(hn, ws_ref[...])


def _halo_specs(d, n_rows, halo, col_blk=0):
    blocks_per_tile = ROW_TILE // halo
    last = n_rows // halo - 1
    prev = pl.BlockSpec((halo, d), lambda i: (jnp.maximum(i * blocks_per_tile - 1, 0), col_blk))
    nxt = pl.BlockSpec((halo, d), lambda i: (jnp.minimum((i + 1) * blocks_per_tile, last), col_blk))
    return prev, nxt


def _resident(shape):
    return pl.BlockSpec(shape, lambda i: (0,) * len(shape), pipeline_mode=pl.Buffered(1))


def _norm_proj(x2d, gain, w, w_small, conv_w, seq_len, n_main, conv_cols, name):
    t, d = x2d.shape
    row = pl.BlockSpec((ROW_TILE, d), lambda i: (i, 0))
    in_specs = [row]
    args = [x2d]
    if conv_cols:
        prev, nxt = _halo_specs(d, t, HALO)
        in_specs += [prev, nxt]
        args += [x2d, x2d]
    in_specs += [_resident((1, d)), _resident(w.shape), _resident(w_small.shape)]
    args += [gain, w, w_small]
    if conv_cols:
        in_specs.append(_resident(conv_w.shape))
        args.append(conv_w)
    kern = functools.partial(_norm_proj_kernel, n_main=n_main, conv_cols=conv_cols,
                             tiles_per_seq=seq_len // ROW_TILE)
    return pl.pallas_call(
        kern,
        grid=(t // ROW_TILE,),
        in_specs=in_specs,
        out_specs=[pl.BlockSpec((ROW_TILE, n_main), lambda i: (i, 0)),
                   pl.BlockSpec((ROW_TILE, SMALL_W), lambda i: (i, 0))],
        out_shape=[jax.ShapeDtypeStruct((t, n_main), BF16),
                   jax.ShapeDtypeStruct((t, SMALL_W), F32)],
        compiler_params=pltpu.CompilerParams(dimension_semantics=("arbitrary",),
                                             vmem_limit_bytes=VMEM_LIMIT),
        name=name,
    )(*args)


def _with_halo_bf16(m_ref, mp_ref, mn_ref):
    return jnp.concatenate([m_ref[...].astype(F32), mp_ref[...].astype(F32)[HALO_BF16 - HALO:],
                            mn_ref[...].astype(F32)[:HALO]], axis=0)


def _ffn_kernel(*refs, tiles_per_seq, final_norm, gdn_out):
    refs = list(refs)
    x_ref, xp_ref, xn_ref = refs[:3]
    del refs[:3]
    x_ext = jnp.concatenate([x_ref[...], xp_ref[...], xn_ref[...]], axis=0)
    tm = x_ref.shape[0]
    if gdn_out:
        m_ref, mp_ref, mn_ref, z_ref, zp_ref, zn_ref, gn_ref, wo_ref = refs[:8]
        del refs[:8]
        m_ext = _with_halo_bf16(m_ref, mp_ref, mn_ref)
        z_ext = _with_halo_bf16(z_ref, zp_ref, zn_ref)
        parts = []
        for h in range(GDN_HEADS):
            hs = slice(h * GDN_DV, (h + 1) * GDN_DV)
            parts.append((_rms(m_ext[:, hs], gn_ref[...]) * _silu(z_ext[:, hs])).astype(BF16))
        x_ext = x_ext + _dot(jnp.concatenate(parts, axis=1), wo_ref[...])
    g_ref, wup_ref, cw_ref, cb_ref, wdn_ref = refs[:5]
    del refs[:5]
    if final_norm:
        gf_ref = refs.pop(0)
    o_ref, act_ref = refs
    hx = _rms(x_ext, g_ref[...]).astype(BF16)
    hn = hx[:tm]
    for c in range(0, FFN_DIM, COL_TILE):
        cs = slice(c, c + COL_TILE)
        acc = _dot(hx, wup_ref[:, cs])
        up = _dot(hn, wup_ref[:, FFN_DIM + c:FFN_DIM + c + COL_TILE])
        gp = acc[:tm]
        prev_row, next_row = _halo_rows(acc[tm:], tiles_per_seq)
        g_prev, g_next = _shift_rows(gp, prev_row, next_row)
        cw = cw_ref[:, cs]
        gate = cw[0:1] * g_prev + cw[1:2] * gp + cw[2:3] * g_next + cb_ref[:, cs]
        act_ref[:, cs] = (_silu(gate) * up).astype(BF16)
    out = x_ext[:tm] + _dot(act_ref[...], wdn_ref[...])
    if final_norm:
        out = _rms(out, gf_ref[...])
    o_ref[...] = out


def _layer_resident(stacked, layer):
    shape = stacked.shape[1:]
    return pl.BlockSpec((None,) + shape, lambda i: (layer,) + (0,) * len(shape),
                        pipeline_mode=pl.Buffered(1))


def _ffn(h2d, gain, w_up, conv_w, conv_b, w_down, layer, seq_len, final_gain, name, gdn_out=None):
    t, d = h2d.shape
    row = pl.BlockSpec((ROW_TILE, d), lambda i: (i, 0))
    prev, nxt = _halo_specs(d, t, HALO)
    in_specs = [row, prev, nxt]
    args = [h2d, h2d, h2d]
    if gdn_out is not None:
        o_gdn, p_main, gnorm, w_out = gdn_out
        z_blk = GDN_CONV_DIM // GDN_V
        for arr, cb in ((o_gdn, 0), (p_main, z_blk)):
            prev_b, nxt_b = _halo_specs(GDN_V, t, HALO_BF16, cb)
            in_specs += [pl.BlockSpec((ROW_TILE, GDN_V), lambda i, cb=cb: (i, cb)), prev_b, nxt_b]
            args += [arr, arr, arr]
        in_specs += [_resident(gnorm.shape), _resident(w_out.shape)]
        args += [gnorm, w_out]
    in_specs += [_resident((1, d)), _layer_resident(w_up, layer), _resident(conv_w.shape),
                 _resident(conv_b.shape), _layer_resident(w_down, layer)]
    args += [gain, w_up, conv_w, conv_b, w_down]
    if final_gain is not None:
        in_specs.append(_resident((1, d)))
        args.append(final_gain)
    kern = functools.partial(_ffn_kernel, tiles_per_seq=seq_len // ROW_TILE,
                             final_norm=final_gain is not None, gdn_out=gdn_out is not None)
    return pl.pallas_call(
        kern,
        grid=(t // ROW_TILE,),
        in_specs=in_specs,
        out_specs=row,
        out_shape=jax.ShapeDtypeStruct((t, d), F32),
        scratch_shapes=[pltpu.VMEM((ROW_TILE, FFN_DIM), BF16)],
        compiler_params=pltpu.CompilerParams(dimension_semantics=("arbitrary",),
                                             vmem_limit_bytes=VMEM_LIMIT),
        name=name,
    )(*args)


def _pair_rows(x):
    lo = _iota(x.shape, 1) < CHUNK
    return jnp.concatenate([jnp.where(lo, x, 0.0), jnp.where(lo, 0.0, x)], axis=0)


def _pair_blockdiag(x):
    lo = _iota(x.shape, 1) < LANES
    return jnp.concatenate([jnp.where(lo, x, 0.0), jnp.where(lo, 0.0, x)], axis=0)


def _block_tri_ones(lower, n=2 * CHUNK):
    r = _iota((n, n), 0)
    c = _iota((n, n), 1)
    tri = jnp.where((c <= r) if lower else (c >= r), 1.0, 0.0)
    return jnp.where((r // CHUNK) == (c // CHUNK), tri, 0.0).astype(BF16)


def _packed_tri_mask(lower, strict=False):
    r = _iota((CHUNK, LANES), 0)
    c = _iota((CHUNK, LANES), 1) & (CHUNK - 1)
    if lower:
        return (c < r) if strict else (c <= r)
    return (c > r) if strict else (c >= r)


def _gla_kernel(q_ref, k_ref, v_ref, gate_ref, lr_ref, wg_ref, bg_ref, gn_ref, o_ref,
                la_ref, of_ref, ob_ref, st_ref, *, seq_len):
    n_pairs = GLA_HEADS // 2
    n_dbl = seq_len // (2 * CHUNK)
    blk = 2 * CHUNK

    def prep(rb, carry):
        r0 = pl.multiple_of(rb * blk, blk)
        rows = pl.ds(r0, blk)
        z = _dot(lr_ref[rows, :].astype(BF16), wg_ref[...]) + bg_ref[...]
        log_sig = jnp.minimum(z, 0.0) - jnp.log(1.0 + jnp.exp(-jnp.abs(z)))
        la_ref[rows, :] = log_sig * (1.0 / GLA_GATE_NORMALIZER)
        return carry

    lax.fori_loop(0, n_dbl, prep, 0)
    st_ref[...] = jnp.zeros(st_ref.shape, F32)

    tri = (_block_tri_ones(True), _block_tri_ones(False))
    masks = (_packed_tri_mask(True), _packed_tri_mask(False))
    st_mask = (_iota((LANES, 2 * GLA_DV), 0) // GLA_DK) == (_iota((LANES, 2 * GLA_DV), 1) // GLA_DV)
    first_half = _iota((blk, GLA_QK), 0) < CHUNK
    o_refs = (of_ref, ob_ref)

    def body(it, carry):
        steps = []
        for sub in range(GLA_UNROLL):
            df = GLA_UNROLL * it + sub
            dbs = (df, n_dbl - 1 - df)
            chains = []
            for dirn in range(2):
                rows = pl.ds(pl.multiple_of(dbs[dirn] * blk, blk), blk)
                la = la_ref[rows, dirn * GLA_QK:(dirn + 1) * GLA_QK]
                q2, k2, v2 = q_ref[rows, :], k_ref[rows, :], v_ref[rows, :]
                cum = _dot_exact_lhs(tri[dirn], la)
                if dirn == 0:
                    tots = (cum[CHUNK - 1:CHUNK, :], cum[blk - 1:blk, :])
                else:
                    tots = (cum[0:1, :], cum[CHUNK:CHUNK + 1, :])
                tot_rows = jnp.where(first_half, tots[0], tots[1])
                q = q2.astype(F32) * (GLA_DK ** -0.5)
                k = k2.astype(F32)
                q_dec = (q * jnp.exp(cum)).astype(BF16)
                k_inv = k * jnp.exp(-cum)
                k_end = k * jnp.exp(tot_rows - cum)
                v32 = v2.astype(F32)
                for p in range(n_pairs):
                    ls = slice(p * LANES, (p + 1) * LANES)
                    vs = slice(p * 2 * GLA_DV, (p + 1) * 2 * GLA_DV)
                    dec = [jnp.broadcast_to(jnp.exp(tots[cc][:, ls]), (LANES, LANES)).T
                           for cc in range(2)]
                    chains.append(dict(dirn=dirn, p=p, rows=rows,
                                       order=(0, 1) if dirn == 0 else (1, 0),
                                       qd=q_dec[:, ls], k_inv=k_inv[:, ls],
                                       k_end=k_end[:, ls].astype(BF16),
                                       v=v32[:, vs], vb=v2[:, vs],
                                       dec=[jnp.concatenate([d, d], axis=1) for d in dec],
                                       o=[None, None]))
            steps.append(chains)
        states = {(dirn, p): st_ref[dirn, p] for dirn in range(2) for p in range(n_pairs)}
        for chains in steps:
            for ch in chains:
                for cc in range(2):
                    rs = slice(cc * CHUNK, (cc + 1) * CHUNK)
                    kbd = _pair_rows(ch["k_inv"][rs]).astype(BF16)
                    sc = jnp.where(masks[ch["dirn"]], _dot_nt(ch["qd"][rs], kbd), 0.0).astype(BF16)
                    ch["o"][cc] = _dot(sc, _pair_blockdiag(ch["v"][rs]).astype(BF16))
        for chains in steps:
            for step in range(2):
                for ch in chains:
                    cc = ch["order"][step]
                    rs = slice(cc * CHUNK, (cc + 1) * CHUNK)
                    state = states[(ch["dirn"], ch["p"])]
                    ch["o"][cc] = ch["o"][cc] + _dot(ch["qd"][rs], state.astype(BF16))
                    upd = _dot_tn(ch["k_end"][rs], ch["vb"][rs])
                    states[(ch["dirn"], ch["p"])] = state * ch["dec"][cc] + jnp.where(st_mask, upd, 0.0)
        for chains in steps:
            for dirn in range(2):
                mine = [ch for ch in chains if ch["dirn"] == dirn]
                o_refs[dirn][mine[0]["rows"], :] = jnp.concatenate(
                    [jnp.concatenate(ch["o"], axis=0) for ch in mine], axis=1)
        for (dirn, p), state in states.items():
            st_ref[dirn, p] = state
        return carry

    lax.fori_loop(0, n_dbl // GLA_UNROLL, body, 0)

    def finish(rb, carry):
        r0 = pl.multiple_of(rb * blk, blk)
        rows = pl.ds(r0, blk)
        for h in range(GLA_HEADS):
            hs = slice(h * GLA_DV, (h + 1) * GLA_DV)
            o = of_ref[rows, hs] + ob_ref[rows, hs]
            y = _rms(o, gn_ref[...]) * _silu(gate_ref[rows, hs].astype(F32))
            o_ref[rows, hs] = y.astype(o_ref.dtype)
        return carry

    lax.fori_loop(0, n_dbl, finish, 0)


def _gla(p_main, p_small, wg, bg, gnorm, batch, seq_len):
    t = batch * seq_len
    kern = functools.partial(_gla_kernel, seq_len=seq_len)
    qk_blk = GLA_QK
    return pl.pallas_call(
        kern,
        grid=(batch,),
        in_specs=[
            pl.BlockSpec((seq_len, GLA_QK), lambda b: (b, 0)),
            pl.BlockSpec((seq_len, GLA_QK), lambda b: (b, 1)),
            pl.BlockSpec((seq_len, GLA_V), lambda b: (b, (2 * qk_blk) // GLA_V)),
            pl.BlockSpec((seq_len, GLA_V), lambda b: (b, (2 * qk_blk) // GLA_V + 1)),
            pl.BlockSpec((seq_len, SMALL_W), lambda b: (b, 0)),
            _resident(wg.shape), _resident(bg.shape), _resident(gnorm.shape),
        ],
        out_specs=pl.BlockSpec((seq_len, GLA_V), lambda b: (b, 0)),
        out_shape=jax.ShapeDtypeStruct((t, GLA_V), BF16),
        scratch_shapes=[
            pltpu.VMEM((seq_len, 2 * GLA_QK), F32),
            pltpu.VMEM((seq_len, GLA_V), F32),
            pltpu.VMEM((seq_len, GLA_V), F32),
            pltpu.VMEM((2, GLA_HEADS // 2, LANES, 2 * GLA_DV), F32),
        ],
        compiler_params=pltpu.CompilerParams(dimension_semantics=("arbitrary",),
                                             vmem_limit_bytes=VMEM_LIMIT),
        name="gla_mixer",
    )(p_main, p_main, p_main, p_main, p_small, wg, bg, gnorm)


def _sgu_out_kernel(h_ref, oa_ref, su_ref, sv_ref, lng_ref, lnb_ref, ws_ref, bs_ref, wo_ref, o_ref):
    tm = h_ref.shape[0]
    u = _gelu_tanh(su_ref[...].astype(F32))
    g = _gelu_tanh(sv_ref[...].astype(F32))
    mu = jnp.mean(g, axis=-1, keepdims=True)
    gc = g - mu
    var = jnp.mean(gc * gc, axis=-1, keepdims=True)
    vv = (gc * lax.rsqrt(var + NORM_EPS) * lng_ref[...] + lnb_ref[...]).astype(BF16)
    rows = []
    for c in range(tm // SGU_CHUNK):
        rs = slice(c * SGU_CHUNK, (c + 1) * SGU_CHUNK)
        cols = []
        for gi in range(SGU_GROUPS):
            gs = slice(gi * SGU_GROUP_DIM, (gi + 1) * SGU_GROUP_DIM)
            cols.append(_dot(ws_ref[gi], vv[rs, gs]))
        rows.append(jnp.concatenate(cols, axis=1) + bs_ref[...])
    mixed = jnp.concatenate(rows, axis=0)
    ob = (u * mixed).astype(BF16)
    acc = _dot(oa_ref[...], wo_ref[:GLA_V, :]) + _dot(ob, wo_ref[GLA_V:, :])
    o_ref[...] = h_ref[...] + acc


def _sgu_out(h2d, o_a, p_main, ln_g, ln_b, w_s, b_full, w_out):
    t, d = h2d.shape
    su_blk = (2 * GLA_QK + 2 * GLA_V) // SGU_DIM
    return pl.pallas_call(
        _sgu_out_kernel,
        grid=(t // ROW_TILE,),
        in_specs=[
            pl.BlockSpec((ROW_TILE, d), lambda i: (i, 0)),
            pl.BlockSpec((ROW_TILE, GLA_V), lambda i: (i, 0)),
            pl.BlockSpec((ROW_TILE, SGU_DIM), lambda i: (i, su_blk)),
            pl.BlockSpec((ROW_TILE, SGU_DIM), lambda i: (i, su_blk + 1)),
            _resident(ln_g.shape), _resident(ln_b.shape), _resident(w_s.shape),
            _resident(b_full.shape), _resident(w_out.shape),
        ],
        out_specs=pl.BlockSpec((ROW_TILE, d), lambda i: (i, 0)),
        out_shape=jax.ShapeDtypeStruct((t, d), F32),
        compiler_params=pltpu.CompilerParams(dimension_semantics=("arbitrary",),
                                             vmem_limit_bytes=VMEM_LIMIT),
        name="sgu_out_proj",
    )(h2d, o_a, p_main, p_main, ln_g, ln_b, w_s, b_full, w_out)


GDN_SCALARS = 6
GDN_PAIRS_PER_STEP = 2
GDN_STEP_W = GDN_PAIRS_PER_STEP * 2 * GDN_DK
GDN_NORM_COL = 4 * GDN_HEADS
GDN_STEP_HEADS = 2 * GDN_PAIRS_PER_STEP
GDN_SCAN_UNROLL = 4
GDN_PRE_CHUNKS = 4


def _packed_product(x, y):
    lo = _iota(y.shape, 1) < CHUNK
    zero = jnp.zeros_like(y)
    return _dot(x, jnp.concatenate([jnp.where(lo, y, zero), jnp.where(lo, zero, y)], axis=0))


def _gdn_kernel(q_ref, k_ref, v_ref, sm_ref, tab_ref, esum_ref, eexp_ref, o_ref,
                tt_ref, aqd_ref, dec_ref, of_ref, ob_ref, st_ref, *, seq_len):
    n_chunks = seq_len // CHUNK
    blk = 2 * CHUNK
    npp = GDN_PAIRS_PER_STEP
    pw = 2 * GDN_DK

    pblk = GDN_PRE_CHUNKS * CHUNK
    tri = (_block_tri_ones(True, pblk), _block_tri_ones(False, pblk))
    incl = (_packed_tri_mask(True), _packed_tri_mask(False))
    strict = (_packed_tri_mask(True, strict=True), _packed_tri_mask(False, strict=True))
    diag = _iota((CHUNK, LANES), 0) == (_iota((CHUNK, LANES), 1) & (CHUNK - 1))
    lo_half = _iota((SUBLANES_F32, LANES), 1) < CHUNK
    eye = jnp.where(diag, 1.0, 0.0)

    def row_form(col_form):
        return jnp.sum(jnp.where(diag, col_form, 0.0), axis=0, keepdims=True)

    def precompute(it, carry):
        rows = pl.ds(pl.multiple_of(it * pblk, pblk), pblk)
        q2 = q_ref[rows, :]
        k2 = k_ref[rows, :]
        sm = sm_ref[rows, :]
        a_exp = jnp.exp(tab_ref[0:1, :])
        dt_b = tab_ref[1:2, :]

        lane = _iota(sm.shape, 1)
        gates = jnp.where(lane < 2 * GDN_HEADS, _sigmoid(sm), -a_exp * _softplus(sm + dt_b))
        qf = q2.astype(F32)
        kf = k2.astype(F32)
        ssq = _dot((qf * qf).astype(BF16), esum_ref[0]) + _dot((kf * kf).astype(BF16), esum_ref[1])
        inv = lax.rsqrt(ssq + NORM_EPS)
        is_qn = jnp.abs(2 * lane - (2 * GDN_NORM_COL + GDN_STEP_HEADS - 1)) < GDN_STEP_HEADS
        inv = jnp.where(is_qn, inv * (GDN_DK ** -0.5), inv)
        table = jnp.where(lane < GDN_NORM_COL, gates, inv)
        cum_f = _dot_exact_lhs(tri[0], table)
        cum_b = _dot_exact_lhs(tri[1], table)
        band = lane // GDN_HEADS
        table = jnp.where(band == 2, cum_f, jnp.where(band == 3, cum_b, table))
        hi, lo = _split(table)
        n_cum = 2 * npp * LANES
        e_cum = eexp_ref[0, :, :n_cum]
        ex_cum = _dot(hi, e_cum) + _dot(lo, e_cum)
        ex_rest = _dot(hi, eexp_ref[0, :, n_cum:])

        chains = []
        for cc in range(GDN_PRE_CHUNKS):
            rs = slice(cc * CHUNK, (cc + 1) * CHUNK)
            for pp in range(npp):
                ps = slice(pp * pw, (pp + 1) * pw)
                cols = [ex_cum[rs, (2 * pp + j) * LANES:(2 * pp + j + 1) * LANES] for j in range(2)]
                cols += [ex_rest[rs, (4 * pp + j) * LANES:(4 * pp + j + 1) * LANES] for j in range(4)]
                rq_c = cols[4]
                rk_c = cols[5]
                rk_r = row_form(rk_c)
                kbd = _pair_blockdiag(kf[rs, ps]).astype(BF16)
                gram = _dot_nt(jnp.concatenate([k2[rs, ps], q2[rs, ps]], axis=0), kbd)
                kk = gram[:CHUNK] * rk_c * rk_r
                qk = gram[CHUNK:] * rq_c * rk_r
                for dirn in range(2):
                    cum_c = cols[dirn]
                    beta_c = cols[2 + dirn]
                    tot_r = cum_c[CHUNK - 1:CHUNK, :] if dirn == 0 else cum_c[0:1, :]
                    cum_r = row_form(cum_c)
                    beta_r = row_form(beta_c)
                    decay = jnp.exp(jnp.where(incl[dirn], cum_c - cum_r, -1e30))
                    a = jnp.where(strict[dirn], kk * beta_c * decay, 0.0)
                    ci = GDN_PRE_CHUNKS * it + cc
                    d_q = jnp.where(diag, rq_c * jnp.exp(cum_c), 0.0)
                    aqd_ref[ci, dirn, pp] = jnp.concatenate([qk * decay, d_q], axis=1).astype(BF16)
                    tot8 = jnp.broadcast_to(tot_r, (SUBLANES_F32, LANES))
                    tot8r = pltpu.roll(tot8, CHUNK, axis=1)
                    dec_ref[ci, dirn, pp] = jnp.exp(jnp.concatenate(
                        [jnp.where(lo_half, tot8, tot8r), jnp.where(lo_half, tot8r, tot8)], axis=1))
                    chains.append(dict(ci=ci, pp=pp, dirn=dirn, pw_a=a, inv_m=eye - a,
                                       scale_u=beta_r, scale_w=beta_r * jnp.exp(cum_r) * rk_r,
                                       e_c=rk_c * jnp.exp(tot_r - cum_c)))
        for ch in chains:
            a_bf = ch["pw_a"].astype(BF16)
            ch["pw_a"] = _packed_product(a_bf, a_bf).astype(BF16)
        for _ in range(4):
            for ch in chains:
                lhs = jnp.concatenate([ch["inv_m"].astype(BF16), ch["pw_a"]], axis=0)
                both = _packed_product(lhs, ch["pw_a"])
                ch["inv_m"] = ch["inv_m"] + both[:CHUNK]
                ch["pw_a"] = both[CHUNK:].astype(BF16)
        for ch in chains:
            ch["inv_m"] = ch["inv_m"] + _packed_product(ch["inv_m"].astype(BF16), ch["pw_a"])
        for ch in chains:
            t_u = ch["inv_m"] * ch["scale_u"]
            t_w = ch["inv_m"] * ch["scale_w"]
            e_c = ch["e_c"]
            tt_ref[ch["ci"], ch["dirn"], ch["pp"]] = jnp.concatenate(
                [t_u, t_u * e_c, t_w, t_w * e_c], axis=0).astype(BF16)
        return carry

    lax.fori_loop(0, seq_len // pblk, precompute, 0)
    st_ref[...] = jnp.zeros(st_ref.shape, F32)

    o_refs = (of_ref, ob_ref)

    def scan(it, carry):
        heads = (slice(0, GDN_DV), slice(GDN_DV, 2 * GDN_DV))
        steps = []
        for sub in range(GDN_SCAN_UNROLL):
            cf = GDN_SCAN_UNROLL * it + sub
            cis = (cf, n_chunks - 1 - cf)
            chains = []
            for dirn in range(2):
                ci = cis[dirn]
                rows = pl.ds(pl.multiple_of(ci * CHUNK, CHUNK), CHUNK)
                q_c, k_c, v_c = q_ref[rows, :], k_ref[rows, :], v_ref[rows, :]
                for pp in range(npp):
                    ps = slice(pp * pw, (pp + 1) * pw)
                    chains.append(dict(dirn=dirn, pp=pp, rows=rows, tt=tt_ref[ci, dirn, pp],
                                       aqd=aqd_ref[ci, dirn, pp], dec=dec_ref[ci, dirn, pp],
                                       q=q_c[:, ps], k=k_c[:, ps], v=v_c[:, ps]))
            steps.append(chains)
        states = {(dirn, pp): [st_ref[dirn, pp, j] for j in range(2)]
                  for dirn in range(2) for pp in range(npp)}
        for chains in steps:
            for ch in chains:
                kbd = _pair_blockdiag(ch["k"].astype(F32)).astype(BF16)
                vbd = _pair_blockdiag(ch["v"].astype(F32)).astype(BF16)
                ch["uu"] = _dot(ch["tt"][:blk], vbd)
                ch["ww"] = _dot(ch["tt"][blk:], kbd)
        for chains in steps:
            for ch in chains:
                state = states[(ch["dirn"], ch["pp"])]
                lhs = jnp.concatenate([ch["ww"].astype(BF16), ch["q"]], axis=0)
                ch["prod"] = jnp.concatenate(
                    [_dot(lhs[:, hs], state[j].astype(BF16)) for j, hs in enumerate(heads)], axis=1)
            for ch in chains:
                state = states[(ch["dirn"], ch["pp"])]
                v_new_e = (ch["uu"][CHUNK:] - ch["prod"][CHUNK:blk]).astype(BF16)
                states[(ch["dirn"], ch["pp"])] = [
                    state[j] * ch["dec"][0:1, hs] + _dot_tn(ch["k"][:, hs], v_new_e[:, hs])
                    for j, hs in enumerate(heads)]
            for ch in chains:
                v_new = ch["uu"][:CHUNK] - ch["prod"][:CHUNK]
                rhs = jnp.concatenate([_pair_blockdiag(v_new), _pair_blockdiag(ch["prod"][blk:])], axis=0)
                ch["o"] = _dot(ch["aqd"], rhs.astype(BF16))
        for chains in steps:
            for dirn in range(2):
                mine = [ch for ch in chains if ch["dirn"] == dirn]
                o_refs[dirn][mine[0]["rows"], :] = jnp.concatenate([ch["o"] for ch in mine], axis=1)
        for (dirn, pp), state in states.items():
            for j in range(2):
                st_ref[dirn, pp, j] = state[j]
        return carry

    lax.fori_loop(0, n_chunks // GDN_SCAN_UNROLL, scan, 0)

    def finish(rb, carry):
        r0 = pl.multiple_of(rb * blk, blk)
        rows = pl.ds(r0, blk)
        o_ref[rows, :] = (of_ref[rows, :] + ob_ref[rows, :]).astype(o_ref.dtype)
        return carry

    lax.fori_loop(0, seq_len // blk, finish, 0)


def _gdn(p_main, p_small, tab, esum, eexp, batch, seq_len):
    t = batch * seq_len
    n_steps = GDN_QK // GDN_STEP_W
    n_chunks = seq_len // CHUNK
    npp = GDN_PAIRS_PER_STEP
    pw = 2 * GDN_DK
    sw = GDN_STEP_W
    kern = functools.partial(_gdn_kernel, seq_len=seq_len)
    return pl.pallas_call(
        kern,
        grid=(batch, n_steps),
        in_specs=[
            pl.BlockSpec((seq_len, sw), lambda b, p: (b, p)),
            pl.BlockSpec((seq_len, sw), lambda b, p: (b, GDN_QK // sw + p)),
            pl.BlockSpec((seq_len, sw), lambda b, p: (b, 2 * GDN_QK // sw + p)),
            pl.BlockSpec((seq_len, SMALL_W), lambda b, p: (b, 0)),
            pl.BlockSpec(tab.shape, lambda b, p: (0, 0)),
            pl.BlockSpec(esum.shape, lambda b, p: (0, 0, 0)),
            pl.BlockSpec((1, SMALL_W, npp * GDN_SCALARS * LANES), lambda b, p: (p, 0, 0)),
        ],
        out_specs=pl.BlockSpec((seq_len, sw), lambda b, p: (b, p)),
        out_shape=jax.ShapeDtypeStruct((t, GDN_V), BF16),
        scratch_shapes=[
            pltpu.VMEM((n_chunks, 2, npp, 4 * CHUNK, LANES), BF16),
            pltpu.VMEM((n_chunks, 2, npp, CHUNK, 2 * LANES), BF16),
            pltpu.VMEM((n_chunks, 2, npp, SUBLANES_F32, pw), F32),
            pltpu.VMEM((seq_len, sw), F32),
            pltpu.VMEM((seq_len, sw), F32),
            pltpu.VMEM((2, npp, 2, GDN_DK, GDN_DV), F32),
        ],
        compiler_params=pltpu.CompilerParams(dimension_semantics=("arbitrary", "arbitrary"),
                                             vmem_limit_bytes=VMEM_LIMIT),
        name="gdn_mixer",
    )(p_main, p_main, p_main, p_small, tab, esum, eexp)


def _pad_cols(w, width):
    return jnp.pad(w, ((0, 0), (0, width - w.shape[1])))


def _gla_params(ab_w_in, w_gate_fwd, b_gate_fwd, w_gate_bwd, b_gate_bwd):
    n_wide = 2 * GLA_QK + 2 * GLA_V
    lr0 = n_wide
    sg0 = lr0 + 2 * GLA_LOWRANK
    wide = ab_w_in[:, :n_wide]
    sgu = ab_w_in[:, sg0:sg0 + 2 * SGU_DIM]
    small = _pad_cols(ab_w_in[:, lr0:sg0], SMALL_W).astype(BF16)
    w = jnp.concatenate([wide, sgu], axis=1).astype(BF16)
    wg = jnp.zeros((SMALL_W, 2 * GLA_QK), F32)
    wg = wg.at[:GLA_LOWRANK, :GLA_QK].set(w_gate_fwd)
    wg = wg.at[GLA_LOWRANK:2 * GLA_LOWRANK, GLA_QK:].set(w_gate_bwd)
    bg = jnp.concatenate([b_gate_fwd, b_gate_bwd])[None, :]
    return w, small, wg.astype(BF16), bg


def _gdn_params(gdn_w_in, a_log_fwd, dt_bias_fwd, a_log_bwd, dt_bias_bwd):
    n_main = GDN_CONV_DIM + GDN_V
    small = _pad_cols(gdn_w_in[:, n_main:], SMALL_W).astype(BF16)
    w = gdn_w_in.astype(BF16)
    pad = SMALL_W - 4 * GDN_HEADS
    zeros2 = jnp.zeros((2 * GDN_HEADS,), F32)
    a_log = jnp.concatenate([zeros2, a_log_fwd, a_log_bwd, jnp.zeros((pad,), F32)])
    dt_b = jnp.concatenate([zeros2, dt_bias_fwd, dt_bias_bwd, jnp.zeros((pad,), F32)])
    tab = jnp.zeros((SUBLANES_F32, SMALL_W), F32).at[0].set(a_log).at[1].set(dt_b)
    ch_head = jnp.arange(GDN_STEP_W) // GDN_DK
    col = jnp.arange(SMALL_W)
    esum_q = (col[None, :] == (GDN_NORM_COL + ch_head)[:, None])
    esum_k = (col[None, :] == (GDN_NORM_COL + GDN_STEP_HEADS + ch_head)[:, None])
    esum = jnp.stack([esum_q, esum_k]).astype(BF16)
    n_steps = GDN_QK // GDN_STEP_W
    lane = jnp.arange(GDN_PAIRS_PER_STEP * GDN_SCALARS * LANES)
    blk = lane // LANES
    n_cum_blk = 2 * GDN_PAIRS_PER_STEP
    pair = jnp.where(blk < n_cum_blk, blk // 2, (blk - n_cum_blk) // 4)
    quant = jnp.where(blk < n_cum_blk, blk % 2, 2 + (blk - n_cum_blk) % 4)
    local_head = 2 * pair + (lane % LANES) // CHUNK
    head = GDN_STEP_HEADS * jnp.arange(n_steps)[:, None] + local_head[None, :]
    src_gate = jnp.array([2 * GDN_HEADS, 3 * GDN_HEADS, 0, GDN_HEADS])
    src_norm = GDN_NORM_COL + GDN_STEP_HEADS * (quant - 4) + local_head
    src = jnp.where(quant[None, :] < 4, src_gate[jnp.minimum(quant, 3)][None, :] + head,
                    src_norm[None, :])
    eexp = (col[None, :, None] == src[:, None, :]).astype(BF16)
    return w, small, tab, esum, eexp


def kernel(x, norm_mix, norm_ffn, norm_final, ab_w_in, gla_w_gate_fwd, gla_b_gate_fwd, gla_w_gate_bwd, gla_b_gate_bwd, gla_norm, sgu_ln_g, sgu_ln_b, sgu_w_s, sgu_b_s, ab_w_out, gdn_w_in, gdn_conv_w, gdn_a_log_fwd, gdn_dt_bias_fwd, gdn_a_log_bwd, gdn_dt_bias_bwd, gdn_norm, gdn_w_out, ffn_w_up, ffn_conv_w, ffn_conv_b, ffn_w_down):
    batch, seq_len, d = x.shape
    t = batch * seq_len
    assert seq_len % ROW_TILE == 0 and seq_len % (2 * CHUNK) == 0
    h = x.reshape(t, d)

    w0, w0_small, wg, bg = _gla_params(ab_w_in[0], gla_w_gate_fwd[0], gla_b_gate_fwd[0],
                                       gla_w_gate_bwd[0], gla_b_gate_bwd[0])
    n_main0 = 2 * GLA_QK + 2 * GLA_V + 2 * SGU_DIM
    p0, s0 = _norm_proj(h, norm_mix[0][None, :], w0, w0_small, None, seq_len, n_main0, 0,
                        "gla_sgu_in_proj")
    o_a = _gla(p0, s0, wg, bg, gla_norm[0][None, :], batch, seq_len)
    b_full = jnp.repeat(sgu_b_s[0].T, SGU_GROUP_DIM, axis=1)
    h = _sgu_out(h, o_a, p0, sgu_ln_g[0][None, :], sgu_ln_b[0][None, :],
                 sgu_w_s[0].astype(BF16), b_full, ab_w_out[0].astype(BF16))
    w_up_all = ffn_w_up.astype(BF16)
    w_down_all = ffn_w_down.astype(BF16)
    h = _ffn(h, norm_ffn[0][None, :], w_up_all, ffn_conv_w[0], ffn_conv_b[0][None, :], w_down_all,
             0, seq_len, None, "ffn0")

    w1, w1_small, tab, esum, eexp = _gdn_params(gdn_w_in[0], gdn_a_log_fwd[0], gdn_dt_bias_fwd[0],
                                                gdn_a_log_bwd[0], gdn_dt_bias_bwd[0])
    n_main1 = GDN_CONV_DIM + GDN_V
    p1, s1 = _norm_proj(h, norm_mix[1][None, :], w1, w1_small, gdn_conv_w[0], seq_len, n_main1,
                        GDN_CONV_DIM, "gdn_in_proj")
    o_g = _gdn(p1, s1, tab, esum, eexp, batch, seq_len)
    h = _ffn(h, norm_ffn[1][None, :], w_up_all, ffn_conv_w[1], ffn_conv_b[1][None, :], w_down_all,
             1, seq_len, norm_final[None, :], "gdn_out_ffn1",
             gdn_out=(o_g, p1, gdn_norm[0][None, :], gdn_w_out[0].astype(BF16)))
    return h.reshape(batch, seq_len, d)
```

```python
import functools

import jax
import jax.numpy as jnp
from jax import lax
from jax.experimental import pallas as pl
from jax.experimental.pallas import tpu as pltpu

F32 = jnp.float32
BF16 = jnp.bfloat16

NORM_EPS = 1e-6
GLA_HEADS = 4
GLA_DK = 64
GLA_DV = 128
GLA_QK = GLA_HEADS * GLA_DK
GLA_V = GLA_HEADS * GLA_DV
GLA_LOWRANK = 16
GLA_GATE_NORMALIZER = 16.0
SGU_GROUPS = 4
SGU_GROUP_DIM = 128
SGU_DIM = SGU_GROUPS * SGU_GROUP_DIM
SGU_CHUNK = 128
GDN_HEADS = 8
GDN_DK = 128
GDN_DV = 128
GDN_QK = GDN_HEADS * GDN_DK
GDN_V = GDN_HEADS * GDN_DV
GDN_CONV_DIM = 2 * GDN_QK + GDN_V
FFN_DIM = 2816

LANES = 128
SUBLANES_F32 = 8
CHUNK = 64
SMALL_W = LANES

ROW_TILE = 1024
COL_TILE = 256
HALO = SUBLANES_F32
HALO_BF16 = 2 * SUBLANES_F32
VMEM_LIMIT = 56 * 1024 * 1024
GLA_UNROLL = 4


def _dot(a, b):
    return jnp.dot(a, b, preferred_element_type=F32)


def _dot_nt(a, b):
    return lax.dot_general(a, b, (((1,), (1,)), ((), ())), preferred_element_type=F32)


def _dot_tn(a, b):
    return lax.dot_general(a, b, (((0,), (0,)), ((), ())), preferred_element_type=F32)


def _split(a):
    hi = a.astype(BF16)
    lo = (a - hi.astype(F32)).astype(BF16)
    return hi, lo


def _dot_exact_lhs(l_bf16, a):
    hi, lo = _split(a)
    return _dot(l_bf16, hi) + _dot(l_bf16, lo)


def _rms(x, gain):
    ms = jnp.mean(x * x, axis=-1, keepdims=True)
    return x * lax.rsqrt(ms + NORM_EPS) * gain


def _sigmoid(x):
    return 1.0 / (1.0 + jnp.exp(-x))


def _silu(x):
    return x * _sigmoid(x)


def _softplus(x):
    return jnp.maximum(x, 0.0) + jnp.log(1.0 + jnp.exp(-jnp.abs(x)))


def _gelu_tanh(x):
    c = 0.7978845608028654
    return 0.5 * x * (1.0 + jnp.tanh(c * (x + 0.044715 * (x * x * x))))


def _iota(shape, dim):
    return lax.broadcasted_iota(jnp.int32, shape, dim)


def _shift_rows(g, first_row, last_row):
    n = g.shape[0]
    row = _iota(g.shape, 0)
    g_prev = jnp.where(row == 0, first_row, pltpu.roll(g, 1, axis=0))
    g_next = jnp.where(row == n - 1, last_row, pltpu.roll(g, n - 1, axis=0))
    return g_prev, g_next


def _halo_rows(gh, tiles_per_seq):
    i = pl.program_id(0)
    pos = i % tiles_per_seq
    keep_prev = (pos != 0).astype(F32)
    keep_next = (pos != tiles_per_seq - 1).astype(F32)
    prev_row = gh[HALO - 1:HALO, :] * keep_prev
    next_row = gh[HALO:HALO + 1, :] * keep_next
    return prev_row, next_row


def _normed_with_halo(x_ref, xp_ref, xn_ref, g_ref):
    g = g_ref[...]
    hn = _rms(x_ref[...], g).astype(BF16)
    halo = jnp.concatenate([xp_ref[...], xn_ref[...]], axis=0)
    hh = _rms(halo, g).astype(BF16)
    return hn, jnp.concatenate([hn, hh], axis=0)


def _norm_proj_kernel(*refs, n_main, conv_cols, tiles_per_seq):
    if conv_cols:
        x_ref, xp_ref, xn_ref, g_ref, w_ref, ws_ref, cw_ref, o_ref, s_ref = refs
        hn, hx = _normed_with_halo(x_ref, xp_ref, xn_ref, g_ref)
    else:
        x_ref, g_ref, w_ref, ws_ref, o_ref, s_ref = refs
        hn = _rms(x_ref[...], g_ref[...]).astype(BF16)
    tm = hn.shape[0]
    for c in range(0, n_main, COL_TILE):
        cs = slice(c, c + COL_TILE)
        if c < conv_cols:
            acc = _dot(hx, w_ref[:, cs])
            gp = acc[:tm]
            prev_row, next_row = _halo_rows(acc[tm:], tiles_per_seq)
            g_prev, g_next = _shift_rows(gp, prev_row, next_row)
            cw = cw_ref[:, cs]
            y = cw[0:1] * g_prev + cw[1:2] * gp + cw[2:3] * g_next
            o_ref[:, cs] = _silu(y).astype(o_ref.dtype)
        else:
            o_ref[:, cs] = _dot(hn, w_ref[:, cs]).astype(o_ref.dtype)
    s_ref[...] = _dot(hn, ws_ref[...])


def _halo_specs(d, n_rows, halo, col_blk=0):
    blocks_per_tile = ROW_TILE // halo
    last = n_rows // halo - 1
    prev = pl.BlockSpec((halo, d), lambda i: (jnp.maximum(i * blocks_per_tile - 1, 0), col_blk))
    nxt = pl.BlockSpec((halo, d), lambda i: (jnp.minimum((i + 1) * blocks_per_tile, last), col_blk))
    return prev, nxt


def _resident(shape):
    return pl.BlockSpec(shape, lambda i: (0,) * len(shape), pipeline_mode=pl.Buffered(1))


def _norm_proj(x2d, gain, w, w_small, conv_w, seq_len, n_main, conv_cols, name):
    t, d = x2d.shape
    row = pl.BlockSpec((ROW_TILE, d), lambda i: (i, 0))
    in_specs = [row]
    args = [x2d]
    if conv_cols:
        prev, nxt = _halo_specs(d, t, HALO)
        in_specs += [prev, nxt]
        args += [x2d, x2d]
    in_specs += [_resident((1, d)), _resident(w.shape), _resident(w_small.shape)]
    args += [gain, w, w_small]
    if conv_cols:
        in_specs.append(_resident(conv_w.shape))
        args.append(conv_w)
    kern = functools.partial(_norm_proj_kernel, n_main=n_main, conv_cols=conv_cols,
                             tiles_per_seq=seq_len // ROW_TILE)
    return pl.pallas_call(
        kern,
        grid=(t // ROW_TILE,),
        in_specs=in_specs,
        out_specs=[pl.BlockSpec((ROW_TILE, n_main), lambda i: (i, 0)),
                   pl.BlockSpec((ROW_TILE, SMALL_W), lambda i: (i, 0))],
        out_shape=[jax.ShapeDtypeStruct((t, n_main), BF16),
                   jax.ShapeDtypeStruct((t, SMALL_W), F32)],
        compiler_params=pltpu.CompilerParams(dimension_semantics=("arbitrary",),
                                             vmem_limit_bytes=VMEM_LIMIT),
        name=name,
    )(*args)


def _with_halo_bf16(m_ref, mp_ref, mn_ref):
    return jnp.concatenate([m_ref[...].astype(F32), mp_ref[...].astype(F32)[HALO_BF16 - HALO:],
                            mn_ref[...].astype(F32)[:HALO]], axis=0)


def _ffn_kernel(*refs, tiles_per_seq, final_norm, gdn_out):
    refs = list(refs)
    x_ref, xp_ref, xn_ref = refs[:3]
    del refs[:3]
    x_ext = jnp.concatenate([x_ref[...], xp_ref[...], xn_ref[...]], axis=0)
    tm = x_ref.shape[0]
    if gdn_out:
        m_ref, mp_ref, mn_ref, z_ref, zp_ref, zn_ref, gn_ref, wo_ref = refs[:8]
        del refs[:8]
        m_ext = _with_halo_bf16(m_ref, mp_ref, mn_ref)
        z_ext = _with_halo_bf16(z_ref, zp_ref, zn_ref)
        parts = []
        for h in range(GDN_HEADS):
            hs = slice(h * GDN_DV, (h + 1) * GDN_DV)
            parts.append((_rms(m_ext[:, hs], gn_ref[...]) * _silu(z_ext[:, hs])).astype(BF16))
        x_ext = x_ext + _dot(jnp.concatenate(parts, axis=1), wo_ref[...])
    g_ref, wup_ref, cw_ref, cb_ref, wdn_ref = refs[:5]
    del refs[:5]
    if final_norm:
        gf_ref = refs.pop(0)
    o_ref, act_ref = refs
    hx = _rms(x_ext, g_ref[...]).astype(BF16)
    hn = hx[:tm]
    for c in range(0, FFN_DIM, COL_TILE):
        cs = slice(c, c + COL_TILE)
        acc = _dot(hx, wup_ref[:, cs])
        up = _dot(hn, wup_ref[:, FFN_DIM + c:FFN_DIM + c + COL_TILE])
        gp = acc[:tm]
        prev_row, next_row = _halo_rows(acc[tm:], tiles_per_seq)
        g_prev, g_next = _shift_rows(gp, prev_row, next_row)
        cw = cw_ref[:, cs]
        gate = cw[0:1] * g_prev + cw[1:2] * gp + cw[2:3] * g_next + cb_ref[:, cs]
        act_ref[:, cs] = (_silu(gate) * up).astype(BF16)
    out = x_ext[:tm] + _dot(act_ref[...], wdn_ref[...])
    if final_norm:
        out = _rms(out, gf_ref[...])
    o_ref[...] = out


def _layer_resident(stacked, layer):
    shape = stacked.shape[1:]
    return pl.BlockSpec((None,) + shape, lambda i: (layer,) + (0,) * len(shape),
                        pipeline_mode=pl.Buffered(1))


def _ffn(h2d, gain, w_up, conv_w, conv_b, w_down, layer, seq_len, final_gain, name, gdn_out=None):
    t, d = h2d.shape
    row = pl.BlockSpec((ROW_TILE, d), lambda i: (i, 0))
    prev, nxt = _halo_specs(d, t, HALO)
    in_specs = [row, prev, nxt]
    args = [h2d, h2d, h2d]
    if gdn_out is not None:
        o_gdn, p_main, gnorm, w_out = gdn_out
        z_blk = GDN_CONV_DIM // GDN_V
        for arr, cb in ((o_gdn, 0), (p_main, z_blk)):
            prev_b, nxt_b = _halo_specs(GDN_V, t, HALO_BF16, cb)
            in_specs += [pl.BlockSpec((ROW_TILE, GDN_V), lambda i, cb=cb: (i, cb)), prev_b, nxt_b]
            args += [arr, arr, arr]
        in_specs += [_resident(gnorm.shape), _resident(w_out.shape)]
        args += [gnorm, w_out]
    in_specs += [_resident((1, d)), _layer_resident(w_up, layer), _resident(conv_w.shape),
                 _resident(conv_b.shape), _layer_resident(w_down, layer)]
    args += [gain, w_up, conv_w, conv_b, w_down]
    if final_gain is not None:
        in_specs.append(_resident((1, d)))
        args.append(final_gain)
    kern = functools.partial(_ffn_kernel, tiles_per_seq=seq_len // ROW_TILE,
                             final_norm=final_gain is not None, gdn_out=gdn_out is not None)
    return pl.pallas_call(
        kern,
        grid=(t // ROW_TILE,),
        in_specs=in_specs,
        out_specs=row,
        out_shape=jax.ShapeDtypeStruct((t, d), F32),
        scratch_shapes=[pltpu.VMEM((ROW_TILE, FFN_DIM), BF16)],
        compiler_params=pltpu.CompilerParams(dimension_semantics=("arbitrary",),
                                             vmem_limit_bytes=VMEM_LIMIT),
        name=name,
    )(*args)


def _pair_rows(x):
    lo = _iota(x.shape, 1) < CHUNK
    return jnp.concatenate([jnp.where(lo, x, 0.0), jnp.where(lo, 0.0, x)], axis=0)


def _pair_blockdiag(x):
    lo = _iota(x.shape, 1) < LANES
    return jnp.concatenate([jnp.where(lo, x, 0.0), jnp.where(lo, 0.0, x)], axis=0)


def _block_tri_ones(lower, n=2 * CHUNK):
    r = _iota((n, n), 0)
    c = _iota((n, n), 1)
    tri = jnp.where((c <= r) if lower else (c >= r), 1.0, 0.0)
    return jnp.where((r // CHUNK) == (c // CHUNK), tri, 0.0).astype(BF16)


def _packed_tri_mask(lower, strict=False):
    r = _iota((CHUNK, LANES), 0)
    c = _iota((CHUNK, LANES), 1) & (CHUNK - 1)
    if lower:
        return (c < r) if strict else (c <= r)
    return (c > r) if strict else (c >= r)


def _gla_kernel(q_ref, k_ref, v_ref, gate_ref, lr_ref, wg_ref, bg_ref, gn_ref, o_ref,
                la_ref, of_ref, ob_ref, st_ref, *, seq_len):
    n_pairs = GLA_HEADS // 2
    n_dbl = seq_len // (2 * CHUNK)
    blk = 2 * CHUNK

    def prep(rb, carry):
        r0 = pl.multiple_of(rb * blk, blk)
        rows = pl.ds(r0, blk)
        z = _dot(lr_ref[rows, :].astype(BF16), wg_ref[...]) + bg_ref[...]
        log_sig = jnp.minimum(z, 0.0) - jnp.log(1.0 + jnp.exp(-jnp.abs(z)))
        la_ref[rows, :] = log_sig * (1.0 / GLA_GATE_NORMALIZER)
        return carry

    lax.fori_loop(0, n_dbl, prep, 0)
    st_ref[...] = jnp.zeros(st_ref.shape, F32)

    tri = (_block_tri_ones(True), _block_tri_ones(False))
    masks = (_packed_tri_mask(True), _packed_tri_mask(False))
    st_mask = (_iota((LANES, 2 * GLA_DV), 0) // GLA_DK) == (_iota((LANES, 2 * GLA_DV), 1) // GLA_DV)
    first_half = _iota((blk, GLA_QK), 0) < CHUNK
    o_refs = (of_ref, ob_ref)

    def body(it, carry):
        steps = []
        for sub in range(GLA_UNROLL):
            df = GLA_UNROLL * it + sub
            dbs = (df, n_dbl - 1 - df)
            chains = []
            for dirn in range(2):
                rows = pl.ds(pl.multiple_of(dbs[dirn] * blk, blk), blk)
                la = la_ref[rows, dirn * GLA_QK:(dirn + 1) * GLA_QK]
                q2, k2, v2 = q_ref[rows, :], k_ref[rows, :], v_ref[rows, :]
                cum = _dot_exact_lhs(tri[dirn], la)
                if dirn == 0:
                    tots = (cum[CHUNK - 1:CHUNK, :], cum[blk - 1:blk, :])
                else:
                    tots = (cum[0:1, :], cum[CHUNK:CHUNK + 1, :])
                tot_rows = jnp.where(first_half, tots[0], tots[1])
                q = q2.astype(F32) * (GLA_DK ** -0.5)
                k = k2.astype(F32)
                q_dec = (q * jnp.exp(cum)).astype(BF16)
                k_inv = k * jnp.exp(-cum)
                k_end = k * jnp.exp(tot_rows - cum)
                v32 = v2.astype(F32)
                for p in range(n_pairs):
                    ls = slice(p * LANES, (p + 1) * LANES)
                    vs = slice(p * 2 * GLA_DV, (p + 1) * 2 * GLA_DV)
                    dec = [jnp.broadcast_to(jnp.exp(tots[cc][:, ls]), (LANES, LANES)).T
                           for cc in range(2)]
                    chains.append(dict(dirn=dirn, p=p, rows=rows,
                                       order=(0, 1) if dirn == 0 else (1, 0),
                                       qd=q_dec[:, ls], k_inv=k_inv[:, ls],
                                       k_end=k_end[:, ls].astype(BF16),
                                       v=v32[:, vs], vb=v2[:, vs],
                                       dec=[jnp.concatenate([d, d], axis=1) for d in dec],
                                       o=[None, None]))
            steps.append(chains)
        states = {(dirn, p): st_ref[dirn, p] for dirn in range(2) for p in range(n_pairs)}
        for chains in steps:
            for ch in chains:
                for cc in range(2):
                    rs = slice(cc * CHUNK, (cc + 1) * CHUNK)
                    kbd = _pair_rows(ch["k_inv"][rs]).astype(BF16)
                    sc = jnp.where(masks[ch["dirn"]], _dot_nt(ch["qd"][rs], kbd), 0.0).astype(BF16)
                    ch["o"][cc] = _dot(sc, _pair_blockdiag(ch["v"][rs]).astype(BF16))
        for chains in steps:
            for step in range(2):
                for ch in chains:
                    cc = ch["order"][step]
                    rs = slice(cc * CHUNK, (cc + 1) * CHUNK)
                    state = states[(ch["dirn"], ch["p"])]
                    ch["o"][cc] = ch["o"][cc] + _dot(ch["qd"][rs], state.astype(BF16))
                    upd = _dot_tn(ch["k_end"][rs], ch["vb"][rs])
                    states[(ch["dirn"], ch["p"])] = state * ch["dec"][cc] + jnp.where(st_mask, upd, 0.0)
        for chains in steps:
            for dirn in range(2):
                mine = [ch for ch in chains if ch["dirn"] == dirn]
                o_refs[dirn][mine[0]["rows"], :] = jnp.concatenate(
                    [jnp.concatenate(ch["o"], axis=0) for ch in mine], axis=1)
        for (dirn, p), state in states.items():
            st_ref[dirn, p] = state
        return carry

    lax.fori_loop(0, n_dbl // GLA_UNROLL, body, 0)

    def finish(rb, carry):
        r0 = pl.multiple_of(rb * blk, blk)
        rows = pl.ds(r0, blk)
        for h in range(GLA_HEADS):
            hs = slice(h * GLA_DV, (h + 1) * GLA_DV)
            o = of_ref[rows, hs] + ob_ref[rows, hs]
            y = _rms(o, gn_ref[...]) * _silu(gate_ref[rows, hs].astype(F32))
            o_ref[rows, hs] = y.astype(o_ref.dtype)
        return carry

    lax.fori_loop(0, n_dbl, finish, 0)


def _gla(p_main, p_small, wg, bg, gnorm, batch, seq_len):
    t = batch * seq_len
    kern = functools.partial(_gla_kernel, seq_len=seq_len)
    qk_blk = GLA_QK
    return pl.pallas_call(
        kern,
        grid=(batch,),
        in_specs=[
            pl.BlockSpec((seq_len, GLA_QK), lambda b: (b, 0)),
            pl.BlockSpec((seq_len, GLA_QK), lambda b: (b, 1)),
            pl.BlockSpec((seq_len, GLA_V), lambda b: (b, (2 * qk_blk) // GLA_V)),
            pl.BlockSpec((seq_len, GLA_V), lambda b: (b, (2 * qk_blk) // GLA_V + 1)),
            pl.BlockSpec((seq_len, SMALL_W), lambda b: (b, 0)),
            _resident(wg.shape), _resident(bg.shape), _resident(gnorm.shape),
        ],
        out_specs=pl.BlockSpec((seq_len, GLA_V), lambda b: (b, 0)),
        out_shape=jax.ShapeDtypeStruct((t, GLA_V), BF16),
        scratch_shapes=[
            pltpu.VMEM((seq_len, 2 * GLA_QK), F32),
            pltpu.VMEM((seq_len, GLA_V), F32),
            pltpu.VMEM((seq_len, GLA_V), F32),
            pltpu.VMEM((2, GLA_HEADS // 2, LANES, 2 * GLA_DV), F32),
        ],
        compiler_params=pltpu.CompilerParams(dimension_semantics=("arbitrary",),
                                             vmem_limit_bytes=VMEM_LIMIT),
        name="gla_mixer",
    )(p_main, p_main, p_main, p_main, p_small, wg, bg, gnorm)


def _sgu_out_kernel(h_ref, oa_ref, su_ref, sv_ref, lng_ref, lnb_ref, ws_ref, bs_ref, wo_ref, o_ref):
    tm = h_ref.shape[0]
    u = _gelu_tanh(su_ref[...].astype(F32))
    g = _gelu_tanh(sv_ref[...].astype(F32))
    mu = jnp.mean(g, axis=-1, keepdims=True)
    gc = g - mu
    var = jnp.mean(gc * gc, axis=-1, keepdims=True)
    vv = (gc * lax.rsqrt(var + NORM_EPS) * lng_ref[...] + lnb_ref[...]).astype(BF16)
    rows = []
    for c in range(tm // SGU_CHUNK):
        rs = slice(c * SGU_CHUNK, (c + 1) * SGU_CHUNK)
        cols = []
        for gi in range(SGU_GROUPS):
            gs = slice(gi * SGU_GROUP_DIM, (gi + 1) * SGU_GROUP_DIM)
            cols.append(_dot(ws_ref[gi], vv[rs, gs]))
        rows.append(jnp.concatenate(cols, axis=1) + bs_ref[...])
    mixed = jnp.concatenate(rows, axis=0)
    ob = (u * mixed).astype(BF16)
    acc = _dot(oa_ref[...], wo_ref[:GLA_V, :]) + _dot(ob, wo_ref[GLA_V:, :])
    o_ref[...] = h_ref[...] + acc


def _sgu_out(h2d, o_a, p_main, ln_g, ln_b, w_s, b_full, w_out):
    t, d = h2d.shape
    su_blk = (2 * GLA_QK + 2 * GLA_V) // SGU_DIM
    return pl.pallas_call(
        _sgu_out_kernel,
        grid=(t // ROW_TILE,),
        in_specs=[
            pl.BlockSpec((ROW_TILE, d), lambda i: (i, 0)),
            pl.BlockSpec((ROW_TILE, GLA_V), lambda i: (i, 0)),
            pl.BlockSpec((ROW_TILE, SGU_DIM), lambda i: (i, su_blk)),
            pl.BlockSpec((ROW_TILE, SGU_DIM), lambda i: (i, su_blk + 1)),
            _resident(ln_g.shape), _resident(ln_b.shape), _resident(w_s.shape),
            _resident(b_full.shape), _resident(w_out.shape),
        ],
        out_specs=pl.BlockSpec((ROW_TILE, d), lambda i: (i, 0)),
        out_shape=jax.ShapeDtypeStruct((t, d), F32),
        compiler_params=pltpu.CompilerParams(dimension_semantics=("arbitrary",),
                                             vmem_limit_bytes=VMEM_LIMIT),
        name="sgu_out_proj",
    )(h2d, o_a, p_main, p_main, ln_g, ln_b, w_s, b_full, w_out)


GDN_SCALARS = 6
GDN_PAIRS_PER_STEP = 2
GDN_STEP_W = GDN_PAIRS_PER_STEP * 2 * GDN_DK
GDN_NORM_COL = 4 * GDN_HEADS
GDN_STEP_HEADS = 2 * GDN_PAIRS_PER_STEP
GDN_SCAN_UNROLL = 8
GDN_PRE_CHUNKS = 4


def _packed_product(x, y):
    lo = _iota(y.shape, 1) < CHUNK
    zero = jnp.zeros_like(y)
    return _dot(x, jnp.concatenate([jnp.where(lo, y, zero), jnp.where(lo, zero, y)], axis=0))


def _gdn_kernel(q_ref, k_ref, v_ref, sm_ref, tab_ref, esum_ref, eexp_ref, o_ref,
                tt_ref, aqd_ref, dec_ref, of_ref, ob_ref, st_ref, *, seq_len):
    n_chunks = seq_len // CHUNK
    blk = 2 * CHUNK
    npp = GDN_PAIRS_PER_STEP
    pw = 2 * GDN_DK

    pblk = GDN_PRE_CHUNKS * CHUNK
    tri = (_block_tri_ones(True, pblk), _block_tri_ones(False, pblk))
    incl = (_packed_tri_mask(True), _packed_tri_mask(False))
    strict = (_packed_tri_mask(True, strict=True), _packed_tri_mask(False, strict=True))
    diag = _iota((CHUNK, LANES), 0) == (_iota((CHUNK, LANES), 1) & (CHUNK - 1))
    lo_half = _iota((SUBLANES_F32, LANES), 1) < CHUNK
    eye = jnp.where(diag, 1.0, 0.0)

    def row_form(col_form):
        return jnp.sum(jnp.where(diag, col_form, 0.0), axis=0, keepdims=True)

    def precompute(it, carry):
        rows = pl.ds(pl.multiple_of(it * pblk, pblk), pblk)
        q2 = q_ref[rows, :]
        k2 = k_ref[rows, :]
        sm = sm_ref[rows, :]
        a_exp = jnp.exp(tab_ref[0:1, :])
        dt_b = tab_ref[1:2, :]

        lane = _iota(sm.shape, 1)
        gates = jnp.where(lane < 2 * GDN_HEADS, _sigmoid(sm), -a_exp * _softplus(sm + dt_b))
        qf = q2.astype(F32)
        kf = k2.astype(F32)
        ssq = _dot((qf * qf).astype(BF16), esum_ref[0]) + _dot((kf * kf).astype(BF16), esum_ref[1])
        inv = lax.rsqrt(ssq + NORM_EPS)
        is_qn = jnp.abs(2 * lane - (2 * GDN_NORM_COL + GDN_STEP_HEADS - 1)) < GDN_STEP_HEADS
        inv = jnp.where(is_qn, inv * (GDN_DK ** -0.5), inv)
        table = jnp.where(lane < GDN_NORM_COL, gates, inv)
        cum_f = _dot_exact_lhs(tri[0], table)
        cum_b = _dot_exact_lhs(tri[1], table)
        band = lane // GDN_HEADS
        table = jnp.where(band == 2, cum_f, jnp.where(band == 3, cum_b, table))
        hi, lo = _split(table)
        n_cum = 2 * npp * LANES
        e_cum = eexp_ref[0, :, :n_cum]
        ex_cum = _dot(hi, e_cum) + _dot(lo, e_cum)
        ex_rest = _dot(hi, eexp_ref[0, :, n_cum:])

        chains = []
        for cc in range(GDN_PRE_CHUNKS):
            rs = slice(cc * CHUNK, (cc + 1) * CHUNK)
            for pp in range(npp):
                ps = slice(pp * pw, (pp + 1) * pw)
                cols = [ex_cum[rs, (2 * pp + j) * LANES:(2 * pp + j + 1) * LANES] for j in range(2)]
                cols += [ex_rest[rs, (4 * pp + j) * LANES:(4 * pp + j + 1) * LANES] for j in range(4)]
                rq_c = cols[4]
                rk_c = cols[5]
                rk_r = row_form(rk_c)
                kbd = _pair_blockdiag(kf[rs, ps]).astype(BF16)
                gram = _dot_nt(jnp.concatenate([k2[rs, ps], q2[rs, ps]], axis=0), kbd)
                kk = gram[:CHUNK] * rk_c * rk_r
                qk = gram[CHUNK:] * rq_c * rk_r
                for dirn in range(2):
                    cum_c = cols[dirn]
                    beta_c = cols[2 + dirn]
                    tot_r = cum_c[CHUNK - 1:CHUNK, :] if dirn == 0 else cum_c[0:1, :]
                    cum_r = row_form(cum_c)
                    beta_r = row_form(beta_c)
                    decay = jnp.exp(jnp.where(incl[dirn], cum_c - cum_r, -1e30))
                    a = jnp.where(strict[dirn], kk * beta_c * decay, 0.0)
                    ci = GDN_PRE_CHUNKS * it + cc
                    d_q = jnp.where(diag, rq_c * jnp.exp(cum_c), 0.0)
                    aqd_ref[ci, dirn, pp] = jnp.concatenate([qk * decay, d_q], axis=1).astype(BF16)
                    tot8 = jnp.broadcast_to(tot_r, (SUBLANES_F32, LANES))
                    tot8r = pltpu.roll(tot8, CHUNK, axis=1)
                    dec_ref[ci, dirn, pp] = jnp.exp(jnp.concatenate(
                        [jnp.where(lo_half, tot8, tot8r), jnp.where(lo_half, tot8r, tot8)], axis=1))
                    chains.append(dict(ci=ci, pp=pp, dirn=dirn, pw_a=a, inv_m=eye - a,
                                       scale_u=beta_r, scale_w=beta_r * jnp.exp(cum_r) * rk_r,
                                       e_c=rk_c * jnp.exp(tot_r - cum_c)))
        for ch in chains:
            a_bf = ch["pw_a"].astype(BF16)
            ch["pw_a"] = _packed_product(a_bf, a_bf).astype(BF16)
        for _ in range(4):
            for ch in chains:
                lhs = jnp.concatenate([ch["inv_m"].astype(BF16), ch["pw_a"]], axis=0)
                both = _packed_product(lhs, ch["pw_a"])
                ch["inv_m"] = ch["inv_m"] + both[:CHUNK]
                ch["pw_a"] = both[CHUNK:].astype(BF16)
        for ch in chains:
            ch["inv_m"] = ch["inv_m"] + _packed_product(ch["inv_m"].astype(BF16), ch["pw_a"])
        for ch in chains:
            t_u = ch["inv_m"] * ch["scale_u"]
            t_w = ch["inv_m"] * ch["scale_w"]
            e_c = ch["e_c"]
            tt_ref[ch["ci"], ch["dirn"], ch["pp"]] = jnp.concatenate(
                [t_u, t_u * e_c, t_w, t_w * e_c], axis=0).astype(BF16)
        return carry

    lax.fori_loop(0, seq_len // pblk, precompute, 0)
    st_ref[...] = jnp.zeros(st_ref.shape, F32)

    o_refs = (of_ref, ob_ref)

    def scan(it, carry):
        heads = (slice(0, GDN_DV), slice(GDN_DV, 2 * GDN_DV))
        steps = []
        for sub in range(GDN_SCAN_UNROLL):
            cf = GDN_SCAN_UNROLL * it + sub
            cis = (cf, n_chunks - 1 - cf)
            chains = []
            for dirn in range(2):
                ci = cis[dirn]
                rows = pl.ds(pl.multiple_of(ci * CHUNK, CHUNK), CHUNK)
                q_c, k_c, v_c = q_ref[rows, :], k_ref[rows, :], v_ref[rows, :]
                for pp in range(npp):
                    ps = slice(pp * pw, (pp + 1) * pw)
                    chains.append(dict(dirn=dirn, pp=pp, rows=rows, tt=tt_ref[ci, dirn, pp],
                                       aqd=aqd_ref[ci, dirn, pp], dec=dec_ref[ci, dirn, pp],
                                       q=q_c[:, ps], k=k_c[:, ps], v=v_c[:, ps]))
            steps.append(chains)
        states = {(dirn, pp): [st_ref[dirn, pp, j] for j in range(2)]
                  for dirn in range(2) for pp in range(npp)}
        for chains in steps:
            for ch in chains:
                kbd = _pair_blockdiag(ch["k"].astype(F32)).astype(BF16)
                vbd = _pair_blockdiag(ch["v"].astype(F32)).astype(BF16)
                ch["uu"] = _dot(ch["tt"][:blk], vbd)
                ch["ww"] = _dot(ch["tt"][blk:], kbd)
        for chains in steps:
            for ch in chains:
                state = states[(ch["dirn"], ch["pp"])]
                lhs = jnp.concatenate([ch["ww"].astype(BF16), ch["q"]], axis=0)
                ch["prod"] = jnp.concatenate(
                    [_dot(lhs[:, hs], state[j].astype(BF16)) for j, hs in enumerate(heads)], axis=1)
            for ch in chains:
                state = states[(ch["dirn"], ch["pp"])]
                v_new_e = (ch["uu"][CHUNK:] - ch["prod"][CHUNK:blk]).astype(BF16)
                states[(ch["dirn"], ch["pp"])] = [
                    state[j] * ch["dec"][0:1, hs] + _dot_tn(ch["k"][:, hs], v_new_e[:, hs])
                    for j, hs in enumerate(heads)]
            for ch in chains:
                v_new = ch["uu"][:CHUNK] - ch["prod"][:CHUNK]
                rhs = jnp.concatenate([_pair_blockdiag(v_new), _pair_blockdiag(ch["prod"][blk:])], axis=0)
                ch["o"] = _dot(ch["aqd"], rhs.astype(BF16))
        for chains in steps:
            for dirn in range(2):
                mine = [ch for ch in chains if ch["dirn"] == dirn]
                o_refs[dirn][mine[0]["rows"], :] = jnp.concatenate([ch["o"] for ch in mine], axis=1)
        for (dirn, pp), state in states.items():
            for j in range(2):
                st_ref[dirn, pp, j] = state[j]
        return carry

    lax.fori_loop(0, n_chunks // GDN_SCAN_UNROLL, scan, 0)

    def finish(rb, carry):
        r0 = pl.multiple_of(rb * blk, blk)
        rows = pl.ds(r0, blk)
        o_ref[rows, :] = (of_ref[rows, :] + ob_ref[rows, :]).astype(o_ref.dtype)
        return carry

    lax.fori_loop(0, seq_len // blk, finish, 0)


def _gdn(p_main, p_small, tab, esum, eexp, batch, seq_len):
    t = batch * seq_len
    n_steps = GDN_QK // GDN_STEP_W
    n_chunks = seq_len // CHUNK
    npp = GDN_PAIRS_PER_STEP
    pw = 2 * GDN_DK
    sw = GDN_STEP_W
    kern = functools.partial(_gdn_kernel, seq_len=seq_len)
    return pl.pallas_call(
        kern,
        grid=(batch, n_steps),
        in_specs=[
            pl.BlockSpec((seq_len, sw), lambda b, p: (b, p)),
            pl.BlockSpec((seq_len, sw), lambda b, p: (b, GDN_QK // sw + p)),
            pl.BlockSpec((seq_len, sw), lambda b, p: (b, 2 * GDN_QK // sw + p)),
            pl.BlockSpec((seq_len, SMALL_W), lambda b, p: (b, 0)),
            pl.BlockSpec(tab.shape, lambda b, p: (0, 0)),
            pl.BlockSpec(esum.shape, lambda b, p: (0, 0, 0)),
            pl.BlockSpec((1, SMALL_W, npp * GDN_SCALARS * LANES), lambda b, p: (p, 0, 0)),
        ],
        out_specs=pl.BlockSpec((seq_len, sw), lambda b, p: (b, p)),
        out_shape=jax.ShapeDtypeStruct((t, GDN_V), BF16),
        scratch_shapes=[
            pltpu.VMEM((n_chunks, 2, npp, 4 * CHUNK, LANES), BF16),
            pltpu.VMEM((n_chunks, 2, npp, CHUNK, 2 * LANES), BF16),
            pltpu.VMEM((n_chunks, 2, npp, SUBLANES_F32, pw), F32),
            pltpu.VMEM((seq_len, sw), F32),
            pltpu.VMEM((seq_len, sw), F32),
            pltpu.VMEM((2, npp, 2, GDN_DK, GDN_DV), F32),
        ],
        compiler_params=pltpu.CompilerParams(dimension_semantics=("arbitrary", "arbitrary"),
                                             vmem_limit_bytes=VMEM_LIMIT),
        name="gdn_mixer",
    )(p_main, p_main, p_main, p_small, tab, esum, eexp)


def _pad_cols(w, width):
    return jnp.pad(w, ((0, 0), (0, width - w.shape[1])))


def _gla_params(ab_w_in, w_gate_fwd, b_gate_fwd, w_gate_bwd, b_gate_bwd):
    n_wide = 2 * GLA_QK + 2 * GLA_V
    lr0 = n_wide
    sg0 = lr0 + 2 * GLA_LOWRANK
    wide = ab_w_in[:, :n_wide]
    sgu = ab_w_in[:, sg0:sg0 + 2 * SGU_DIM]
    small = _pad_cols(ab_w_in[:, lr0:sg0], SMALL_W).astype(BF16)
    w = jnp.concatenate([wide, sgu], axis=1).astype(BF16)
    wg = jnp.zeros((SMALL_W, 2 * GLA_QK), F32)
    wg = wg.at[:GLA_LOWRANK, :GLA_QK].set(w_gate_fwd)
    wg = wg.at[GLA_LOWRANK:2 * GLA_LOWRANK, GLA_QK:].set(w_gate_bwd)
    bg = jnp.concatenate([b_gate_fwd, b_gate_bwd])[None, :]
    return w, small, wg.astype(BF16), bg


def _gdn_params(gdn_w_in, a_log_fwd, dt_bias_fwd, a_log_bwd, dt_bias_bwd):
    n_main = GDN_CONV_DIM + GDN_V
    small = _pad_cols(gdn_w_in[:, n_main:], SMALL_W).astype(BF16)
    w = gdn_w_in.astype(BF16)
    pad = SMALL_W - 4 * GDN_HEADS
    zeros2 = jnp.zeros((2 * GDN_HEADS,), F32)
    a_log = jnp.concatenate([zeros2, a_log_fwd, a_log_bwd, jnp.zeros((pad,), F32)])
    dt_b = jnp.concatenate([zeros2, dt_bias_fwd, dt_bias_bwd, jnp.zeros((pad,), F32)])
    tab = jnp.zeros((SUBLANES_F32, SMALL_W), F32).at[0].set(a_log).at[1].set(dt_b)
    ch_head = jnp.arange(GDN_STEP_W) // GDN_DK
    col = jnp.arange(SMALL_W)
    esum_q = (col[None, :] == (GDN_NORM_COL + ch_head)[:, None])
    esum_k = (col[None, :] == (GDN_NORM_COL + GDN_STEP_HEADS + ch_head)[:, None])
    esum = jnp.stack([esum_q, esum_k]).astype(BF16)
    n_steps = GDN_QK // GDN_STEP_W
    lane = jnp.arange(GDN_PAIRS_PER_STEP * GDN_SCALARS * LANES)
    blk = lane // LANES
    n_cum_blk = 2 * GDN_PAIRS_PER_STEP
    pair = jnp.where(blk < n_cum_blk, blk // 2, (blk - n_cum_blk) // 4)
    quant = jnp.where(blk < n_cum_blk, blk % 2, 2 + (blk - n_cum_blk) % 4)
    local_head = 2 * pair + (lane % LANES) // CHUNK
    head = GDN_STEP_HEADS * jnp.arange(n_steps)[:, None] + local_head[None, :]
    src_gate = jnp.array([2 * GDN_HEADS, 3 * GDN_HEADS, 0, GDN_HEADS])
    src_norm = GDN_NORM_COL + GDN_STEP_HEADS * (quant - 4) + local_head
    src = jnp.where(quant[None, :] < 4, src_gate[jnp.minimum(quant, 3)][None, :] + head,
                    src_norm[None, :])
    eexp = (col[None, :, None] == src[:, None, :]).astype(BF16)
    return w, small, tab, esum, eexp


def kernel(x, norm_mix, norm_ffn, norm_final, ab_w_in, gla_w_gate_fwd, gla_b_gate_fwd, gla_w_gate_bwd, gla_b_gate_bwd, gla_norm, sgu_ln_g, sgu_ln_b, sgu_w_s, sgu_b_s, ab_w_out, gdn_w_in, gdn_conv_w, gdn_a_log_fwd, gdn_dt_bias_fwd, gdn_a_log_bwd, gdn_dt_bias_bwd, gdn_norm, gdn_w_out, ffn_w_up, ffn_conv_w, ffn_conv_b, ffn_w_down):
    batch, seq_len, d = x.shape
    t = batch * seq_len
    assert seq_len % ROW_TILE == 0 and seq_len % (2 * CHUNK) == 0
    h = x.reshape(t, d)

    w0, w0_small, wg, bg = _gla_params(ab_w_in[0], gla_w_gate_fwd[0], gla_b_gate_fwd[0],
                                       gla_w_gate_bwd[0], gla_b_gate_bwd[0])
    n_main0 = 2 * GLA_QK + 2 * GLA_V + 2 * SGU_DIM
    p0, s0 = _norm_proj(h, norm_mix[0][None, :], w0, w0_small, None, seq_len, n_main0, 0,
                        "gla_sgu_in_proj")
    o_a = _gla(p0, s0, wg, bg, gla_norm[0][None, :], batch, seq_len)
    b_full = jnp.repeat(sgu_b_s[0].T, SGU_GROUP_DIM, axis=1)
    h = _sgu_out(h, o_a, p0, sgu_ln_g[0][None, :], sgu_ln_b[0][None, :],
                 sgu_w_s[0].astype(BF16), b_full, ab_w_out[0].astype(BF16))
    w_up_all = ffn_w_up.astype(BF16)
    w_down_all = ffn_w_down.astype(BF16)
    h = _ffn(h, norm_ffn[0][None, :], w_up_all, ffn_conv_w[0], ffn_conv_b[0][None, :], w_down_all,
             0, seq_len, None, "ffn0")

    w1, w1_small, tab, esum, eexp = _gdn_params(gdn_w_in[0], gdn_a_log_fwd[0], gdn_dt_bias_fwd[0],
                                                gdn_a_log_bwd[0], gdn_dt_bias_bwd[0])
    n_main1 = GDN_CONV_DIM + GDN_V
    p1, s1 = _norm_proj(h, norm_mix[1][None, :], w1, w1_small, gdn_conv_w[0], seq_len, n_main1,
                        GDN_CONV_DIM, "gdn_in_proj")
    o_g = _gdn(p1, s1, tab, esum, eexp, batch, seq_len)
    h = _ffn(h, norm_ffn[1][None, :], w_up_all, ffn_conv_w[1], ffn_conv_b[1][None, :], w_down_all,
             1, seq_len, norm_final[None, :], "gdn_out_ffn1",
             gdn_out=(o_g, p1, gdn_norm[0][None, :], gdn_w_out[0].astype(BF16)))
    return h.reshape(batch, seq_len, d)
```

```python
import functools

import jax
import jax.numpy as jnp
from jax import lax
from jax.experimental import pallas as pl
from jax.experimental.pallas import tpu as pltpu

F32 = jnp.float32
BF16 = jnp.bfloat16

NORM_EPS = 1e-6
GLA_HEADS = 4
GLA_DK = 64
GLA_DV = 128
GLA_QK = GLA_HEADS * GLA_DK
GLA_V = GLA_HEADS * GLA_DV
GLA_LOWRANK = 16
GLA_GATE_NORMALIZER = 16.0
SGU_GROUPS = 4
SGU_GROUP_DIM = 128
SGU_DIM = SGU_GROUPS * SGU_GROUP_DIM
SGU_CHUNK = 128
GDN_HEADS = 8
GDN_DK = 128
GDN_DV = 128
GDN_QK = GDN_HEADS * GDN_DK
GDN_V = GDN_HEADS * GDN_DV
GDN_CONV_DIM = 2 * GDN_QK + GDN_V
FFN_DIM = 2816

LANES = 128
SUBLANES_F32 = 8
CHUNK = 64
SMALL_W = LANES

ROW_TILE = 1024
COL_TILE = 256
ROW_GROUPS = 4
HALO = SUBLANES_F32
HALO_BF16 = 2 * SUBLANES_F32
VMEM_LIMIT = 56 * 1024 * 1024
GLA_UNROLL = 4


def _dot(a, b):
    return jnp.dot(a, b, preferred_element_type=F32)


def _dot_nt(a, b):
    return lax.dot_general(a, b, (((1,), (1,)), ((), ())), preferred_element_type=F32)


def _dot_tn(a, b):
    return lax.dot_general(a, b, (((0,), (0,)), ((), ())), preferred_element_type=F32)


def _split(a):
    hi = a.astype(BF16)
    lo = (a - hi.astype(F32)).astype(BF16)
    return hi, lo


def _dot_exact_lhs(l_bf16, a):
    hi, lo = _split(a)
    return _dot(l_bf16, hi) + _dot(l_bf16, lo)


def _rms(x, gain):
    ms = jnp.mean(x * x, axis=-1, keepdims=True)
    return x * lax.rsqrt(ms + NORM_EPS) * gain


def _sigmoid(x):
    return 1.0 / (1.0 + jnp.exp(-x))


def _silu(x):
    return x * _sigmoid(x)


def _softplus(x):
    return jnp.maximum(x, 0.0) + jnp.log(1.0 + jnp.exp(-jnp.abs(x)))


def _gelu_tanh(x):
    c = 0.7978845608028654
    return 0.5 * x * (1.0 + jnp.tanh(c * (x + 0.044715 * (x * x * x))))


def _iota(shape, dim):
    return lax.broadcasted_iota(jnp.int32, shape, dim)


def _shift_rows(g, first_row, last_row):
    n = g.shape[0]
    row = _iota(g.shape, 0)
    g_prev = jnp.where(row == 0, first_row, pltpu.roll(g, 1, axis=0))
    g_next = jnp.where(row == n - 1, last_row, pltpu.roll(g, n - 1, axis=0))
    return g_prev, g_next


def _halo_rows(gh, tiles_per_seq):
    i = pl.program_id(0)
    pos = i % tiles_per_seq
    keep_prev = (pos != 0).astype(F32)
    keep_next = (pos != tiles_per_seq - 1).astype(F32)
    prev_row = gh[HALO - 1:HALO, :] * keep_prev
    next_row = gh[HALO:HALO + 1, :] * keep_next
    return prev_row, next_row


def _normed_with_halo(x_ref, xp_ref, xn_ref, g_ref):
    g = g_ref[...]
    hn = _rms(x_ref[...], g).astype(BF16)
    halo = jnp.concatenate([xp_ref[...], xn_ref[...]], axis=0)
    hh = _rms(halo, g).astype(BF16)
    return hn, jnp.concatenate([hn, hh], axis=0)


def _row_groups(tm, n_rows):
    step = tm // ROW_GROUPS
    edges = [g * step for g in range(ROW_GROUPS)] + [n_rows]
    return [slice(edges[g], edges[g + 1]) for g in range(ROW_GROUPS)]


def _norm_and_first_dot(x_ext, gain, tm, w_first, w_first_main=None):
    hs, accs, mains = [], [], []
    for rs in _row_groups(tm, x_ext.shape[0]):
        h = _rms(x_ext[rs], gain).astype(BF16)
        hs.append(h)
        accs.append(_dot(h, w_first))
        if w_first_main is not None:
            mains.append(_dot(h[:min(rs.stop, tm) - rs.start], w_first_main))
    main = jnp.concatenate(mains, axis=0) if mains else None
    return jnp.concatenate(hs, axis=0), jnp.concatenate(accs, axis=0), main


def _norm_proj_kernel(*refs, n_main, conv_cols, tiles_per_seq):
    if conv_cols:
        x_ref, xp_ref, xn_ref, g_ref, w_ref, ws_ref, cw_ref, o_ref, s_ref = refs
        hn, hx = _normed_with_halo(x_ref, xp_ref, xn_ref, g_ref)
    else:
        x_ref, g_ref, w_ref, ws_ref, o_ref, s_ref = refs
        hn = _rms(x_ref[...], g_ref[...]).astype(BF16)
    tm = hn.shape[0]
    for c in range(0, n_main, COL_TILE):
        cs = slice(c, c + COL_TILE)
        if c < conv_cols:
            acc = _dot(hx, w_ref[:, cs])
            gp = acc[:tm]
            prev_row, next_row = _halo_rows(acc[tm:], tiles_per_seq)
            g_prev, g_next = _shift_rows(gp, prev_row, next_row)
            cw = cw_ref[:, cs]
            y = cw[0:1] * g_prev + cw[1:2] * gp + cw[2:3] * g_next
            o_ref[:, cs] = _silu(y).astype(o_ref.dtype)
        else:
            o_ref[:, cs] = _dot(hn, w_ref[:, cs]).astype(o_ref.dtype)
    s_ref[...] = _dot(hn, ws_ref[...])


def _halo_specs(d, n_rows, halo, col_blk=0):
    blocks_per_tile = ROW_TILE // halo
    last = n_rows // halo - 1
    prev = pl.BlockSpec((halo, d), lambda i: (jnp.maximum(i * blocks_per_tile - 1, 0), col_blk))
    nxt = pl.BlockSpec((halo, d), lambda i: (jnp.minimum((i + 1) * blocks_per_tile, last), col_blk))
    return prev, nxt


def _resident(shape):
    return pl.BlockSpec(shape, lambda i: (0,) * len(shape), pipeline_mode=pl.Buffered(1))


def _norm_proj(x2d, gain, w, w_small, conv_w, seq_len, n_main, conv_cols, name):
    t, d = x2d.shape
    row = pl.BlockSpec((ROW_TILE, d), lambda i: (i, 0))
    in_specs = [row]
    args = [x2d]
    if conv_cols:
        prev, nxt = _halo_specs(d, t, HALO)
        in_specs += [prev, nxt]
        args += [x2d, x2d]
    in_specs += [_resident((1, d)), _resident(w.shape), _resident(w_small.shape)]
    args += [gain, w, w_small]
    if conv_cols:
        in_specs.append(_resident(conv_w.shape))
        args.append(conv_w)
    kern = functools.partial(_norm_proj_kernel, n_main=n_main, conv_cols=conv_cols,
                             tiles_per_seq=seq_len // ROW_TILE)
    return pl.pallas_call(
        kern,
        grid=(t // ROW_TILE,),
        in_specs=in_specs,
        out_specs=[pl.BlockSpec((ROW_TILE, n_main), lambda i: (i, 0)),
                   pl.BlockSpec((ROW_TILE, SMALL_W), lambda i: (i, 0))],
        out_shape=[jax.ShapeDtypeStruct((t, n_main), BF16),
                   jax.ShapeDtypeStruct((t, SMALL_W), F32)],
        compiler_params=pltpu.CompilerParams(dimension_semantics=("arbitrary",),
                                             vmem_limit_bytes=VMEM_LIMIT),
        name=name,
    )(*args)


def _with_halo_bf16(m_ref, mp_ref, mn_ref):
    return jnp.concatenate([m_ref[...].astype(F32), mp_ref[...].astype(F32)[HALO_BF16 - HALO:],
                            mn_ref[...].astype(F32)[:HALO]], axis=0)


def _ffn_kernel(*refs, tiles_per_seq, final_norm, gdn_out):
    refs = list(refs)
    x_ref, xp_ref, xn_ref = refs[:3]
    del refs[:3]
    x_ext = jnp.concatenate([x_ref[...], xp_ref[...], xn_ref[...]], axis=0)
    tm = x_ref.shape[0]
    if gdn_out:
        m_ref, mp_ref, mn_ref, z_ref, zp_ref, zn_ref, gn_ref, wo_ref = refs[:8]
        del refs[:8]
        m_ext = _with_halo_bf16(m_ref, mp_ref, mn_ref)
        z_ext = _with_halo_bf16(z_ref, zp_ref, zn_ref)
        x_parts = []
        for rs in _row_groups(tm, x_ext.shape[0]):
            parts = []
            for h in range(GDN_HEADS):
                hs = slice(h * GDN_DV, (h + 1) * GDN_DV)
                parts.append((_rms(m_ext[rs, hs], gn_ref[...]) * _silu(z_ext[rs, hs])).astype(BF16))
            x_parts.append(x_ext[rs] + _dot(jnp.concatenate(parts, axis=1), wo_ref[...]))
        x_ext = jnp.concatenate(x_parts, axis=0)
    g_ref, wup_ref, cw_ref, cb_ref, wdn_ref = refs[:5]
    del refs[:5]
    if final_norm:
        gf_ref = refs.pop(0)
    o_ref, act_ref = refs
    hx, acc0, up0 = _norm_and_first_dot(x_ext, g_ref[...], tm, wup_ref[:, :COL_TILE],
                                        wup_ref[:, FFN_DIM:FFN_DIM + COL_TILE])
    hn = hx[:tm]
    for c in range(0, FFN_DIM, COL_TILE):
        cs = slice(c, c + COL_TILE)
        acc = acc0 if c == 0 else _dot(hx, wup_ref[:, cs])
        up = up0 if c == 0 else _dot(hn, wup_ref[:, FFN_DIM + c:FFN_DIM + c + COL_TILE])
        gp = acc[:tm]
        prev_row, next_row = _halo_rows(acc[tm:], tiles_per_seq)
        g_prev, g_next = _shift_rows(gp, prev_row, next_row)
        cw = cw_ref[:, cs]
        gate = cw[0:1] * g_prev + cw[1:2] * gp + cw[2:3] * g_next + cb_ref[:, cs]
        act_ref[:, cs] = (_silu(gate) * up).astype(BF16)
    for rs in _row_groups(tm, tm):
        out = x_ext[rs] + _dot(act_ref[rs, :], wdn_ref[...])
        if final_norm:
            out = _rms(out, gf_ref[...])
        o_ref[rs, :] = out


def _layer_resident(stacked, layer):
    shape = stacked.shape[1:]
    return pl.BlockSpec((None,) + shape, lambda i: (layer,) + (0,) * len(shape),
                        pipeline_mode=pl.Buffered(1))


def _ffn(h2d, gain, w_up, conv_w, conv_b, w_down, layer, seq_len, final_gain, name, gdn_out=None):
    t, d = h2d.shape
    row = pl.BlockSpec((ROW_TILE, d), lambda i: (i, 0))
    prev, nxt = _halo_specs(d, t, HALO)
    in_specs = [row, prev, nxt]
    args = [h2d, h2d, h2d]
    if gdn_out is not None:
        o_gdn, p_main, gnorm, w_out = gdn_out
        z_blk = GDN_CONV_DIM // GDN_V
        for arr, cb in ((o_gdn, 0), (p_main, z_blk)):
            prev_b, nxt_b = _halo_specs(GDN_V, t, HALO_BF16, cb)
            in_specs += [pl.BlockSpec((ROW_TILE, GDN_V), lambda i, cb=cb: (i, cb)), prev_b, nxt_b]
            args += [arr, arr, arr]
        in_specs += [_resident(gnorm.shape), _resident(w_out.shape)]
        args += [gnorm, w_out]
    in_specs += [_resident((1, d)), _layer_resident(w_up, layer), _resident(conv_w.shape),
                 _resident(conv_b.shape), _layer_resident(w_down, layer)]
    args += [gain, w_up, conv_w, conv_b, w_down]
    if final_gain is not None:
        in_specs.append(_resident((1, d)))
        args.append(final_gain)
    kern = functools.partial(_ffn_kernel, tiles_per_seq=seq_len // ROW_TILE,
                             final_norm=final_gain is not None, gdn_out=gdn_out is not None)
    return pl.pallas_call(
        kern,
        grid=(t // ROW_TILE,),
        in_specs=in_specs,
        out_specs=row,
        out_shape=jax.ShapeDtypeStruct((t, d), F32),
        scratch_shapes=[pltpu.VMEM((ROW_TILE, FFN_DIM), BF16)],
        compiler_params=pltpu.CompilerParams(dimension_semantics=("arbitrary",),
                                             vmem_limit_bytes=VMEM_LIMIT),
        name=name,
    )(*args)


def _pair_rows(x):
    lo = _iota(x.shape, 1) < CHUNK
    return jnp.concatenate([jnp.where(lo, x, 0.0), jnp.where(lo, 0.0, x)], axis=0)


def _pair_blockdiag(x):
    lo = _iota(x.shape, 1) < LANES
    return jnp.concatenate([jnp.where(lo, x, 0.0), jnp.where(lo, 0.0, x)], axis=0)


def _block_tri_ones(lower, n=2 * CHUNK):
    r = _iota((n, n), 0)
    c = _iota((n, n), 1)
    tri = jnp.where((c <= r) if lower else (c >= r), 1.0, 0.0)
    return jnp.where((r // CHUNK) == (c // CHUNK), tri, 0.0).astype(BF16)


def _packed_tri_mask(lower, strict=False):
    r = _iota((CHUNK, LANES), 0)
    c = _iota((CHUNK, LANES), 1) & (CHUNK - 1)
    if lower:
        return (c < r) if strict else (c <= r)
    return (c > r) if strict else (c >= r)


def _gla_kernel(q_ref, k_ref, v_ref, gate_ref, lr_ref, wg_ref, bg_ref, gn_ref, o_ref,
                la_ref, of_ref, ob_ref, st_ref, *, seq_len):
    n_pairs = GLA_HEADS // 2
    n_dbl = seq_len // (2 * CHUNK)
    blk = 2 * CHUNK

    def prep(rb, carry):
        r0 = pl.multiple_of(rb * blk, blk)
        rows = pl.ds(r0, blk)
        z = _dot(lr_ref[rows, :].astype(BF16), wg_ref[...]) + bg_ref[...]
        log_sig = jnp.minimum(z, 0.0) - jnp.log(1.0 + jnp.exp(-jnp.abs(z)))
        la_ref[rows, :] = log_sig * (1.0 / GLA_GATE_NORMALIZER)
        return carry

    lax.fori_loop(0, n_dbl, prep, 0)
    st_ref[...] = jnp.zeros(st_ref.shape, F32)

    tri = (_block_tri_ones(True), _block_tri_ones(False))
    masks = (_packed_tri_mask(True), _packed_tri_mask(False))
    st_mask = (_iota((LANES, 2 * GLA_DV), 0) // GLA_DK) == (_iota((LANES, 2 * GLA_DV), 1) // GLA_DV)
    first_half = _iota((blk, GLA_QK), 0) < CHUNK
    o_refs = (of_ref, ob_ref)

    def body(it, carry):
        steps = []
        for sub in range(GLA_UNROLL):
            df = GLA_UNROLL * it + sub
            dbs = (df, n_dbl - 1 - df)
            chains = []
            for dirn in range(2):
                rows = pl.ds(pl.multiple_of(dbs[dirn] * blk, blk), blk)
                la = la_ref[rows, dirn * GLA_QK:(dirn + 1) * GLA_QK]
                q2, k2, v2 = q_ref[rows, :], k_ref[rows, :], v_ref[rows, :]
                cum = _dot_exact_lhs(tri[dirn], la)
                if dirn == 0:
                    tots = (cum[CHUNK - 1:CHUNK, :], cum[blk - 1:blk, :])
                else:
                    tots = (cum[0:1, :], cum[CHUNK:CHUNK + 1, :])
                tot_rows = jnp.where(first_half, tots[0], tots[1])
                q = q2.astype(F32) * (GLA_DK ** -0.5)
                k = k2.astype(F32)
                q_dec = (q * jnp.exp(cum)).astype(BF16)
                k_inv = k * jnp.exp(-cum)
                k_end = k * jnp.exp(tot_rows - cum)
                v32 = v2.astype(F32)
                for p in range(n_pairs):
                    ls = slice(p * LANES, (p + 1) * LANES)
                    vs = slice(p * 2 * GLA_DV, (p + 1) * 2 * GLA_DV)
                    dec = [jnp.broadcast_to(jnp.exp(tots[cc][:, ls]), (LANES, LANES)).T
                           for cc in range(2)]
                    chains.append(dict(dirn=dirn, p=p, rows=rows,
                                       order=(0, 1) if dirn == 0 else (1, 0),
                                       qd=q_dec[:, ls], k_inv=k_inv[:, ls],
                                       k_end=k_end[:, ls].astype(BF16),
                                       v=v32[:, vs], vb=v2[:, vs],
                                       dec=[jnp.concatenate([d, d], axis=1) for d in dec],
                                       o=[None, None]))
            steps.append(chains)
        states = {(dirn, p): st_ref[dirn, p] for dirn in range(2) for p in range(n_pairs)}
        for chains in steps:
            for ch in chains:
                for cc in range(2):
                    rs = slice(cc * CHUNK, (cc + 1) * CHUNK)
                    kbd = _pair_rows(ch["k_inv"][rs]).astype(BF16)
                    sc = jnp.where(masks[ch["dirn"]], _dot_nt(ch["qd"][rs], kbd), 0.0).astype(BF16)
                    ch["o"][cc] = _dot(sc, _pair_blockdiag(ch["v"][rs]).astype(BF16))
        for chains in steps:
            for step in range(2):
                for ch in chains:
                    cc = ch["order"][step]
                    rs = slice(cc * CHUNK, (cc + 1) * CHUNK)
                    state = states[(ch["dirn"], ch["p"])]
                    ch["o"][cc] = ch["o"][cc] + _dot(ch["qd"][rs], state.astype(BF16))
                    upd = _dot_tn(ch["k_end"][rs], ch["vb"][rs])
                    states[(ch["dirn"], ch["p"])] = state * ch["dec"][cc] + jnp.where(st_mask, upd, 0.0)
        for chains in steps:
            for dirn in range(2):
                mine = [ch for ch in chains if ch["dirn"] == dirn]
                o_refs[dirn][mine[0]["rows"], :] = jnp.concatenate(
                    [jnp.concatenate(ch["o"], axis=0) for ch in mine], axis=1)
        for (dirn, p), state in states.items():
            st_ref[dirn, p] = state
        return carry

    lax.fori_loop(0, n_dbl // GLA_UNROLL, body, 0)

    def finish(rb, carry):
        r0 = pl.multiple_of(rb * blk, blk)
        rows = pl.ds(r0, blk)
        for h in range(GLA_HEADS):
            hs = slice(h * GLA_DV, (h + 1) * GLA_DV)
            o = of_ref[rows, hs] + ob_ref[rows, hs]
            y = _rms(o, gn_ref[...]) * _silu(gate_ref[rows, hs].astype(F32))
            o_ref[rows, hs] = y.astype(o_ref.dtype)
        return carry

    lax.fori_loop(0, n_dbl, finish, 0)


def _gla(p_main, p_small, wg, bg, gnorm, batch, seq_len):
    t = batch * seq_len
    kern = functools.partial(_gla_kernel, seq_len=seq_len)
    qk_blk = GLA_QK
    return pl.pallas_call(
        kern,
        grid=(batch,),
        in_specs=[
            pl.BlockSpec((seq_len, GLA_QK), lambda b: (b, 0)),
            pl.BlockSpec((seq_len, GLA_QK), lambda b: (b, 1)),
            pl.BlockSpec((seq_len, GLA_V), lambda b: (b, (2 * qk_blk) // GLA_V)),
            pl.BlockSpec((seq_len, GLA_V), lambda b: (b, (2 * qk_blk) // GLA_V + 1)),
            pl.BlockSpec((seq_len, SMALL_W), lambda b: (b, 0)),
            _resident(wg.shape), _resident(bg.shape), _resident(gnorm.shape),
        ],
        out_specs=pl.BlockSpec((seq_len, GLA_V), lambda b: (b, 0)),
        out_shape=jax.ShapeDtypeStruct((t, GLA_V), BF16),
        scratch_shapes=[
            pltpu.VMEM((seq_len, 2 * GLA_QK), F32),
            pltpu.VMEM((seq_len, GLA_V), F32),
            pltpu.VMEM((seq_len, GLA_V), F32),
            pltpu.VMEM((2, GLA_HEADS // 2, LANES, 2 * GLA_DV), F32),
        ],
        compiler_params=pltpu.CompilerParams(dimension_semantics=("arbitrary",),
                                             vmem_limit_bytes=VMEM_LIMIT),
        name="gla_mixer",
    )(p_main, p_main, p_main, p_main, p_small, wg, bg, gnorm)


def _sgu_out_kernel(h_ref, oa_ref, su_ref, sv_ref, lng_ref, lnb_ref, ws_ref, bs_ref, wo_ref, o_ref):
    tm = h_ref.shape[0]
    for rg in _row_groups(tm, tm):
        u = _gelu_tanh(su_ref[rg, :].astype(F32))
        g = _gelu_tanh(sv_ref[rg, :].astype(F32))
        mu = jnp.mean(g, axis=-1, keepdims=True)
        gc = g - mu
        var = jnp.mean(gc * gc, axis=-1, keepdims=True)
        vv = (gc * lax.rsqrt(var + NORM_EPS) * lng_ref[...] + lnb_ref[...]).astype(BF16)
        rows = []
        for c in range((rg.stop - rg.start) // SGU_CHUNK):
            rs = slice(c * SGU_CHUNK, (c + 1) * SGU_CHUNK)
            cols = []
            for gi in range(SGU_GROUPS):
                gs = slice(gi * SGU_GROUP_DIM, (gi + 1) * SGU_GROUP_DIM)
                cols.append(_dot(ws_ref[gi], vv[rs, gs]))
            rows.append(jnp.concatenate(cols, axis=1) + bs_ref[...])
        ob = (u * jnp.concatenate(rows, axis=0)).astype(BF16)
        acc = _dot(oa_ref[rg, :], wo_ref[:GLA_V, :]) + _dot(ob, wo_ref[GLA_V:, :])
        o_ref[rg, :] = h_ref[rg, :] + acc


def _sgu_out(h2d, o_a, p_main, ln_g, ln_b, w_s, b_full, w_out):
    t, d = h2d.shape
    su_blk = (2 * GLA_QK + 2 * GLA_V) // SGU_DIM
    return pl.pallas_call(
        _sgu_out_kernel,
        grid=(t // ROW_TILE,),
        in_specs=[
            pl.BlockSpec((ROW_TILE, d), lambda i: (i, 0)),
            pl.BlockSpec((ROW_TILE, GLA_V), lambda i: (i, 0)),
            pl.BlockSpec((ROW_TILE, SGU_DIM), lambda i: (i, su_blk)),
            pl.BlockSpec((ROW_TILE, SGU_DIM), lambda i: (i, su_blk + 1)),
            _resident(ln_g.shape), _resident(ln_b.shape), _resident(w_s.shape),
            _resident(b_full.shape), _resident(w_out.shape),
        ],
        out_specs=pl.BlockSpec((ROW_TILE, d), lambda i: (i, 0)),
        out_shape=jax.ShapeDtypeStruct((t, d), F32),
        compiler_params=pltpu.CompilerParams(dimension_semantics=("arbitrary",),
                                             vmem_limit_bytes=VMEM_LIMIT),
        name="sgu_out_proj",
    )(h2d, o_a, p_main, p_main, ln_g, ln_b, w_s, b_full, w_out)


GDN_SCALARS = 6
GDN_PAIRS_PER_STEP = 2
GDN_STEP_W = GDN_PAIRS_PER_STEP * 2 * GDN_DK
GDN_NORM_COL = 4 * GDN_HEADS
GDN_STEP_HEADS = 2 * GDN_PAIRS_PER_STEP
GDN_SCAN_UNROLL = 8
GDN_PRE_CHUNKS = 4


def _packed_product(x, y):
    lo = _iota(y.shape, 1) < CHUNK
    zero = jnp.zeros_like(y)
    return _dot(x, jnp.concatenate([jnp.where(lo, y, zero), jnp.where(lo, zero, y)], axis=0))


def _gdn_kernel(q_ref, k_ref, v_ref, sm_ref, tab_ref, esum_ref, eexp_ref, o_ref,
                tt_ref, aqd_ref, dec_ref, of_ref, ob_ref, st_ref, *, seq_len):
    n_chunks = seq_len // CHUNK
    blk = 2 * CHUNK
    npp = GDN_PAIRS_PER_STEP
    pw = 2 * GDN_DK

    pblk = GDN_PRE_CHUNKS * CHUNK
    tri = (_block_tri_ones(True, pblk), _block_tri_ones(False, pblk))
    incl = (_packed_tri_mask(True), _packed_tri_mask(False))
    strict = (_packed_tri_mask(True, strict=True), _packed_tri_mask(False, strict=True))
    diag = _iota((CHUNK, LANES), 0) == (_iota((CHUNK, LANES), 1) & (CHUNK - 1))
    lo_half = _iota((SUBLANES_F32, LANES), 1) < CHUNK
    eye = jnp.where(diag, 1.0, 0.0)

    def row_form(col_form):
        return jnp.sum(jnp.where(diag, col_form, 0.0), axis=0, keepdims=True)

    def precompute(it, carry):
        rows = pl.ds(pl.multiple_of(it * pblk, pblk), pblk)
        q2 = q_ref[rows, :]
        k2 = k_ref[rows, :]
        sm = sm_ref[rows, :]
        a_exp = jnp.exp(tab_ref[0:1, :])
        dt_b = tab_ref[1:2, :]

        lane = _iota(sm.shape, 1)
        gates = jnp.where(lane < 2 * GDN_HEADS, _sigmoid(sm), -a_exp * _softplus(sm + dt_b))
        qf = q2.astype(F32)
        kf = k2.astype(F32)
        ssq = _dot((qf * qf).astype(BF16), esum_ref[0]) + _dot((kf * kf).astype(BF16), esum_ref[1])
        inv = lax.rsqrt(ssq + NORM_EPS)
        is_qn = jnp.abs(2 * lane - (2 * GDN_NORM_COL + GDN_STEP_HEADS - 1)) < GDN_STEP_HEADS
        inv = jnp.where(is_qn, inv * (GDN_DK ** -0.5), inv)
        table = jnp.where(lane < GDN_NORM_COL, gates, inv)
        cum_f = _dot_exact_lhs(tri[0], table)
        cum_b = _dot_exact_lhs(tri[1], table)
        band = lane // GDN_HEADS
        table = jnp.where(band == 2, cum_f, jnp.where(band == 3, cum_b, table))
        hi, lo = _split(table)
        n_cum = 2 * npp * LANES
        e_cum = eexp_ref[0, :, :n_cum]
        ex_cum = _dot(hi, e_cum) + _dot(lo, e_cum)
        ex_rest = _dot(hi, eexp_ref[0, :, n_cum:])

        chains = []
        for cc in range(GDN_PRE_CHUNKS):
            rs = slice(cc * CHUNK, (cc + 1) * CHUNK)
            for pp in range(npp):
                ps = slice(pp * pw, (pp + 1) * pw)
                cols = [ex_cum[rs, (2 * pp + j) * LANES:(2 * pp + j + 1) * LANES] for j in range(2)]
                cols += [ex_rest[rs, (4 * pp + j) * LANES:(4 * pp + j + 1) * LANES] for j in range(4)]
                rq_c = cols[4]
                rk_c = cols[5]
                rk_r = row_form(rk_c)
                kbd = _pair_blockdiag(kf[rs, ps]).astype(BF16)
                gram = _dot_nt(jnp.concatenate([k2[rs, ps], q2[rs, ps]], axis=0), kbd)
                kk = gram[:CHUNK] * rk_c * rk_r
                qk = gram[CHUNK:] * rq_c * rk_r
                for dirn in range(2):
                    cum_c = cols[dirn]
                    beta_c = cols[2 + dirn]
                    tot_r = cum_c[CHUNK - 1:CHUNK, :] if dirn == 0 else cum_c[0:1, :]
                    cum_r = row_form(cum_c)
                    beta_r = row_form(beta_c)
                    decay = jnp.exp(jnp.where(incl[dirn], cum_c - cum_r, -1e30))
                    a = jnp.where(strict[dirn], kk * beta_c * decay, 0.0)
                    ci = GDN_PRE_CHUNKS * it + cc
                    d_q = jnp.where(diag, rq_c * jnp.exp(cum_c), 0.0)
                    aqd_ref[ci, dirn, pp] = jnp.concatenate([qk * decay, d_q], axis=1).astype(BF16)
                    tot8 = jnp.broadcast_to(tot_r, (SUBLANES_F32, LANES))
                    tot8r = pltpu.roll(tot8, CHUNK, axis=1)
                    dec_ref[ci, dirn, pp] = jnp.exp(jnp.concatenate(
                        [jnp.where(lo_half, tot8, tot8r), jnp.where(lo_half, tot8r, tot8)], axis=1))
                    chains.append(dict(ci=ci, pp=pp, dirn=dirn, pw_a=a, inv_m=eye - a,
                                       scale_u=beta_r, scale_w=beta_r * jnp.exp(cum_r) * rk_r,
                                       e_c=rk_c * jnp.exp(tot_r - cum_c)))
        for ch in chains:
            a_bf = ch["pw_a"].astype(BF16)
            ch["pw_a"] = _packed_product(a_bf, a_bf).astype(BF16)
        for _ in range(4):
            for ch in chains:
                lhs = jnp.concatenate([ch["inv_m"].astype(BF16), ch["pw_a"]], axis=0)
                both = _packed_product(lhs, ch["pw_a"])
                ch["inv_m"] = ch["inv_m"] + both[:CHUNK]
                ch["pw_a"] = both[CHUNK:].astype(BF16)
        for ch in chains:
            ch["inv_m"] = ch["inv_m"] + _packed_product(ch["inv_m"].astype(BF16), ch["pw_a"])
        for ch in chains:
            t_u = ch["inv_m"] * ch["scale_u"]
            t_w = ch["inv_m"] * ch["scale_w"]
            e_c = ch["e_c"]
            tt_ref[ch["ci"], ch["dirn"], ch["pp"]] = jnp.concatenate(
                [t_u, t_u * e_c, t_w, t_w * e_c], axis=0).astype(BF16)
        return carry

    lax.fori_loop(0, seq_len // pblk, precompute, 0)
    st_ref[...] = jnp.zeros(st_ref.shape, F32)

    o_refs = (of_ref, ob_ref)

    def scan(it, carry):
        heads = (slice(0, GDN_DV), slice(GDN_DV, 2 * GDN_DV))
        steps = []
        for sub in range(GDN_SCAN_UNROLL):
            cf = GDN_SCAN_UNROLL * it + sub
            cis = (cf, n_chunks - 1 - cf)
            chains = []
            for dirn in range(2):
                ci = cis[dirn]
                rows = pl.ds(pl.multiple_of(ci * CHUNK, CHUNK), CHUNK)
                q_c, k_c, v_c = q_ref[rows, :], k_ref[rows, :], v_ref[rows, :]
                for pp in range(npp):
                    ps = slice(pp * pw, (pp + 1) * pw)
                    chains.append(dict(dirn=dirn, pp=pp, rows=rows, tt=tt_ref[ci, dirn, pp],
                                       aqd=aqd_ref[ci, dirn, pp], dec=dec_ref[ci, dirn, pp],
                                       q=q_c[:, ps], k=k_c[:, ps], v=v_c[:, ps]))
            steps.append(chains)
        states = {(dirn, pp): [st_ref[dirn, pp, j] for j in range(2)]
                  for dirn in range(2) for pp in range(npp)}
        for chains in steps:
            for ch in chains:
                kbd = _pair_blockdiag(ch["k"].astype(F32)).astype(BF16)
                vbd = _pair_blockdiag(ch["v"].astype(F32)).astype(BF16)
                ch["uu"] = _dot(ch["tt"][:blk], vbd)
                ch["ww"] = _dot(ch["tt"][blk:], kbd)
        for chains in steps:
            for ch in chains:
                state = states[(ch["dirn"], ch["pp"])]
                lhs = jnp.concatenate([ch["ww"].astype(BF16), ch["q"]], axis=0)
                ch["prod"] = jnp.concatenate(
                    [_dot(lhs[:, hs], state[j].astype(BF16)) for j, hs in enumerate(heads)], axis=1)
            for ch in chains:
                state = states[(ch["dirn"], ch["pp"])]
                v_new_e = (ch["uu"][CHUNK:] - ch["prod"][CHUNK:blk]).astype(BF16)
                states[(ch["dirn"], ch["pp"])] = [
                    state[j] * ch["dec"][0:1, hs] + _dot_tn(ch["k"][:, hs], v_new_e[:, hs])
                    for j, hs in enumerate(heads)]
            for ch in chains:
                v_new = ch["uu"][:CHUNK] - ch["prod"][:CHUNK]
                rhs = jnp.concatenate([_pair_blockdiag(v_new), _pair_blockdiag(ch["prod"][blk:])], axis=0)
                ch["o"] = _dot(ch["aqd"], rhs.astype(BF16))
        for chains in steps:
            for dirn in range(2):
                mine = [ch for ch in chains if ch["dirn"] == dirn]
                o_refs[dirn][mine[0]["rows"], :] = jnp.concatenate([ch["o"] for ch in mine], axis=1)
        for (dirn, pp), state in states.items():
            for j in range(2):
                st_ref[dirn, pp, j] = state[j]
        return carry

    lax.fori_loop(0, n_chunks // GDN_SCAN_UNROLL, scan, 0)

    def finish(rb, carry):
        r0 = pl.multiple_of(rb * blk, blk)
        rows = pl.ds(r0, blk)
        o_ref[rows, :] = (of_ref[rows, :] + ob_ref[rows, :]).astype(o_ref.dtype)
        return carry

    lax.fori_loop(0, seq_len // blk, finish, 0)


def _gdn(p_main, p_small, tab, esum, eexp, batch, seq_len):
    t = batch * seq_len
    n_steps = GDN_QK // GDN_STEP_W
    n_chunks = seq_len // CHUNK
    npp = GDN_PAIRS_PER_STEP
    pw = 2 * GDN_DK
    sw = GDN_STEP_W
    kern = functools.partial(_gdn_kernel, seq_len=seq_len)
    return pl.pallas_call(
        kern,
        grid=(batch, n_steps),
        in_specs=[
            pl.BlockSpec((seq_len, sw), lambda b, p: (b, p)),
            pl.BlockSpec((seq_len, sw), lambda b, p: (b, GDN_QK // sw + p)),
            pl.BlockSpec((seq_len, sw), lambda b, p: (b, 2 * GDN_QK // sw + p)),
            pl.BlockSpec((seq_len, SMALL_W), lambda b, p: (b, 0)),
            pl.BlockSpec(tab.shape, lambda b, p: (0, 0)),
            pl.BlockSpec(esum.shape, lambda b, p: (0, 0, 0)),
            pl.BlockSpec((1, SMALL_W, npp * GDN_SCALARS * LANES), lambda b, p: (p, 0, 0)),
        ],
        out_specs=pl.BlockSpec((seq_len, sw), lambda b, p: (b, p)),
        out_shape=jax.ShapeDtypeStruct((t, GDN_V), BF16),
        scratch_shapes=[
            pltpu.VMEM((n_chunks, 2, npp, 4 * CHUNK, LANES), BF16),
            pltpu.VMEM((n_chunks, 2, npp, CHUNK, 2 * LANES), BF16),
            pltpu.VMEM((n_chunks, 2, npp, SUBLANES_F32, pw), F32),
            pltpu.VMEM((seq_len, sw), F32),
            pltpu.VMEM((seq_len, sw), F32),
            pltpu.VMEM((2, npp, 2, GDN_DK, GDN_DV), F32),
        ],
        compiler_params=pltpu.CompilerParams(dimension_semantics=("arbitrary", "arbitrary"),
                                             vmem_limit_bytes=VMEM_LIMIT),
        name="gdn_mixer",
    )(p_main, p_main, p_main, p_small, tab, esum, eexp)


def _pad_cols(w, width):
    return jnp.pad(w, ((0, 0), (0, width - w.shape[1])))


def _gla_params(ab_w_in, w_gate_fwd, b_gate_fwd, w_gate_bwd, b_gate_bwd):
    n_wide = 2 * GLA_QK + 2 * GLA_V
    lr0 = n_wide
    sg0 = lr0 + 2 * GLA_LOWRANK
    w_all = ab_w_in.astype(BF16)
    small = _pad_cols(w_all[:, lr0:sg0], SMALL_W)
    w = jnp.concatenate([w_all[:, :n_wide], w_all[:, sg0:sg0 + 2 * SGU_DIM]], axis=1)
    wg = jnp.zeros((SMALL_W, 2 * GLA_QK), F32)
    wg = wg.at[:GLA_LOWRANK, :GLA_QK].set(w_gate_fwd)
    wg = wg.at[GLA_LOWRANK:2 * GLA_LOWRANK, GLA_QK:].set(w_gate_bwd)
    bg = jnp.concatenate([b_gate_fwd, b_gate_bwd])[None, :]
    return w, small, wg.astype(BF16), bg


def _gdn_params(gdn_w_in, a_log_fwd, dt_bias_fwd, a_log_bwd, dt_bias_bwd):
    n_main = GDN_CONV_DIM + GDN_V
    w = gdn_w_in.astype(BF16)
    small = _pad_cols(w[:, n_main:], SMALL_W)
    pad = SMALL_W - 4 * GDN_HEADS
    zeros2 = jnp.zeros((2 * GDN_HEADS,), F32)
    a_log = jnp.concatenate([zeros2, a_log_fwd, a_log_bwd, jnp.zeros((pad,), F32)])
    dt_b = jnp.concatenate([zeros2, dt_bias_fwd, dt_bias_bwd, jnp.zeros((pad,), F32)])
    tab = jnp.zeros((SUBLANES_F32, SMALL_W), F32).at[0].set(a_log).at[1].set(dt_b)
    ch_head = jnp.arange(GDN_STEP_W) // GDN_DK
    col = jnp.arange(SMALL_W)
    esum_q = (col[None, :] == (GDN_NORM_COL + ch_head)[:, None])
    esum_k = (col[None, :] == (GDN_NORM_COL + GDN_STEP_HEADS + ch_head)[:, None])
    esum = jnp.stack([esum_q, esum_k]).astype(BF16)
    n_steps = GDN_QK // GDN_STEP_W
    lane = jnp.arange(GDN_PAIRS_PER_STEP * GDN_SCALARS * LANES)
    blk = lane // LANES
    n_cum_blk = 2 * GDN_PAIRS_PER_STEP
    pair = jnp.where(blk < n_cum_blk, blk // 2, (blk - n_cum_blk) // 4)
    quant = jnp.where(blk < n_cum_blk, blk % 2, 2 + (blk - n_cum_blk) % 4)
    local_head = 2 * pair + (lane % LANES) // CHUNK
    head = GDN_STEP_HEADS * jnp.arange(n_steps)[:, None] + local_head[None, :]
    src_gate = jnp.array([2 * GDN_HEADS, 3 * GDN_HEADS, 0, GDN_HEADS])
    src_norm = GDN_NORM_COL + GDN_STEP_HEADS * (quant - 4) + local_head
    src = jnp.where(quant[None, :] < 4, src_gate[jnp.minimum(quant, 3)][None, :] + head,
                    src_norm[None, :])
    eexp = (col[None, :, None] == src[:, None, :]).astype(BF16)
    return w, small, tab, esum, eexp


def kernel(x, norm_mix, norm_ffn, norm_final, ab_w_in, gla_w_gate_fwd, gla_b_gate_fwd, gla_w_gate_bwd, gla_b_gate_bwd, gla_norm, sgu_ln_g, sgu_ln_b, sgu_w_s, sgu_b_s, ab_w_out, gdn_w_in, gdn_conv_w, gdn_a_log_fwd, gdn_dt_bias_fwd, gdn_a_log_bwd, gdn_dt_bias_bwd, gdn_norm, gdn_w_out, ffn_w_up, ffn_conv_w, ffn_conv_b, ffn_w_down):
    batch, seq_len, d = x.shape
    t = batch * seq_len
    assert seq_len % ROW_TILE == 0 and seq_len % (2 * CHUNK) == 0
    h = x.reshape(t, d)

    w0, w0_small, wg, bg = _gla_params(ab_w_in[0], gla_w_gate_fwd[0], gla_b_gate_fwd[0],
                                       gla_w_gate_bwd[0], gla_b_gate_bwd[0])
    n_main0 = 2 * GLA_QK + 2 * GLA_V + 2 * SGU_DIM
    p0, s0 = _norm_proj(h, norm_mix[0][None, :], w0, w0_small, None, seq_len, n_main0, 0,
                        "gla_sgu_in_proj")
    o_a = _gla(p0, s0, wg, bg, gla_norm[0][None, :], batch, seq_len)
    b_full = jnp.repeat(sgu_b_s[0].T, SGU_GROUP_DIM, axis=1)
    h = _sgu_out(h, o_a, p0, sgu_ln_g[0][None, :], sgu_ln_b[0][None, :],
                 sgu_w_s[0].astype(BF16), b_full, ab_w_out[0].astype(BF16))
    w_up_all = ffn_w_up.astype(BF16)
    w_down_all = ffn_w_down.astype(BF16)
    h = _ffn(h, norm_ffn[0][None, :], w_up_all, ffn_conv_w[0], ffn_conv_b[0][None, :], w_down_all,
             0, seq_len, None, "ffn0")

    w1, w1_small, tab, esum, eexp = _gdn_params(gdn_w_in[0], gdn_a_log_fwd[0], gdn_dt_bias_fwd[0],
                                                gdn_a_log_bwd[0], gdn_dt_bias_bwd[0])
    n_main1 = GDN_CONV_DIM + GDN_V
    p1, s1 = _norm_proj(h, norm_mix[1][None, :], w1, w1_small, gdn_conv_w[0], seq_len, n_main1,
                        GDN_CONV_DIM, "gdn_in_proj")
    o_g = _gdn(p1, s1, tab, esum, eexp, batch, seq_len)
    h = _ffn(h, norm_ffn[1][None, :], w_up_all, ffn_conv_w[1], ffn_conv_b[1][None, :], w_down_all,
             1, seq_len, norm_final[None, :], "gdn_out_ffn1",
             gdn_out=(o_g, p1, gdn_norm[0][None, :], gdn_w_out[0].astype(BF16)))
    return h.reshape(batch, seq_len, d)
```

```python
import functools

import jax
import jax.numpy as jnp
from jax import lax
from jax.experimental import pallas as pl
from jax.experimental.pallas import tpu as pltpu

F32 = jnp.float32
BF16 = jnp.bfloat16

NORM_EPS = 1e-6
GLA_HEADS = 4
GLA_DK = 64
GLA_DV = 128
GLA_QK = GLA_HEADS * GLA_DK
GLA_V = GLA_HEADS * GLA_DV
GLA_LOWRANK = 16
GLA_GATE_NORMALIZER = 16.0
SGU_GROUPS = 4
SGU_GROUP_DIM = 128
SGU_DIM = SGU_GROUPS * SGU_GROUP_DIM
SGU_CHUNK = 128
GDN_HEADS = 8
GDN_DK = 128
GDN_DV = 128
GDN_QK = GDN_HEADS * GDN_DK
GDN_V = GDN_HEADS * GDN_DV
GDN_CONV_DIM = 2 * GDN_QK + GDN_V
FFN_DIM = 2816

LANES = 128
SUBLANES_F32 = 8
CHUNK = 64
SMALL_W = LANES

ROW_TILE = 1024
COL_TILE = 256
ROW_GROUPS = 4
HALO = SUBLANES_F32
HALO_BF16 = 2 * SUBLANES_F32
VMEM_LIMIT = 56 * 1024 * 1024
GLA_PREP_ROWS = 512
GLA_UNROLL = 4


def _dot(a, b):
    return jnp.dot(a, b, preferred_element_type=F32)


def _dot_nt(a, b):
    return lax.dot_general(a, b, (((1,), (1,)), ((), ())), preferred_element_type=F32)


def _dot_tn(a, b):
    return lax.dot_general(a, b, (((0,), (0,)), ((), ())), preferred_element_type=F32)


def _split(a):
    hi = a.astype(BF16)
    lo = (a - hi.astype(F32)).astype(BF16)
    return hi, lo


def _dot_exact_lhs(l_bf16, a):
    hi, lo = _split(a)
    return _dot(l_bf16, hi) + _dot(l_bf16, lo)


def _rms(x, gain):
    ms = jnp.mean(x * x, axis=-1, keepdims=True)
    return x * lax.rsqrt(ms + NORM_EPS) * gain


def _sigmoid(x):
    return 1.0 / (1.0 + jnp.exp(-x))


def _silu(x):
    return x * _sigmoid(x)


def _softplus(x):
    return jnp.maximum(x, 0.0) + jnp.log(1.0 + jnp.exp(-jnp.abs(x)))


def _gelu_tanh(x):
    c = 0.7978845608028654
    return 0.5 * x * (1.0 + jnp.tanh(c * (x + 0.044715 * (x * x * x))))


def _iota(shape, dim):
    return lax.broadcasted_iota(jnp.int32, shape, dim)


def _shift_rows(g, first_row, last_row):
    n = g.shape[0]
    row = _iota(g.shape, 0)
    g_prev = jnp.where(row == 0, first_row, pltpu.roll(g, 1, axis=0))
    g_next = jnp.where(row == n - 1, last_row, pltpu.roll(g, n - 1, axis=0))
    return g_prev, g_next


def _halo_rows(gh, tiles_per_seq):
    i = pl.program_id(0)
    pos = i % tiles_per_seq
    keep_prev = (pos != 0).astype(F32)
    keep_next = (pos != tiles_per_seq - 1).astype(F32)
    prev_row = gh[HALO - 1:HALO, :] * keep_prev
    next_row = gh[HALO:HALO + 1, :] * keep_next
    return prev_row, next_row


def _normed_with_halo(x_ref, xp_ref, xn_ref, g_ref):
    g = g_ref[...]
    hn = _rms(x_ref[...], g).astype(BF16)
    halo = jnp.concatenate([xp_ref[...], xn_ref[...]], axis=0)
    hh = _rms(halo, g).astype(BF16)
    return hn, jnp.concatenate([hn, hh], axis=0)


def _row_groups(tm, n_rows):
    step = tm // ROW_GROUPS
    edges = [g * step for g in range(ROW_GROUPS)] + [n_rows]
    return [slice(edges[g], edges[g + 1]) for g in range(ROW_GROUPS)]


def _norm_and_first_dot(x_ext, gain, tm, w_first, w_first_main=None):
    hs, accs, mains = [], [], []
    for rs in _row_groups(tm, x_ext.shape[0]):
        h = _rms(x_ext[rs], gain).astype(BF16)
        hs.append(h)
        accs.append(_dot(h, w_first))
        if w_first_main is not None:
            mains.append(_dot(h[:min(rs.stop, tm) - rs.start], w_first_main))
    main = jnp.concatenate(mains, axis=0) if mains else None
    return jnp.concatenate(hs, axis=0), jnp.concatenate(accs, axis=0), main


def _norm_proj_kernel(*refs, n_main, conv_cols, tiles_per_seq):
    if conv_cols:
        x_ref, xp_ref, xn_ref, g_ref, w_ref, ws_ref, cw_ref, o_ref, s_ref = refs
        hn, hx = _normed_with_halo(x_ref, xp_ref, xn_ref, g_ref)
    else:
        x_ref, g_ref, w_ref, ws_ref, o_ref, s_ref = refs
        hn = _rms(x_ref[...], g_ref[...]).astype(BF16)
    tm = hn.shape[0]
    for c in range(0, n_main, COL_TILE):
        cs = slice(c, c + COL_TILE)
        if c < conv_cols:
            acc = _dot(hx, w_ref[:, cs])
            gp = acc[:tm]
            prev_row, next_row = _halo_rows(acc[tm:], tiles_per_seq)
            g_prev, g_next = _shift_rows(gp, prev_row, next_row)
            cw = cw_ref[:, cs]
            y = cw[0:1] * g_prev + cw[1:2] * gp + cw[2:3] * g_next
            o_ref[:, cs] = _silu(y).astype(o_ref.dtype)
        else:
            o_ref[:, cs] = _dot(hn, w_ref[:, cs]).astype(o_ref.dtype)
    s_ref[...] = _dot(hn, ws_ref[...])


def _halo_specs(d, n_rows, halo, col_blk=0):
    blocks_per_tile = ROW_TILE // halo
    last = n_rows // halo - 1
    prev = pl.BlockSpec((halo, d), lambda i: (jnp.maximum(i * blocks_per_tile - 1, 0), col_blk))
    nxt = pl.BlockSpec((halo, d), lambda i: (jnp.minimum((i + 1) * blocks_per_tile, last), col_blk))
    return prev, nxt


def _resident(shape):
    return pl.BlockSpec(shape, lambda i: (0,) * len(shape), pipeline_mode=pl.Buffered(1))


def _norm_proj(x2d, gain, w, w_small, conv_w, seq_len, n_main, conv_cols, name):
    t, d = x2d.shape
    row = pl.BlockSpec((ROW_TILE, d), lambda i: (i, 0))
    in_specs = [row]
    args = [x2d]
    if conv_cols:
        prev, nxt = _halo_specs(d, t, HALO)
        in_specs += [prev, nxt]
        args += [x2d, x2d]
    in_specs += [_resident((1, d)), _resident(w.shape), _resident(w_small.shape)]
    args += [gain, w, w_small]
    if conv_cols:
        in_specs.append(_resident(conv_w.shape))
        args.append(conv_w)
    kern = functools.partial(_norm_proj_kernel, n_main=n_main, conv_cols=conv_cols,
                             tiles_per_seq=seq_len // ROW_TILE)
    return pl.pallas_call(
        kern,
        grid=(t // ROW_TILE,),
        in_specs=in_specs,
        out_specs=[pl.BlockSpec((ROW_TILE, n_main), lambda i: (i, 0)),
                   pl.BlockSpec((ROW_TILE, SMALL_W), lambda i: (i, 0))],
        out_shape=[jax.ShapeDtypeStruct((t, n_main), BF16),
                   jax.ShapeDtypeStruct((t, SMALL_W), F32)],
        compiler_params=pltpu.CompilerParams(dimension_semantics=("arbitrary",),
                                             vmem_limit_bytes=VMEM_LIMIT),
        name=name,
    )(*args)


def _with_halo_bf16(m_ref, mp_ref, mn_ref):
    return jnp.concatenate([m_ref[...].astype(F32), mp_ref[...].astype(F32)[HALO_BF16 - HALO:],
                            mn_ref[...].astype(F32)[:HALO]], axis=0)


def _ffn_kernel(*refs, tiles_per_seq, final_norm, gdn_out):
    refs = list(refs)
    x_ref, xp_ref, xn_ref = refs[:3]
    del refs[:3]
    x_ext = jnp.concatenate([x_ref[...], xp_ref[...], xn_ref[...]], axis=0)
    tm = x_ref.shape[0]
    if gdn_out:
        m_ref, mp_ref, mn_ref, z_ref, zp_ref, zn_ref, gn_ref, wo_ref = refs[:8]
        del refs[:8]
        m_ext = _with_halo_bf16(m_ref, mp_ref, mn_ref)
        z_ext = _with_halo_bf16(z_ref, zp_ref, zn_ref)
        x_parts = []
        for rs in _row_groups(tm, x_ext.shape[0]):
            parts = []
            for h in range(GDN_HEADS):
                hs = slice(h * GDN_DV, (h + 1) * GDN_DV)
                parts.append((_rms(m_ext[rs, hs], gn_ref[...]) * _silu(z_ext[rs, hs])).astype(BF16))
            x_parts.append(x_ext[rs] + _dot(jnp.concatenate(parts, axis=1), wo_ref[...]))
        x_ext = jnp.concatenate(x_parts, axis=0)
    g_ref, wup_ref, cw_ref, cb_ref, wdn_ref = refs[:5]
    del refs[:5]
    if final_norm:
        gf_ref = refs.pop(0)
    o_ref, act_ref = refs
    hx, acc0, up0 = _norm_and_first_dot(x_ext, g_ref[...], tm, wup_ref[:, :COL_TILE],
                                        wup_ref[:, FFN_DIM:FFN_DIM + COL_TILE])
    hn = hx[:tm]
    for c in range(0, FFN_DIM, COL_TILE):
        cs = slice(c, c + COL_TILE)
        acc = acc0 if c == 0 else _dot(hx, wup_ref[:, cs])
        up = up0 if c == 0 else _dot(hn, wup_ref[:, FFN_DIM + c:FFN_DIM + c + COL_TILE])
        gp = acc[:tm]
        prev_row, next_row = _halo_rows(acc[tm:], tiles_per_seq)
        g_prev, g_next = _shift_rows(gp, prev_row, next_row)
        cw = cw_ref[:, cs]
        gate = cw[0:1] * g_prev + cw[1:2] * gp + cw[2:3] * g_next + cb_ref[:, cs]
        act_ref[:, cs] = (_silu(gate) * up).astype(BF16)
    for rs in _row_groups(tm, tm):
        out = x_ext[rs] + _dot(act_ref[rs, :], wdn_ref[...])
        if final_norm:
            out = _rms(out, gf_ref[...])
        o_ref[rs, :] = out


def _layer_resident(stacked, layer):
    shape = stacked.shape[1:]
    return pl.BlockSpec((None,) + shape, lambda i: (layer,) + (0,) * len(shape),
                        pipeline_mode=pl.Buffered(1))


def _ffn(h2d, gain, w_up, conv_w, conv_b, w_down, layer, seq_len, final_gain, name, gdn_out=None):
    t, d = h2d.shape
    row = pl.BlockSpec((ROW_TILE, d), lambda i: (i, 0))
    prev, nxt = _halo_specs(d, t, HALO)
    in_specs = [row, prev, nxt]
    args = [h2d, h2d, h2d]
    if gdn_out is not None:
        o_gdn, p_main, gnorm, w_out = gdn_out
        z_blk = GDN_CONV_DIM // GDN_V
        for arr, cb in ((o_gdn, 0), (p_main, z_blk)):
            prev_b, nxt_b = _halo_specs(GDN_V, t, HALO_BF16, cb)
            in_specs += [pl.BlockSpec((ROW_TILE, GDN_V), lambda i, cb=cb: (i, cb)), prev_b, nxt_b]
            args += [arr, arr, arr]
        in_specs += [_resident(gnorm.shape), _resident(w_out.shape)]
        args += [gnorm, w_out]
    in_specs += [_resident((1, d)), _layer_resident(w_up, layer), _resident(conv_w.shape),
                 _resident(conv_b.shape), _layer_resident(w_down, layer)]
    args += [gain, w_up, conv_w, conv_b, w_down]
    if final_gain is not None:
        in_specs.append(_resident((1, d)))
        args.append(final_gain)
    kern = functools.partial(_ffn_kernel, tiles_per_seq=seq_len // ROW_TILE,
                             final_norm=final_gain is not None, gdn_out=gdn_out is not None)
    return pl.pallas_call(
        kern,
        grid=(t // ROW_TILE,),
        in_specs=in_specs,
        out_specs=row,
        out_shape=jax.ShapeDtypeStruct((t, d), F32),
        scratch_shapes=[pltpu.VMEM((ROW_TILE, FFN_DIM), BF16)],
        compiler_params=pltpu.CompilerParams(dimension_semantics=("arbitrary",),
                                             vmem_limit_bytes=VMEM_LIMIT),
        name=name,
    )(*args)


def _pair_rows(x):
    lo = _iota(x.shape, 1) < CHUNK
    return jnp.concatenate([jnp.where(lo, x, 0.0), jnp.where(lo, 0.0, x)], axis=0)


def _pair_blockdiag(x):
    lo = _iota(x.shape, 1) < LANES
    return jnp.concatenate([jnp.where(lo, x, 0.0), jnp.where(lo, 0.0, x)], axis=0)


def _block_tri_ones(lower, n=2 * CHUNK):
    r = _iota((n, n), 0)
    c = _iota((n, n), 1)
    tri = jnp.where((c <= r) if lower else (c >= r), 1.0, 0.0)
    return jnp.where((r // CHUNK) == (c // CHUNK), tri, 0.0).astype(BF16)


def _packed_tri_mask(lower, strict=False):
    r = _iota((CHUNK, LANES), 0)
    c = _iota((CHUNK, LANES), 1) & (CHUNK - 1)
    if lower:
        return (c < r) if strict else (c <= r)
    return (c > r) if strict else (c >= r)


def _gla_kernel(q_ref, k_ref, v_ref, gate_ref, lr_ref, wg_ref, bg_ref, gn_ref, o_ref,
                la_ref, of_ref, ob_ref, st_ref, *, seq_len):
    n_pairs = GLA_HEADS // 2
    n_dbl = seq_len // (2 * CHUNK)
    blk = 2 * CHUNK

    def prep(rb, carry):
        rows = pl.ds(pl.multiple_of(rb * GLA_PREP_ROWS, GLA_PREP_ROWS), GLA_PREP_ROWS)
        z = _dot(lr_ref[rows, :].astype(BF16), wg_ref[...]) + bg_ref[...]
        log_sig = jnp.minimum(z, 0.0) - jnp.log(1.0 + jnp.exp(-jnp.abs(z)))
        la_ref[rows, :] = log_sig * (1.0 / GLA_GATE_NORMALIZER)
        return carry

    lax.fori_loop(0, seq_len // GLA_PREP_ROWS, prep, 0)
    st_ref[...] = jnp.zeros(st_ref.shape, F32)

    tri = (_block_tri_ones(True), _block_tri_ones(False))
    masks = (_packed_tri_mask(True), _packed_tri_mask(False))
    st_mask = (_iota((LANES, 2 * GLA_DV), 0) // GLA_DK) == (_iota((LANES, 2 * GLA_DV), 1) // GLA_DV)
    first_half = _iota((blk, GLA_QK), 0) < CHUNK
    o_refs = (of_ref, ob_ref)

    def body(it, carry):
        steps = []
        for sub in range(GLA_UNROLL):
            df = GLA_UNROLL * it + sub
            dbs = (df, n_dbl - 1 - df)
            chains = []
            for dirn in range(2):
                rows = pl.ds(pl.multiple_of(dbs[dirn] * blk, blk), blk)
                la = la_ref[rows, dirn * GLA_QK:(dirn + 1) * GLA_QK]
                q2, k2, v2 = q_ref[rows, :], k_ref[rows, :], v_ref[rows, :]
                cum = _dot_exact_lhs(tri[dirn], la)
                if dirn == 0:
                    tots = (cum[CHUNK - 1:CHUNK, :], cum[blk - 1:blk, :])
                else:
                    tots = (cum[0:1, :], cum[CHUNK:CHUNK + 1, :])
                tot_rows = jnp.where(first_half, tots[0], tots[1])
                q = q2.astype(F32) * (GLA_DK ** -0.5)
                k = k2.astype(F32)
                q_dec = (q * jnp.exp(cum)).astype(BF16)
                k_inv = k * jnp.exp(-cum)
                k_end = k * jnp.exp(tot_rows - cum)
                v32 = v2.astype(F32)
                for p in range(n_pairs):
                    ls = slice(p * LANES, (p + 1) * LANES)
                    vs = slice(p * 2 * GLA_DV, (p + 1) * 2 * GLA_DV)
                    dec = [jnp.broadcast_to(jnp.exp(tots[cc][:, ls]), (LANES, LANES)).T
                           for cc in range(2)]
                    chains.append(dict(dirn=dirn, p=p, rows=rows,
                                       order=(0, 1) if dirn == 0 else (1, 0),
                                       qd=q_dec[:, ls], k_inv=k_inv[:, ls],
                                       k_end=k_end[:, ls].astype(BF16),
                                       v=v32[:, vs], vb=v2[:, vs],
                                       dec=[jnp.concatenate([d, d], axis=1) for d in dec],
                                       o=[None, None]))
            steps.append(chains)
        states = {(dirn, p): st_ref[dirn, p] for dirn in range(2) for p in range(n_pairs)}
        for chains in steps:
            for ch in chains:
                for cc in range(2):
                    rs = slice(cc * CHUNK, (cc + 1) * CHUNK)
                    kbd = _pair_rows(ch["k_inv"][rs]).astype(BF16)
                    sc = jnp.where(masks[ch["dirn"]], _dot_nt(ch["qd"][rs], kbd), 0.0).astype(BF16)
                    ch["o"][cc] = _dot(sc, _pair_blockdiag(ch["v"][rs]).astype(BF16))
        for chains in steps:
            for step in range(2):
                for ch in chains:
                    cc = ch["order"][step]
                    rs = slice(cc * CHUNK, (cc + 1) * CHUNK)
                    state = states[(ch["dirn"], ch["p"])]
                    ch["o"][cc] = ch["o"][cc] + _dot(ch["qd"][rs], state.astype(BF16))
                    upd = _dot_tn(ch["k_end"][rs], ch["vb"][rs])
                    states[(ch["dirn"], ch["p"])] = state * ch["dec"][cc] + jnp.where(st_mask, upd, 0.0)
        for chains in steps:
            for dirn in range(2):
                mine = [ch for ch in chains if ch["dirn"] == dirn]
                o_refs[dirn][mine[0]["rows"], :] = jnp.concatenate(
                    [jnp.concatenate(ch["o"], axis=0) for ch in mine], axis=1)
        for (dirn, p), state in states.items():
            st_ref[dirn, p] = state
        return carry

    lax.fori_loop(0, n_dbl // GLA_UNROLL, body, 0)

    def finish(rb, carry):
        r0 = pl.multiple_of(rb * blk, blk)
        rows = pl.ds(r0, blk)
        for h in range(GLA_HEADS):
            hs = slice(h * GLA_DV, (h + 1) * GLA_DV)
            o = of_ref[rows, hs] + ob_ref[rows, hs]
            y = _rms(o, gn_ref[...]) * _silu(gate_ref[rows, hs].astype(F32))
            o_ref[rows, hs] = y.astype(o_ref.dtype)
        return carry

    lax.fori_loop(0, n_dbl, finish, 0)


def _gla(p_main, p_small, wg, bg, gnorm, batch, seq_len):
    t = batch * seq_len
    kern = functools.partial(_gla_kernel, seq_len=seq_len)
    qk_blk = GLA_QK
    return pl.pallas_call(
        kern,
        grid=(batch,),
        in_specs=[
            pl.BlockSpec((seq_len, GLA_QK), lambda b: (b, 0)),
            pl.BlockSpec((seq_len, GLA_QK), lambda b: (b, 1)),
            pl.BlockSpec((seq_len, GLA_V), lambda b: (b, (2 * qk_blk) // GLA_V)),
            pl.BlockSpec((seq_len, GLA_V), lambda b: (b, (2 * qk_blk) // GLA_V + 1)),
            pl.BlockSpec((seq_len, SMALL_W), lambda b: (b, 0)),
            _resident(wg.shape), _resident(bg.shape), _resident(gnorm.shape),
        ],
        out_specs=pl.BlockSpec((seq_len, GLA_V), lambda b: (b, 0)),
        out_shape=jax.ShapeDtypeStruct((t, GLA_V), BF16),
        scratch_shapes=[
            pltpu.VMEM((seq_len, 2 * GLA_QK), F32),
            pltpu.VMEM((seq_len, GLA_V), F32),
            pltpu.VMEM((seq_len, GLA_V), F32),
            pltpu.VMEM((2, GLA_HEADS // 2, LANES, 2 * GLA_DV), F32),
        ],
        compiler_params=pltpu.CompilerParams(dimension_semantics=("arbitrary",),
                                             vmem_limit_bytes=VMEM_LIMIT),
        name="gla_mixer",
    )(p_main, p_main, p_main, p_main, p_small, wg, bg, gnorm)


def _sgu_out_kernel(h_ref, oa_ref, su_ref, sv_ref, lng_ref, lnb_ref, ws_ref, bs_ref, wo_ref, o_ref):
    tm = h_ref.shape[0]
    for rg in _row_groups(tm, tm):
        u = _gelu_tanh(su_ref[rg, :].astype(F32))
        g = _gelu_tanh(sv_ref[rg, :].astype(F32))
        mu = jnp.mean(g, axis=-1, keepdims=True)
        gc = g - mu
        var = jnp.mean(gc * gc, axis=-1, keepdims=True)
        vv = (gc * lax.rsqrt(var + NORM_EPS) * lng_ref[...] + lnb_ref[...]).astype(BF16)
        rows = []
        for c in range((rg.stop - rg.start) // SGU_CHUNK):
            rs = slice(c * SGU_CHUNK, (c + 1) * SGU_CHUNK)
            cols = []
            for gi in range(SGU_GROUPS):
                gs = slice(gi * SGU_GROUP_DIM, (gi + 1) * SGU_GROUP_DIM)
                cols.append(_dot(ws_ref[gi], vv[rs, gs]))
            rows.append(jnp.concatenate(cols, axis=1) + bs_ref[...])
        ob = (u * jnp.concatenate(rows, axis=0)).astype(BF16)
        acc = _dot(oa_ref[rg, :], wo_ref[:GLA_V, :]) + _dot(ob, wo_ref[GLA_V:, :])
        o_ref[rg, :] = h_ref[rg, :] + acc


def _sgu_out(h2d, o_a, p_main, ln_g, ln_b, w_s, b_full, w_out):
    t, d = h2d.shape
    su_blk = (2 * GLA_QK + 2 * GLA_V) // SGU_DIM
    return pl.pallas_call(
        _sgu_out_kernel,
        grid=(t // ROW_TILE,),
        in_specs=[
            pl.BlockSpec((ROW_TILE, d), lambda i: (i, 0)),
            pl.BlockSpec((ROW_TILE, GLA_V), lambda i: (i, 0)),
            pl.BlockSpec((ROW_TILE, SGU_DIM), lambda i: (i, su_blk)),
            pl.BlockSpec((ROW_TILE, SGU_DIM), lambda i: (i, su_blk + 1)),
            _resident(ln_g.shape), _resident(ln_b.shape), _resident(w_s.shape),
            _resident(b_full.shape), _resident(w_out.shape),
        ],
        out_specs=pl.BlockSpec((ROW_TILE, d), lambda i: (i, 0)),
        out_shape=jax.ShapeDtypeStruct((t, d), F32),
        compiler_params=pltpu.CompilerParams(dimension_semantics=("arbitrary",),
                                             vmem_limit_bytes=VMEM_LIMIT),
        name="sgu_out_proj",
    )(h2d, o_a, p_main, p_main, ln_g, ln_b, w_s, b_full, w_out)


GDN_SCALARS = 6
GDN_PAIRS_PER_STEP = 2
GDN_STEP_W = GDN_PAIRS_PER_STEP * 2 * GDN_DK
GDN_NORM_COL = 4 * GDN_HEADS
GDN_STEP_HEADS = 2 * GDN_PAIRS_PER_STEP
GDN_SCAN_UNROLL = 8
GDN_PRE_CHUNKS = 4


def _packed_product(x, y):
    lo = _iota(y.shape, 1) < CHUNK
    zero = jnp.zeros_like(y)
    return _dot(x, jnp.concatenate([jnp.where(lo, y, zero), jnp.where(lo, zero, y)], axis=0))


def _gdn_kernel(q_ref, k_ref, v_ref, sm_ref, tab_ref, esum_ref, eexp_ref, o_ref,
                tt_ref, aqd_ref, dec_ref, of_ref, ob_ref, st_ref, *, seq_len):
    n_chunks = seq_len // CHUNK
    blk = 2 * CHUNK
    npp = GDN_PAIRS_PER_STEP
    pw = 2 * GDN_DK

    pblk = GDN_PRE_CHUNKS * CHUNK
    tri = (_block_tri_ones(True, pblk), _block_tri_ones(False, pblk))
    incl = (_packed_tri_mask(True), _packed_tri_mask(False))
    strict = (_packed_tri_mask(True, strict=True), _packed_tri_mask(False, strict=True))
    diag = _iota((CHUNK, LANES), 0) == (_iota((CHUNK, LANES), 1) & (CHUNK - 1))
    lo_half = _iota((SUBLANES_F32, LANES), 1) < CHUNK
    eye = jnp.where(diag, 1.0, 0.0)

    def row_form(col_form):
        return jnp.sum(jnp.where(diag, col_form, 0.0), axis=0, keepdims=True)

    def precompute(it, carry):
        rows = pl.ds(pl.multiple_of(it * pblk, pblk), pblk)
        q2 = q_ref[rows, :]
        k2 = k_ref[rows, :]
        sm = sm_ref[rows, :]
        a_exp = jnp.exp(tab_ref[0:1, :])
        dt_b = tab_ref[1:2, :]

        lane = _iota(sm.shape, 1)
        gates = jnp.where(lane < 2 * GDN_HEADS, _sigmoid(sm), -a_exp * _softplus(sm + dt_b))
        qf = q2.astype(F32)
        kf = k2.astype(F32)
        ssq = _dot((qf * qf).astype(BF16), esum_ref[0]) + _dot((kf * kf).astype(BF16), esum_ref[1])
        inv = lax.rsqrt(ssq + NORM_EPS)
        is_qn = jnp.abs(2 * lane - (2 * GDN_NORM_COL + GDN_STEP_HEADS - 1)) < GDN_STEP_HEADS
        inv = jnp.where(is_qn, inv * (GDN_DK ** -0.5), inv)
        table = jnp.where(lane < GDN_NORM_COL, gates, inv)
        cum_f = _dot_exact_lhs(tri[0], table)
        cum_b = _dot_exact_lhs(tri[1], table)
        band = lane // GDN_HEADS
        table = jnp.where(band == 2, cum_f, jnp.where(band == 3, cum_b, table))
        hi, lo = _split(table)
        n_cum = 2 * npp * LANES
        e_cum = eexp_ref[0, :, :n_cum]
        ex_cum = _dot(hi, e_cum) + _dot(lo, e_cum)
        ex_rest = _dot(hi, eexp_ref[0, :, n_cum:])

        chains = []
        for cc in range(GDN_PRE_CHUNKS):
            rs = slice(cc * CHUNK, (cc + 1) * CHUNK)
            for pp in range(npp):
                ps = slice(pp * pw, (pp + 1) * pw)
                cols = [ex_cum[rs, (2 * pp + j) * LANES:(2 * pp + j + 1) * LANES] for j in range(2)]
                cols += [ex_rest[rs, (4 * pp + j) * LANES:(4 * pp + j + 1) * LANES] for j in range(4)]
                rq_c = cols[4]
                rk_c = cols[5]
                rk_r = row_form(rk_c)
                kbd = _pair_blockdiag(kf[rs, ps]).astype(BF16)
                gram = _dot_nt(jnp.concatenate([k2[rs, ps], q2[rs, ps]], axis=0), kbd)
                kk = gram[:CHUNK] * rk_c * rk_r
                qk = gram[CHUNK:] * rq_c * rk_r
                for dirn in range(2):
                    cum_c = cols[dirn]
                    beta_c = cols[2 + dirn]
                    tot_r = cum_c[CHUNK - 1:CHUNK, :] if dirn == 0 else cum_c[0:1, :]
                    cum_r = row_form(cum_c)
                    beta_r = row_form(beta_c)
                    decay = jnp.exp(jnp.where(incl[dirn], cum_c - cum_r, -1e30))
                    a = jnp.where(strict[dirn], kk * beta_c * decay, 0.0)
                    ci = GDN_PRE_CHUNKS * it + cc
                    d_q = jnp.where(diag, rq_c * jnp.exp(cum_c), 0.0)
                    aqd_ref[ci, dirn, pp] = jnp.concatenate([qk * decay, d_q], axis=1).astype(BF16)
                    tot8 = jnp.broadcast_to(tot_r, (SUBLANES_F32, LANES))
                    tot8r = pltpu.roll(tot8, CHUNK, axis=1)
                    dec_ref[ci, dirn, pp] = jnp.exp(jnp.concatenate(
                        [jnp.where(lo_half, tot8, tot8r), jnp.where(lo_half, tot8r, tot8)], axis=1))
                    chains.append(dict(ci=ci, pp=pp, dirn=dirn, pw_a=a, inv_m=eye - a,
                                       scale_u=beta_r, scale_w=beta_r * jnp.exp(cum_r) * rk_r,
                                       e_c=rk_c * jnp.exp(tot_r - cum_c)))
        for ch in chains:
            a_bf = ch["pw_a"].astype(BF16)
            ch["pw_a"] = _packed_product(a_bf, a_bf).astype(BF16)
        for _ in range(4):
            for ch in chains:
                lhs = jnp.concatenate([ch["inv_m"].astype(BF16), ch["pw_a"]], axis=0)
                both = _packed_product(lhs, ch["pw_a"])
                ch["inv_m"] = ch["inv_m"] + both[:CHUNK]
                ch["pw_a"] = both[CHUNK:].astype(BF16)
        for ch in chains:
            ch["inv_m"] = ch["inv_m"] + _packed_product(ch["inv_m"].astype(BF16), ch["pw_a"])
        for ch in chains:
            t_u = ch["inv_m"] * ch["scale_u"]
            t_w = ch["inv_m"] * ch["scale_w"]
            e_c = ch["e_c"]
            tt_ref[ch["ci"], ch["dirn"], ch["pp"]] = jnp.concatenate(
                [t_u, t_u * e_c, t_w, t_w * e_c], axis=0).astype(BF16)
        return carry

    lax.fori_loop(0, seq_len // pblk, precompute, 0)
    st_ref[...] = jnp.zeros(st_ref.shape, F32)

    o_refs = (of_ref, ob_ref)

    def scan(it, carry):
        heads = (slice(0, GDN_DV), slice(GDN_DV, 2 * GDN_DV))
        steps = []
        for sub in range(GDN_SCAN_UNROLL):
            cf = GDN_SCAN_UNROLL * it + sub
            cis = (cf, n_chunks - 1 - cf)
            chains = []
            for dirn in range(2):
                ci = cis[dirn]
                rows = pl.ds(pl.multiple_of(ci * CHUNK, CHUNK), CHUNK)
                q_c, k_c, v_c = q_ref[rows, :], k_ref[rows, :], v_ref[rows, :]
                for pp in range(npp):
                    ps = slice(pp * pw, (pp + 1) * pw)
                    chains.append(dict(dirn=dirn, pp=pp, rows=rows, tt=tt_ref[ci, dirn, pp],
                                       aqd=aqd_ref[ci, dirn, pp], dec=dec_ref[ci, dirn, pp],
                                       q=q_c[:, ps], k=k_c[:, ps], v=v_c[:, ps]))
            steps.append(chains)
        states = {(dirn, pp): [st_ref[dirn, pp, j] for j in range(2)]
                  for dirn in range(2) for pp in range(npp)}
        for chains in steps:
            for ch in chains:
                kbd = _pair_blockdiag(ch["k"].astype(F32)).astype(BF16)
                vbd = _pair_blockdiag(ch["v"].astype(F32)).astype(BF16)
                ch["uu"] = _dot(ch["tt"][:blk], vbd)
                ch["ww"] = _dot(ch["tt"][blk:], kbd)
        for chains in steps:
            for ch in chains:
                state = states[(ch["dirn"], ch["pp"])]
                lhs = jnp.concatenate([ch["ww"].astype(BF16), ch["q"]], axis=0)
                ch["prod"] = jnp.concatenate(
                    [_dot(lhs[:, hs], state[j].astype(BF16)) for j, hs in enumerate(heads)], axis=1)
            for ch in chains:
                state = states[(ch["dirn"], ch["pp"])]
                v_new_e = (ch["uu"][CHUNK:] - ch["prod"][CHUNK:blk]).astype(BF16)
                states[(ch["dirn"], ch["pp"])] = [
                    state[j] * ch["dec"][0:1, hs] + _dot_tn(ch["k"][:, hs], v_new_e[:, hs])
                    for j, hs in enumerate(heads)]
            for ch in chains:
                v_new = ch["uu"][:CHUNK] - ch["prod"][:CHUNK]
                rhs = jnp.concatenate([_pair_blockdiag(v_new), _pair_blockdiag(ch["prod"][blk:])], axis=0)
                ch["o"] = _dot(ch["aqd"], rhs.astype(BF16))
        for chains in steps:
            for dirn in range(2):
                mine = [ch for ch in chains if ch["dirn"] == dirn]
                o_refs[dirn][mine[0]["rows"], :] = jnp.concatenate([ch["o"] for ch in mine], axis=1)
        for (dirn, pp), state in states.items():
            for j in range(2):
                st_ref[dirn, pp, j] = state[j]
        return carry

    lax.fori_loop(0, n_chunks // GDN_SCAN_UNROLL, scan, 0)

    def finish(rb, carry):
        r0 = pl.multiple_of(rb * blk, blk)
        rows = pl.ds(r0, blk)
        o_ref[rows, :] = (of_ref[rows, :] + ob_ref[rows, :]).astype(o_ref.dtype)
        return carry

    lax.fori_loop(0, seq_len // blk, finish, 0)


def _gdn(p_main, p_small, tab, esum, eexp, batch, seq_len):
    t = batch * seq_len
    n_steps = GDN_QK // GDN_STEP_W
    n_chunks = seq_len // CHUNK
    npp = GDN_PAIRS_PER_STEP
    pw = 2 * GDN_DK
    sw = GDN_STEP_W
    kern = functools.partial(_gdn_kernel, seq_len=seq_len)
    return pl.pallas_call(
        kern,
        grid=(batch, n_steps),
        in_specs=[
            pl.BlockSpec((seq_len, sw), lambda b, p: (b, p)),
            pl.BlockSpec((seq_len, sw), lambda b, p: (b, GDN_QK // sw + p)),
            pl.BlockSpec((seq_len, sw), lambda b, p: (b, 2 * GDN_QK // sw + p)),
            pl.BlockSpec((seq_len, SMALL_W), lambda b, p: (b, 0)),
            pl.BlockSpec(tab.shape, lambda b, p: (0, 0)),
            pl.BlockSpec(esum.shape, lambda b, p: (0, 0, 0)),
            pl.BlockSpec((1, SMALL_W, npp * GDN_SCALARS * LANES), lambda b, p: (p, 0, 0)),
        ],
        out_specs=pl.BlockSpec((seq_len, sw), lambda b, p: (b, p)),
        out_shape=jax.ShapeDtypeStruct((t, GDN_V), BF16),
        scratch_shapes=[
            pltpu.VMEM((n_chunks, 2, npp, 4 * CHUNK, LANES), BF16),
            pltpu.VMEM((n_chunks, 2, npp, CHUNK, 2 * LANES), BF16),
            pltpu.VMEM((n_chunks, 2, npp, SUBLANES_F32, pw), F32),
            pltpu.VMEM((seq_len, sw), F32),
            pltpu.VMEM((seq_len, sw), F32),
            pltpu.VMEM((2, npp, 2, GDN_DK, GDN_DV), F32),
        ],
        compiler_params=pltpu.CompilerParams(dimension_semantics=("arbitrary", "arbitrary"),
                                             vmem_limit_bytes=VMEM_LIMIT),
        name="gdn_mixer",
    )(p_main, p_main, p_main, p_small, tab, esum, eexp)


def _pad_cols(w, width):
    return jnp.pad(w, ((0, 0), (0, width - w.shape[1])))


def _gla_params(ab_w_in, w_gate_fwd, b_gate_fwd, w_gate_bwd, b_gate_bwd):
    n_wide = 2 * GLA_QK + 2 * GLA_V
    lr0 = n_wide
    sg0 = lr0 + 2 * GLA_LOWRANK
    w_all = ab_w_in.astype(BF16)
    small = _pad_cols(w_all[:, lr0:sg0], SMALL_W)
    w = jnp.concatenate([w_all[:, :n_wide], w_all[:, sg0:sg0 + 2 * SGU_DIM]], axis=1)
    wg = jnp.zeros((SMALL_W, 2 * GLA_QK), F32)
    wg = wg.at[:GLA_LOWRANK, :GLA_QK].set(w_gate_fwd)
    wg = wg.at[GLA_LOWRANK:2 * GLA_LOWRANK, GLA_QK:].set(w_gate_bwd)
    bg = jnp.concatenate([b_gate_fwd, b_gate_bwd])[None, :]
    return w, small, wg.astype(BF16), bg


def _gdn_params(gdn_w_in, a_log_fwd, dt_bias_fwd, a_log_bwd, dt_bias_bwd):
    n_main = GDN_CONV_DIM + GDN_V
    w = gdn_w_in.astype(BF16)
    small = _pad_cols(w[:, n_main:], SMALL_W)
    pad = SMALL_W - 4 * GDN_HEADS
    zeros2 = jnp.zeros((2 * GDN_HEADS,), F32)
    a_log = jnp.concatenate([zeros2, a_log_fwd, a_log_bwd, jnp.zeros((pad,), F32)])
    dt_b = jnp.concatenate([zeros2, dt_bias_fwd, dt_bias_bwd, jnp.zeros((pad,), F32)])
    tab = jnp.zeros((SUBLANES_F32, SMALL_W), F32).at[0].set(a_log).at[1].set(dt_b)
    ch_head = jnp.arange(GDN_STEP_W) // GDN_DK
    col = jnp.arange(SMALL_W)
    esum_q = (col[None, :] == (GDN_NORM_COL + ch_head)[:, None])
    esum_k = (col[None, :] == (GDN_NORM_COL + GDN_STEP_HEADS + ch_head)[:, None])
    esum = jnp.stack([esum_q, esum_k]).astype(BF16)
    n_steps = GDN_QK // GDN_STEP_W
    lane = jnp.arange(GDN_PAIRS_PER_STEP * GDN_SCALARS * LANES)
    blk = lane // LANES
    n_cum_blk = 2 * GDN_PAIRS_PER_STEP
    pair = jnp.where(blk < n_cum_blk, blk // 2, (blk - n_cum_blk) // 4)
    quant = jnp.where(blk < n_cum_blk, blk % 2, 2 + (blk - n_cum_blk) % 4)
    local_head = 2 * pair + (lane % LANES) // CHUNK
    head = GDN_STEP_HEADS * jnp.arange(n_steps)[:, None] + local_head[None, :]
    src_gate = jnp.array([2 * GDN_HEADS, 3 * GDN_HEADS, 0, GDN_HEADS])
    src_norm = GDN_NORM_COL + GDN_STEP_HEADS * (quant - 4) + local_head
    src = jnp.where(quant[None, :] < 4, src_gate[jnp.minimum(quant, 3)][None, :] + head,
                    src_norm[None, :])
    eexp = (col[None, :, None] == src[:, None, :]).astype(BF16)
    return w, small, tab, esum, eexp


def kernel(x, norm_mix, norm_ffn, norm_final, ab_w_in, gla_w_gate_fwd, gla_b_gate_fwd, gla_w_gate_bwd, gla_b_gate_bwd, gla_norm, sgu_ln_g, sgu_ln_b, sgu_w_s, sgu_b_s, ab_w_out, gdn_w_in, gdn_conv_w, gdn_a_log_fwd, gdn_dt_bias_fwd, gdn_a_log_bwd, gdn_dt_bias_bwd, gdn_norm, gdn_w_out, ffn_w_up, ffn_conv_w, ffn_conv_b, ffn_w_down):
    batch, seq_len, d = x.shape
    t = batch * seq_len
    assert seq_len % ROW_TILE == 0 and seq_len % (2 * CHUNK) == 0
    h = x.reshape(t, d)

    w0, w0_small, wg, bg = _gla_params(ab_w_in[0], gla_w_gate_fwd[0], gla_b_gate_fwd[0],
                                       gla_w_gate_bwd[0], gla_b_gate_bwd[0])
    n_main0 = 2 * GLA_QK + 2 * GLA_V + 2 * SGU_DIM
    p0, s0 = _norm_proj(h, norm_mix[0][None, :], w0, w0_small, None, seq_len, n_main0, 0,
                        "gla_sgu_in_proj")
    o_a = _gla(p0, s0, wg, bg, gla_norm[0][None, :], batch, seq_len)
    b_full = jnp.repeat(sgu_b_s[0].T, SGU_GROUP_DIM, axis=1)
    h = _sgu_out(h, o_a, p0, sgu_ln_g[0][None, :], sgu_ln_b[0][None, :],
                 sgu_w_s[0].astype(BF16), b_full, ab_w_out[0].astype(BF16))
    w_up_all = ffn_w_up.astype(BF16)
    w_down_all = ffn_w_down.astype(BF16)
    h = _ffn(h, norm_ffn[0][None, :], w_up_all, ffn_conv_w[0], ffn_conv_b[0][None, :], w_down_all,
             0, seq_len, None, "ffn0")

    w1, w1_small, tab, esum, eexp = _gdn_params(gdn_w_in[0], gdn_a_log_fwd[0], gdn_dt_bias_fwd[0],
                                                gdn_a_log_bwd[0], gdn_dt_bias_bwd[0])
    n_main1 = GDN_CONV_DIM + GDN_V
    p1, s1 = _norm_proj(h, norm_mix[1][None, :], w1, w1_small, gdn_conv_w[0], seq_len, n_main1,
                        GDN_CONV_DIM, "gdn_in_proj")
    o_g = _gdn(p1, s1, tab, esum, eexp, batch, seq_len)
    h = _ffn(h, norm_ffn[1][None, :], w_up_all, ffn_conv_w[1], ffn_conv_b[1][None, :], w_down_all,
             1, seq_len, norm_final[None, :], "gdn_out_ffn1",
             gdn_out=(o_g, p1, gdn_norm[0][None, :], gdn_w_out[0].astype(BF16)))
    return h.reshape(batch, seq_len, d)
```

```python
import functools

import jax
import jax.numpy as jnp
from jax import lax
from jax.experimental import pallas as pl
from jax.experimental.pallas import tpu as pltpu

F32 = jnp.float32
BF16 = jnp.bfloat16

NORM_EPS = 1e-6
GLA_HEADS = 4
GLA_DK = 64
GLA_DV = 128
GLA_QK = GLA_HEADS * GLA_DK
GLA_V = GLA_HEADS * GLA_DV
GLA_LOWRANK = 16
GLA_GATE_NORMALIZER = 16.0
SGU_GROUPS = 4
SGU_GROUP_DIM = 128
SGU_DIM = SGU_GROUPS * SGU_GROUP_DIM
SGU_CHUNK = 128
GDN_HEADS = 8
GDN_DK = 128
GDN_DV = 128
GDN_QK = GDN_HEADS * GDN_DK
GDN_V = GDN_HEADS * GDN_DV
GDN_CONV_DIM = 2 * GDN_QK + GDN_V
FFN_DIM = 2816

LANES = 128
SUBLANES_F32 = 8
CHUNK = 64
SMALL_W = LANES

ROW_TILE = 1024
COL_TILE = 256
ROW_GROUPS = 4
HALO = SUBLANES_F32
HALO_BF16 = 2 * SUBLANES_F32
VMEM_LIMIT = 56 * 1024 * 1024
GLA_PREP_ROWS = 512
GLA_UNROLL = 4


def _dot(a, b):
    return jnp.dot(a, b, preferred_element_type=F32)


def _dot_nt(a, b):
    return lax.dot_general(a, b, (((1,), (1,)), ((), ())), preferred_element_type=F32)


def _dot_tn(a, b):
    return lax.dot_general(a, b, (((0,), (0,)), ((), ())), preferred_element_type=F32)


def _split(a):
    hi = a.astype(BF16)
    lo = (a - hi.astype(F32)).astype(BF16)
    return hi, lo


def _dot_exact_lhs(l_bf16, a):
    hi, lo = _split(a)
    return _dot(l_bf16, hi) + _dot(l_bf16, lo)


def _rms(x, gain):
    ms = jnp.mean(x * x, axis=-1, keepdims=True)
    return x * lax.rsqrt(ms + NORM_EPS) * gain


def _sigmoid(x):
    return 1.0 / (1.0 + jnp.exp(-x))


def _silu(x):
    return x * _sigmoid(x)


def _softplus(x):
    return jnp.maximum(x, 0.0) + jnp.log(1.0 + jnp.exp(-jnp.abs(x)))


def _gelu_tanh(x):
    c = 0.7978845608028654
    return 0.5 * x * (1.0 + jnp.tanh(c * (x + 0.044715 * (x * x * x))))


def _iota(shape, dim):
    return lax.broadcasted_iota(jnp.int32, shape, dim)


def _shift_rows(g, first_row, last_row):
    n = g.shape[0]
    row = _iota(g.shape, 0)
    g_prev = jnp.where(row == 0, first_row, pltpu.roll(g, 1, axis=0))
    g_next = jnp.where(row == n - 1, last_row, pltpu.roll(g, n - 1, axis=0))
    return g_prev, g_next


def _halo_rows(gh, tiles_per_seq):
    i = pl.program_id(0)
    pos = i % tiles_per_seq
    keep_prev = (pos != 0).astype(F32)
    keep_next = (pos != tiles_per_seq - 1).astype(F32)
    prev_row = gh[HALO - 1:HALO, :] * keep_prev
    next_row = gh[HALO:HALO + 1, :] * keep_next
    return prev_row, next_row


def _normed_with_halo(x_ref, xp_ref, xn_ref, g_ref):
    g = g_ref[...]
    hn = _rms(x_ref[...], g).astype(BF16)
    halo = jnp.concatenate([xp_ref[...], xn_ref[...]], axis=0)
    hh = _rms(halo, g).astype(BF16)
    return hn, jnp.concatenate([hn, hh], axis=0)


def _row_groups(tm, n_rows):
    step = tm // ROW_GROUPS
    edges = [g * step for g in range(ROW_GROUPS)] + [n_rows]
    return [slice(edges[g], edges[g + 1]) for g in range(ROW_GROUPS)]


def _norm_and_first_dot(x_ext, gain, tm, w_first, w_first_main=None):
    hs, accs, mains = [], [], []
    for rs in _row_groups(tm, x_ext.shape[0]):
        h = _rms(x_ext[rs], gain).astype(BF16)
        hs.append(h)
        accs.append(_dot(h, w_first))
        if w_first_main is not None:
            mains.append(_dot(h[:min(rs.stop, tm) - rs.start], w_first_main))
    main = jnp.concatenate(mains, axis=0) if mains else None
    return jnp.concatenate(hs, axis=0), jnp.concatenate(accs, axis=0), main


def _norm_proj_kernel(*refs, n_main, conv_cols, tiles_per_seq):
    if conv_cols:
        x_ref, xp_ref, xn_ref, g_ref, w_ref, ws_ref, cw_ref, o_ref, s_ref = refs
        hn, hx = _normed_with_halo(x_ref, xp_ref, xn_ref, g_ref)
    else:
        x_ref, g_ref, w_ref, ws_ref, o_ref, s_ref = refs
        hn = _rms(x_ref[...], g_ref[...]).astype(BF16)
    tm = hn.shape[0]
    for c in range(0, n_main, COL_TILE):
        cs = slice(c, c + COL_TILE)
        if c < conv_cols:
            acc = _dot(hx, w_ref[:, cs])
            gp = acc[:tm]
            prev_row, next_row = _halo_rows(acc[tm:], tiles_per_seq)
            g_prev, g_next = _shift_rows(gp, prev_row, next_row)
            cw = cw_ref[:, cs]
            y = cw[0:1] * g_prev + cw[1:2] * gp + cw[2:3] * g_next
            o_ref[:, cs] = _silu(y).astype(o_ref.dtype)
        else:
            o_ref[:, cs] = _dot(hn, w_ref[:, cs]).astype(o_ref.dtype)
    s_ref[...] = _dot(hn, ws_ref[...])


def _halo_specs(d, n_rows, halo, col_blk=0):
    blocks_per_tile = ROW_TILE // halo
    last = n_rows // halo - 1
    prev = pl.BlockSpec((halo, d), lambda i: (jnp.maximum(i * blocks_per_tile - 1, 0), col_blk))
    nxt = pl.BlockSpec((halo, d), lambda i: (jnp.minimum((i + 1) * blocks_per_tile, last), col_blk))
    return prev, nxt


def _resident(shape):
    return pl.BlockSpec(shape, lambda i: (0,) * len(shape), pipeline_mode=pl.Buffered(1))


def _norm_proj(x2d, gain, w, w_small, conv_w, seq_len, n_main, conv_cols, name):
    t, d = x2d.shape
    row = pl.BlockSpec((ROW_TILE, d), lambda i: (i, 0))
    in_specs = [row]
    args = [x2d]
    if conv_cols:
        prev, nxt = _halo_specs(d, t, HALO)
        in_specs += [prev, nxt]
        args += [x2d, x2d]
    in_specs += [_resident((1, d)), _resident(w.shape), _resident(w_small.shape)]
    args += [gain, w, w_small]
    if conv_cols:
        in_specs.append(_resident(conv_w.shape))
        args.append(conv_w)
    kern = functools.partial(_norm_proj_kernel, n_main=n_main, conv_cols=conv_cols,
                             tiles_per_seq=seq_len // ROW_TILE)
    return pl.pallas_call(
        kern,
        grid=(t // ROW_TILE,),
        in_specs=in_specs,
        out_specs=[pl.BlockSpec((ROW_TILE, n_main), lambda i: (i, 0)),
                   pl.BlockSpec((ROW_TILE, SMALL_W), lambda i: (i, 0))],
        out_shape=[jax.ShapeDtypeStruct((t, n_main), BF16),
                   jax.ShapeDtypeStruct((t, SMALL_W), F32)],
        compiler_params=pltpu.CompilerParams(dimension_semantics=("arbitrary",),
                                             vmem_limit_bytes=VMEM_LIMIT),
        name=name,
    )(*args)


def _with_halo_bf16(m_ref, mp_ref, mn_ref):
    return jnp.concatenate([m_ref[...].astype(F32), mp_ref[...].astype(F32)[HALO_BF16 - HALO:],
                            mn_ref[...].astype(F32)[:HALO]], axis=0)


def _ffn_kernel(*refs, tiles_per_seq, final_norm, gdn_out, add_in):
    refs = list(refs)
    x_ref, xp_ref, xn_ref = refs[:3]
    del refs[:3]
    x_ext = jnp.concatenate([x_ref[...], xp_ref[...], xn_ref[...]], axis=0)
    tm = x_ref.shape[0]
    if add_in:
        a_ref, ap_ref, an_ref = refs[:3]
        del refs[:3]
        x_ext = x_ext + jnp.concatenate([a_ref[...], ap_ref[...], an_ref[...]], axis=0)
    if gdn_out:
        m_ref, mp_ref, mn_ref, z_ref, zp_ref, zn_ref, gn_ref, wo_ref = refs[:8]
        del refs[:8]
        m_ext = _with_halo_bf16(m_ref, mp_ref, mn_ref)
        z_ext = _with_halo_bf16(z_ref, zp_ref, zn_ref)
        x_parts = []
        for rs in _row_groups(tm, x_ext.shape[0]):
            parts = []
            for h in range(GDN_HEADS):
                hs = slice(h * GDN_DV, (h + 1) * GDN_DV)
                parts.append((_rms(m_ext[rs, hs], gn_ref[...]) * _silu(z_ext[rs, hs])).astype(BF16))
            x_parts.append(x_ext[rs] + _dot(jnp.concatenate(parts, axis=1), wo_ref[...]))
        x_ext = jnp.concatenate(x_parts, axis=0)
    g_ref, wup_ref, cw_ref, cb_ref, wdn_ref = refs[:5]
    del refs[:5]
    if final_norm:
        gf_ref = refs.pop(0)
    o_ref, act_ref = refs
    hx, acc0, up0 = _norm_and_first_dot(x_ext, g_ref[...], tm, wup_ref[:, :COL_TILE],
                                        wup_ref[:, FFN_DIM:FFN_DIM + COL_TILE])
    hn = hx[:tm]
    for c in range(0, FFN_DIM, COL_TILE):
        cs = slice(c, c + COL_TILE)
        acc = acc0 if c == 0 else _dot(hx, wup_ref[:, cs])
        up = up0 if c == 0 else _dot(hn, wup_ref[:, FFN_DIM + c:FFN_DIM + c + COL_TILE])
        gp = acc[:tm]
        prev_row, next_row = _halo_rows(acc[tm:], tiles_per_seq)
        g_prev, g_next = _shift_rows(gp, prev_row, next_row)
        cw = cw_ref[:, cs]
        gate = cw[0:1] * g_prev + cw[1:2] * gp + cw[2:3] * g_next + cb_ref[:, cs]
        act_ref[:, cs] = (_silu(gate) * up).astype(BF16)
    for rs in _row_groups(tm, tm):
        out = x_ext[rs] + _dot(act_ref[rs, :], wdn_ref[...])
        if final_norm:
            out = _rms(out, gf_ref[...])
        o_ref[rs, :] = out


def _layer_resident(stacked, layer):
    shape = stacked.shape[1:]
    return pl.BlockSpec((None,) + shape, lambda i: (layer,) + (0,) * len(shape),
                        pipeline_mode=pl.Buffered(1))


def _ffn(h2d, gain, w_up, conv_w, conv_b, w_down, layer, seq_len, final_gain, name, gdn_out=None,
         add_in=None):
    t, d = h2d.shape
    row = pl.BlockSpec((ROW_TILE, d), lambda i: (i, 0))
    prev, nxt = _halo_specs(d, t, HALO)
    in_specs = [row, prev, nxt]
    args = [h2d, h2d, h2d]
    if add_in is not None:
        in_specs += [row, prev, nxt]
        args += [add_in, add_in, add_in]
    if gdn_out is not None:
        o_gdn, p_main, gnorm, w_out = gdn_out
        z_blk = GDN_CONV_DIM // GDN_V
        for arr, cb in ((o_gdn, 0), (p_main, z_blk)):
            prev_b, nxt_b = _halo_specs(GDN_V, t, HALO_BF16, cb)
            in_specs += [pl.BlockSpec((ROW_TILE, GDN_V), lambda i, cb=cb: (i, cb)), prev_b, nxt_b]
            args += [arr, arr, arr]
        in_specs += [_resident(gnorm.shape), _resident(w_out.shape)]
        args += [gnorm, w_out]
    in_specs += [_resident((1, d)), _layer_resident(w_up, layer), _resident(conv_w.shape),
                 _resident(conv_b.shape), _layer_resident(w_down, layer)]
    args += [gain, w_up, conv_w, conv_b, w_down]
    if final_gain is not None:
        in_specs.append(_resident((1, d)))
        args.append(final_gain)
    kern = functools.partial(_ffn_kernel, tiles_per_seq=seq_len // ROW_TILE,
                             final_norm=final_gain is not None, gdn_out=gdn_out is not None,
                             add_in=add_in is not None)
    return pl.pallas_call(
        kern,
        grid=(t // ROW_TILE,),
        in_specs=in_specs,
        out_specs=row,
        out_shape=jax.ShapeDtypeStruct((t, d), F32),
        scratch_shapes=[pltpu.VMEM((ROW_TILE, FFN_DIM), BF16)],
        compiler_params=pltpu.CompilerParams(dimension_semantics=("arbitrary",),
                                             vmem_limit_bytes=VMEM_LIMIT),
        name=name,
    )(*args)


def _pair_rows(x):
    lo = _iota(x.shape, 1) < CHUNK
    return jnp.concatenate([jnp.where(lo, x, 0.0), jnp.where(lo, 0.0, x)], axis=0)


def _pair_blockdiag(x):
    lo = _iota(x.shape, 1) < LANES
    return jnp.concatenate([jnp.where(lo, x, 0.0), jnp.where(lo, 0.0, x)], axis=0)


def _block_tri_ones(lower, n=2 * CHUNK):
    r = _iota((n, n), 0)
    c = _iota((n, n), 1)
    tri = jnp.where((c <= r) if lower else (c >= r), 1.0, 0.0)
    return jnp.where((r // CHUNK) == (c // CHUNK), tri, 0.0).astype(BF16)


def _packed_tri_mask(lower, strict=False):
    r = _iota((CHUNK, LANES), 0)
    c = _iota((CHUNK, LANES), 1) & (CHUNK - 1)
    if lower:
        return (c < r) if strict else (c <= r)
    return (c > r) if strict else (c >= r)


def _gla_kernel(q_ref, k_ref, v_ref, gate_ref, lr_ref, wg_ref, bg_ref, gn_ref, o_ref,
                la_ref, of_ref, ob_ref, st_ref, *, seq_len):
    n_pairs = GLA_HEADS // 2
    n_dbl = seq_len // (2 * CHUNK)
    blk = 2 * CHUNK

    def prep(rb, carry):
        rows = pl.ds(pl.multiple_of(rb * GLA_PREP_ROWS, GLA_PREP_ROWS), GLA_PREP_ROWS)
        z = _dot(lr_ref[rows, :].astype(BF16), wg_ref[...]) + bg_ref[...]
        log_sig = jnp.minimum(z, 0.0) - jnp.log(1.0 + jnp.exp(-jnp.abs(z)))
        la_ref[rows, :] = log_sig * (1.0 / GLA_GATE_NORMALIZER)
        return carry

    lax.fori_loop(0, seq_len // GLA_PREP_ROWS, prep, 0)
    st_ref[...] = jnp.zeros(st_ref.shape, F32)

    tri = (_block_tri_ones(True), _block_tri_ones(False))
    masks = (_packed_tri_mask(True), _packed_tri_mask(False))
    st_mask = (_iota((LANES, 2 * GLA_DV), 0) // GLA_DK) == (_iota((LANES, 2 * GLA_DV), 1) // GLA_DV)
    first_half = _iota((blk, GLA_QK), 0) < CHUNK
    o_refs = (of_ref, ob_ref)

    def body(it, carry):
        steps = []
        for sub in range(GLA_UNROLL):
            df = GLA_UNROLL * it + sub
            dbs = (df, n_dbl - 1 - df)
            chains = []
            for dirn in range(2):
                rows = pl.ds(pl.multiple_of(dbs[dirn] * blk, blk), blk)
                la = la_ref[rows, dirn * GLA_QK:(dirn + 1) * GLA_QK]
                q2, k2, v2 = q_ref[rows, :], k_ref[rows, :], v_ref[rows, :]
                cum = _dot_exact_lhs(tri[dirn], la)
                if dirn == 0:
                    tots = (cum[CHUNK - 1:CHUNK, :], cum[blk - 1:blk, :])
                else:
                    tots = (cum[0:1, :], cum[CHUNK:CHUNK + 1, :])
                tot_rows = jnp.where(first_half, tots[0], tots[1])
                q = q2.astype(F32) * (GLA_DK ** -0.5)
                k = k2.astype(F32)
                q_dec = (q * jnp.exp(cum)).astype(BF16)
                k_inv = k * jnp.exp(-cum)
                k_end = k * jnp.exp(tot_rows - cum)
                v32 = v2.astype(F32)
                for p in range(n_pairs):
                    ls = slice(p * LANES, (p + 1) * LANES)
                    vs = slice(p * 2 * GLA_DV, (p + 1) * 2 * GLA_DV)
                    dec = [jnp.broadcast_to(jnp.exp(tots[cc][:, ls]), (LANES, LANES)).T
                           for cc in range(2)]
                    chains.append(dict(dirn=dirn, p=p, rows=rows,
                                       order=(0, 1) if dirn == 0 else (1, 0),
                                       qd=q_dec[:, ls], k_inv=k_inv[:, ls],
                                       k_end=k_end[:, ls].astype(BF16),
                                       v=v32[:, vs], vb=v2[:, vs],
                                       dec=[jnp.concatenate([d, d], axis=1) for d in dec],
                                       o=[None, None]))
            steps.append(chains)
        states = {(dirn, p): st_ref[dirn, p] for dirn in range(2) for p in range(n_pairs)}
        for chains in steps:
            for ch in chains:
                for cc in range(2):
                    rs = slice(cc * CHUNK, (cc + 1) * CHUNK)
                    kbd = _pair_rows(ch["k_inv"][rs]).astype(BF16)
                    sc = jnp.where(masks[ch["dirn"]], _dot_nt(ch["qd"][rs], kbd), 0.0).astype(BF16)
                    ch["o"][cc] = _dot(sc, _pair_blockdiag(ch["v"][rs]).astype(BF16))
        for chains in steps:
            for step in range(2):
                for ch in chains:
                    cc = ch["order"][step]
                    rs = slice(cc * CHUNK, (cc + 1) * CHUNK)
                    state = states[(ch["dirn"], ch["p"])]
                    ch["o"][cc] = ch["o"][cc] + _dot(ch["qd"][rs], state.astype(BF16))
                    upd = _dot_tn(ch["k_end"][rs], ch["vb"][rs])
                    states[(ch["dirn"], ch["p"])] = state * ch["dec"][cc] + jnp.where(st_mask, upd, 0.0)
        for chains in steps:
            for dirn in range(2):
                mine = [ch for ch in chains if ch["dirn"] == dirn]
                o_refs[dirn][mine[0]["rows"], :] = jnp.concatenate(
                    [jnp.concatenate(ch["o"], axis=0) for ch in mine], axis=1)
        for (dirn, p), state in states.items():
            st_ref[dirn, p] = state
        return carry

    lax.fori_loop(0, n_dbl // GLA_UNROLL, body, 0)

    def finish(rb, carry):
        r0 = pl.multiple_of(rb * blk, blk)
        rows = pl.ds(r0, blk)
        for h in range(GLA_HEADS):
            hs = slice(h * GLA_DV, (h + 1) * GLA_DV)
            o = of_ref[rows, hs] + ob_ref[rows, hs]
            y = _rms(o, gn_ref[...]) * _silu(gate_ref[rows, hs].astype(F32))
            o_ref[rows, hs] = y.astype(o_ref.dtype)
        return carry

    lax.fori_loop(0, n_dbl, finish, 0)


def _gla(p_main, p_small, wg, bg, gnorm, batch, seq_len):
    t = batch * seq_len
    kern = functools.partial(_gla_kernel, seq_len=seq_len)
    qk_blk = GLA_QK
    return pl.pallas_call(
        kern,
        grid=(batch,),
        in_specs=[
            pl.BlockSpec((seq_len, GLA_QK), lambda b: (b, 0)),
            pl.BlockSpec((seq_len, GLA_QK), lambda b: (b, 1)),
            pl.BlockSpec((seq_len, GLA_V), lambda b: (b, (2 * qk_blk) // GLA_V)),
            pl.BlockSpec((seq_len, GLA_V), lambda b: (b, (2 * qk_blk) // GLA_V + 1)),
            pl.BlockSpec((seq_len, SMALL_W), lambda b: (b, 0)),
            _resident(wg.shape), _resident(bg.shape), _resident(gnorm.shape),
        ],
        out_specs=pl.BlockSpec((seq_len, GLA_V), lambda b: (b, 0)),
        out_shape=jax.ShapeDtypeStruct((t, GLA_V), BF16),
        scratch_shapes=[
            pltpu.VMEM((seq_len, 2 * GLA_QK), F32),
            pltpu.VMEM((seq_len, GLA_V), F32),
            pltpu.VMEM((seq_len, GLA_V), F32),
            pltpu.VMEM((2, GLA_HEADS // 2, LANES, 2 * GLA_DV), F32),
        ],
        compiler_params=pltpu.CompilerParams(dimension_semantics=("arbitrary",),
                                             vmem_limit_bytes=VMEM_LIMIT),
        name="gla_mixer",
    )(p_main, p_main, p_main, p_main, p_small, wg, bg, gnorm)


def _sgu_out_kernel(oa_ref, su_ref, sv_ref, lng_ref, lnb_ref, ws_ref, bs_ref, wo_ref, o_ref):
    tm = oa_ref.shape[0]
    for rg in _row_groups(tm, tm):
        u = _gelu_tanh(su_ref[rg, :].astype(F32))
        g = _gelu_tanh(sv_ref[rg, :].astype(F32))
        mu = jnp.mean(g, axis=-1, keepdims=True)
        gc = g - mu
        var = jnp.mean(gc * gc, axis=-1, keepdims=True)
        vv = (gc * lax.rsqrt(var + NORM_EPS) * lng_ref[...] + lnb_ref[...]).astype(BF16)
        rows = []
        for c in range((rg.stop - rg.start) // SGU_CHUNK):
            rs = slice(c * SGU_CHUNK, (c + 1) * SGU_CHUNK)
            cols = []
            for gi in range(SGU_GROUPS):
                gs = slice(gi * SGU_GROUP_DIM, (gi + 1) * SGU_GROUP_DIM)
                cols.append(_dot(ws_ref[gi], vv[rs, gs]))
            rows.append(jnp.concatenate(cols, axis=1) + bs_ref[...])
        ob = (u * jnp.concatenate(rows, axis=0)).astype(BF16)
        acc = _dot(oa_ref[rg, :], wo_ref[:GLA_V, :]) + _dot(ob, wo_ref[GLA_V:, :])
        o_ref[rg, :] = acc


def _sgu_out(o_a, p_main, ln_g, ln_b, w_s, b_full, w_out):
    t, d = o_a.shape[0], w_out.shape[1]
    su_blk = (2 * GLA_QK + 2 * GLA_V) // SGU_DIM
    return pl.pallas_call(
        _sgu_out_kernel,
        grid=(t // ROW_TILE,),
        in_specs=[
            pl.BlockSpec((ROW_TILE, GLA_V), lambda i: (i, 0)),
            pl.BlockSpec((ROW_TILE, SGU_DIM), lambda i: (i, su_blk)),
            pl.BlockSpec((ROW_TILE, SGU_DIM), lambda i: (i, su_blk + 1)),
            _resident(ln_g.shape), _resident(ln_b.shape), _resident(w_s.shape),
            _resident(b_full.shape), _resident(w_out.shape),
        ],
        out_specs=pl.BlockSpec((ROW_TILE, d), lambda i: (i, 0)),
        out_shape=jax.ShapeDtypeStruct((t, d), F32),
        compiler_params=pltpu.CompilerParams(dimension_semantics=("arbitrary",),
                                             vmem_limit_bytes=VMEM_LIMIT),
        name="sgu_out_proj",
    )(o_a, p_main, p_main, ln_g, ln_b, w_s, b_full, w_out)


GDN_SCALARS = 6
GDN_PAIRS_PER_STEP = 2
GDN_STEP_W = GDN_PAIRS_PER_STEP * 2 * GDN_DK
GDN_NORM_COL = 4 * GDN_HEADS
GDN_STEP_HEADS = 2 * GDN_PAIRS_PER_STEP
GDN_SCAN_UNROLL = 8
GDN_PRE_CHUNKS = 4


def _packed_product(x, y):
    lo = _iota(y.shape, 1) < CHUNK
    zero = jnp.zeros_like(y)
    return _dot(x, jnp.concatenate([jnp.where(lo, y, zero), jnp.where(lo, zero, y)], axis=0))


def _gdn_kernel(q_ref, k_ref, v_ref, sm_ref, tab_ref, esum_ref, eexp_ref, o_ref,
                tt_ref, aqd_ref, dec_ref, of_ref, ob_ref, st_ref, *, seq_len):
    n_chunks = seq_len // CHUNK
    blk = 2 * CHUNK
    npp = GDN_PAIRS_PER_STEP
    pw = 2 * GDN_DK

    pblk = GDN_PRE_CHUNKS * CHUNK
    tri = (_block_tri_ones(True, pblk), _block_tri_ones(False, pblk))
    incl = (_packed_tri_mask(True), _packed_tri_mask(False))
    strict = (_packed_tri_mask(True, strict=True), _packed_tri_mask(False, strict=True))
    diag = _iota((CHUNK, LANES), 0) == (_iota((CHUNK, LANES), 1) & (CHUNK - 1))
    lo_half = _iota((SUBLANES_F32, LANES), 1) < CHUNK
    eye = jnp.where(diag, 1.0, 0.0)

    def row_form(col_form):
        return jnp.sum(jnp.where(diag, col_form, 0.0), axis=0, keepdims=True)

    def precompute(it, carry):
        rows = pl.ds(pl.multiple_of(it * pblk, pblk), pblk)
        q2 = q_ref[rows, :]
        k2 = k_ref[rows, :]
        sm = sm_ref[rows, :]
        a_exp = jnp.exp(tab_ref[0:1, :])
        dt_b = tab_ref[1:2, :]

        lane = _iota(sm.shape, 1)
        gates = jnp.where(lane < 2 * GDN_HEADS, _sigmoid(sm), -a_exp * _softplus(sm + dt_b))
        qf = q2.astype(F32)
        kf = k2.astype(F32)
        ssq = _dot((qf * qf).astype(BF16), esum_ref[0]) + _dot((kf * kf).astype(BF16), esum_ref[1])
        inv = lax.rsqrt(ssq + NORM_EPS)
        is_qn = jnp.abs(2 * lane - (2 * GDN_NORM_COL + GDN_STEP_HEADS - 1)) < GDN_STEP_HEADS
        inv = jnp.where(is_qn, inv * (GDN_DK ** -0.5), inv)
        table = jnp.where(lane < GDN_NORM_COL, gates, inv)
        cum_f = _dot_exact_lhs(tri[0], table)
        cum_b = _dot_exact_lhs(tri[1], table)
        band = lane // GDN_HEADS
        table = jnp.where(band == 2, cum_f, jnp.where(band == 3, cum_b, table))
        hi, lo = _split(table)
        n_cum = 2 * npp * LANES
        e_cum = eexp_ref[0, :, :n_cum]
        ex_cum = _dot(hi, e_cum) + _dot(lo, e_cum)
        ex_rest = _dot(hi, eexp_ref[0, :, n_cum:])

        chains = []
        for cc in range(GDN_PRE_CHUNKS):
            rs = slice(cc * CHUNK, (cc + 1) * CHUNK)
            for pp in range(npp):
                ps = slice(pp * pw, (pp + 1) * pw)
                cols = [ex_cum[rs, (2 * pp + j) * LANES:(2 * pp + j + 1) * LANES] for j in range(2)]
                cols += [ex_rest[rs, (4 * pp + j) * LANES:(4 * pp + j + 1) * LANES] for j in range(4)]
                rq_c = cols[4]
                rk_c = cols[5]
                rk_r = row_form(rk_c)
                kbd = _pair_blockdiag(kf[rs, ps]).astype(BF16)
                gram = _dot_nt(jnp.concatenate([k2[rs, ps], q2[rs, ps]], axis=0), kbd)
                kk = gram[:CHUNK] * rk_c * rk_r
                qk = gram[CHUNK:] * rq_c * rk_r
                for dirn in range(2):
                    cum_c = cols[dirn]
                    beta_c = cols[2 + dirn]
                    tot_r = cum_c[CHUNK - 1:CHUNK, :] if dirn == 0 else cum_c[0:1, :]
                    cum_r = row_form(cum_c)
                    beta_r = row_form(beta_c)
                    decay = jnp.exp(jnp.where(incl[dirn], cum_c - cum_r, -1e30))
                    a = jnp.where(strict[dirn], kk * beta_c * decay, 0.0)
                    ci = GDN_PRE_CHUNKS * it + cc
                    d_q = jnp.where(diag, rq_c * jnp.exp(cum_c), 0.0)
                    aqd_ref[ci, dirn, pp] = jnp.concatenate([qk * decay, d_q], axis=1).astype(BF16)
                    tot8 = jnp.broadcast_to(tot_r, (SUBLANES_F32, LANES))
                    tot8r = pltpu.roll(tot8, CHUNK, axis=1)
                    dec_ref[ci, dirn, pp] = jnp.exp(jnp.concatenate(
                        [jnp.where(lo_half, tot8, tot8r), jnp.where(lo_half, tot8r, tot8)], axis=1))
                    chains.append(dict(ci=ci, pp=pp, dirn=dirn, pw_a=a, inv_m=eye - a,
                                       scale_u=beta_r, scale_w=beta_r * jnp.exp(cum_r) * rk_r,
                                       e_c=rk_c * jnp.exp(tot_r - cum_c)))
        for ch in chains:
            a_bf = ch["pw_a"].astype(BF16)
            ch["pw_a"] = _packed_product(a_bf, a_bf).astype(BF16)
        for _ in range(4):
            for ch in chains:
                lhs = jnp.concatenate([ch["inv_m"].astype(BF16), ch["pw_a"]], axis=0)
                both = _packed_product(lhs, ch["pw_a"])
                ch["inv_m"] = ch["inv_m"] + both[:CHUNK]
                ch["pw_a"] = both[CHUNK:].astype(BF16)
        for ch in chains:
            ch["inv_m"] = ch["inv_m"] + _packed_product(ch["inv_m"].astype(BF16), ch["pw_a"])
        for ch in chains:
            t_u = ch["inv_m"] * ch["scale_u"]
            t_w = ch["inv_m"] * ch["scale_w"]
            e_c = ch["e_c"]
            tt_ref[ch["ci"], ch["dirn"], ch["pp"]] = jnp.concatenate(
                [t_u, t_u * e_c, t_w, t_w * e_c], axis=0).astype(BF16)
        return carry

    lax.fori_loop(0, seq_len // pblk, precompute, 0)
    st_ref[...] = jnp.zeros(st_ref.shape, F32)

    o_refs = (of_ref, ob_ref)

    def scan(it, carry):
        heads = (slice(0, GDN_DV), slice(GDN_DV, 2 * GDN_DV))
        steps = []
        for sub in range(GDN_SCAN_UNROLL):
            cf = GDN_SCAN_UNROLL * it + sub
            cis = (cf, n_chunks - 1 - cf)
            chains = []
            for dirn in range(2):
                ci = cis[dirn]
                rows = pl.ds(pl.multiple_of(ci * CHUNK, CHUNK), CHUNK)
                q_c, k_c, v_c = q_ref[rows, :], k_ref[rows, :], v_ref[rows, :]
                for pp in range(npp):
                    ps = slice(pp * pw, (pp + 1) * pw)
                    chains.append(dict(dirn=dirn, pp=pp, rows=rows, tt=tt_ref[ci, dirn, pp],
                                       aqd=aqd_ref[ci, dirn, pp], dec=dec_ref[ci, dirn, pp],
                                       q=q_c[:, ps], k=k_c[:, ps], v=v_c[:, ps]))
            steps.append(chains)
        states = {(dirn, pp): [st_ref[dirn, pp, j] for j in range(2)]
                  for dirn in range(2) for pp in range(npp)}
        for chains in steps:
            for ch in chains:
                kbd = _pair_blockdiag(ch["k"].astype(F32)).astype(BF16)
                vbd = _pair_blockdiag(ch["v"].astype(F32)).astype(BF16)
                ch["uu"] = _dot(ch["tt"][:blk], vbd)
                ch["ww"] = _dot(ch["tt"][blk:], kbd)
        for chains in steps:
            for ch in chains:
                state = states[(ch["dirn"], ch["pp"])]
                lhs = jnp.concatenate([ch["ww"].astype(BF16), ch["q"]], axis=0)
                ch["prod"] = jnp.concatenate(
                    [_dot(lhs[:, hs], state[j].astype(BF16)) for j, hs in enumerate(heads)], axis=1)
            for ch in chains:
                state = states[(ch["dirn"], ch["pp"])]
                v_new_e = (ch["uu"][CHUNK:] - ch["prod"][CHUNK:blk]).astype(BF16)
                states[(ch["dirn"], ch["pp"])] = [
                    state[j] * ch["dec"][0:1, hs] + _dot_tn(ch["k"][:, hs], v_new_e[:, hs])
                    for j, hs in enumerate(heads)]
            for ch in chains:
                v_new = ch["uu"][:CHUNK] - ch["prod"][:CHUNK]
                rhs = jnp.concatenate([_pair_blockdiag(v_new), _pair_blockdiag(ch["prod"][blk:])], axis=0)
                ch["o"] = _dot(ch["aqd"], rhs.astype(BF16))
        for chains in steps:
            for dirn in range(2):
                mine = [ch for ch in chains if ch["dirn"] == dirn]
                o_refs[dirn][mine[0]["rows"], :] = jnp.concatenate([ch["o"] for ch in mine], axis=1)
        for (dirn, pp), state in states.items():
            for j in range(2):
                st_ref[dirn, pp, j] = state[j]
        return carry

    lax.fori_loop(0, n_chunks // GDN_SCAN_UNROLL, scan, 0)

    def finish(rb, carry):
        r0 = pl.multiple_of(rb * blk, blk)
        rows = pl.ds(r0, blk)
        o_ref[rows, :] = (of_ref[rows, :] + ob_ref[rows, :]).astype(o_ref.dtype)
        return carry

    lax.fori_loop(0, seq_len // blk, finish, 0)


def _gdn(p_main, p_small, tab, esum, eexp, batch, seq_len):
    t = batch * seq_len
    n_steps = GDN_QK // GDN_STEP_W
    n_chunks = seq_len // CHUNK
    npp = GDN_PAIRS_PER_STEP
    pw = 2 * GDN_DK
    sw = GDN_STEP_W
    kern = functools.partial(_gdn_kernel, seq_len=seq_len)
    return pl.pallas_call(
        kern,
        grid=(batch, n_steps),
        in_specs=[
            pl.BlockSpec((seq_len, sw), lambda b, p: (b, p)),
            pl.BlockSpec((seq_len, sw), lambda b, p: (b, GDN_QK // sw + p)),
            pl.BlockSpec((seq_len, sw), lambda b, p: (b, 2 * GDN_QK // sw + p)),
            pl.BlockSpec((seq_len, SMALL_W), lambda b, p: (b, 0)),
            pl.BlockSpec(tab.shape, lambda b, p: (0, 0)),
            pl.BlockSpec(esum.shape, lambda b, p: (0, 0, 0)),
            pl.BlockSpec((1, SMALL_W, npp * GDN_SCALARS * LANES), lambda b, p: (p, 0, 0)),
        ],
        out_specs=pl.BlockSpec((seq_len, sw), lambda b, p: (b, p)),
        out_shape=jax.ShapeDtypeStruct((t, GDN_V), BF16),
        scratch_shapes=[
            pltpu.VMEM((n_chunks, 2, npp, 4 * CHUNK, LANES), BF16),
            pltpu.VMEM((n_chunks, 2, npp, CHUNK, 2 * LANES), BF16),
            pltpu.VMEM((n_chunks, 2, npp, SUBLANES_F32, pw), F32),
            pltpu.VMEM((seq_len, sw), F32),
            pltpu.VMEM((seq_len, sw), F32),
            pltpu.VMEM((2, npp, 2, GDN_DK, GDN_DV), F32),
        ],
        compiler_params=pltpu.CompilerParams(dimension_semantics=("arbitrary", "arbitrary"),
                                             vmem_limit_bytes=VMEM_LIMIT),
        name="gdn_mixer",
    )(p_main, p_main, p_main, p_small, tab, esum, eexp)


def _pad_cols(w, width):
    return jnp.pad(w, ((0, 0), (0, width - w.shape[1])))


def _gla_params(ab_w_in, w_gate_fwd, b_gate_fwd, w_gate_bwd, b_gate_bwd):
    n_wide = 2 * GLA_QK + 2 * GLA_V
    lr0 = n_wide
    sg0 = lr0 + 2 * GLA_LOWRANK
    w_all = ab_w_in.astype(BF16)
    small = _pad_cols(w_all[:, lr0:sg0], SMALL_W)
    w = jnp.concatenate([w_all[:, :n_wide], w_all[:, sg0:sg0 + 2 * SGU_DIM]], axis=1)
    wg = jnp.zeros((SMALL_W, 2 * GLA_QK), F32)
    wg = wg.at[:GLA_LOWRANK, :GLA_QK].set(w_gate_fwd)
    wg = wg.at[GLA_LOWRANK:2 * GLA_LOWRANK, GLA_QK:].set(w_gate_bwd)
    bg = jnp.concatenate([b_gate_fwd, b_gate_bwd])[None, :]
    return w, small, wg.astype(BF16), bg


def _gdn_params(gdn_w_in, a_log_fwd, dt_bias_fwd, a_log_bwd, dt_bias_bwd):
    n_main = GDN_CONV_DIM + GDN_V
    w = gdn_w_in.astype(BF16)
    small = _pad_cols(w[:, n_main:], SMALL_W)
    pad = SMALL_W - 4 * GDN_HEADS
    zeros2 = jnp.zeros((2 * GDN_HEADS,), F32)
    a_log = jnp.concatenate([zeros2, a_log_fwd, a_log_bwd, jnp.zeros((pad,), F32)])
    dt_b = jnp.concatenate([zeros2, dt_bias_fwd, dt_bias_bwd, jnp.zeros((pad,), F32)])
    tab = jnp.zeros((SUBLANES_F32, SMALL_W), F32).at[0].set(a_log).at[1].set(dt_b)
    ch_head = jnp.arange(GDN_STEP_W) // GDN_DK
    col = jnp.arange(SMALL_W)
    esum_q = (col[None, :] == (GDN_NORM_COL + ch_head)[:, None])
    esum_k = (col[None, :] == (GDN_NORM_COL + GDN_STEP_HEADS + ch_head)[:, None])
    esum = jnp.stack([esum_q, esum_k]).astype(BF16)
    n_steps = GDN_QK // GDN_STEP_W
    lane = jnp.arange(GDN_PAIRS_PER_STEP * GDN_SCALARS * LANES)
    blk = lane // LANES
    n_cum_blk = 2 * GDN_PAIRS_PER_STEP
    pair = jnp.where(blk < n_cum_blk, blk // 2, (blk - n_cum_blk) // 4)
    quant = jnp.where(blk < n_cum_blk, blk % 2, 2 + (blk - n_cum_blk) % 4)
    local_head = 2 * pair + (lane % LANES) // CHUNK
    head = GDN_STEP_HEADS * jnp.arange(n_steps)[:, None] + local_head[None, :]
    src_gate = jnp.array([2 * GDN_HEADS, 3 * GDN_HEADS, 0, GDN_HEADS])
    src_norm = GDN_NORM_COL + GDN_STEP_HEADS * (quant - 4) + local_head
    src = jnp.where(quant[None, :] < 4, src_gate[jnp.minimum(quant, 3)][None, :] + head,
                    src_norm[None, :])
    eexp = (col[None, :, None] == src[:, None, :]).astype(BF16)
    return w, small, tab, esum, eexp


def kernel(x, norm_mix, norm_ffn, norm_final, ab_w_in, gla_w_gate_fwd, gla_b_gate_fwd, gla_w_gate_bwd, gla_b_gate_bwd, gla_norm, sgu_ln_g, sgu_ln_b, sgu_w_s, sgu_b_s, ab_w_out, gdn_w_in, gdn_conv_w, gdn_a_log_fwd, gdn_dt_bias_fwd, gdn_a_log_bwd, gdn_dt_bias_bwd, gdn_norm, gdn_w_out, ffn_w_up, ffn_conv_w, ffn_conv_b, ffn_w_down):
    batch, seq_len, d = x.shape
    t = batch * seq_len
    assert seq_len % ROW_TILE == 0 and seq_len % (2 * CHUNK) == 0
    h = x.reshape(t, d)

    w0, w0_small, wg, bg = _gla_params(ab_w_in[0], gla_w_gate_fwd[0], gla_b_gate_fwd[0],
                                       gla_w_gate_bwd[0], gla_b_gate_bwd[0])
    n_main0 = 2 * GLA_QK + 2 * GLA_V + 2 * SGU_DIM
    p0, s0 = _norm_proj(h, norm_mix[0][None, :], w0, w0_small, None, seq_len, n_main0, 0,
                        "gla_sgu_in_proj")
    o_a = _gla(p0, s0, wg, bg, gla_norm[0][None, :], batch, seq_len)
    b_full = jnp.repeat(sgu_b_s[0].T, SGU_GROUP_DIM, axis=1)
    mix0 = _sgu_out(o_a, p0, sgu_ln_g[0][None, :], sgu_ln_b[0][None, :],
                    sgu_w_s[0].astype(BF16), b_full, ab_w_out[0].astype(BF16))
    w_up_all = ffn_w_up.astype(BF16)
    w_down_all = ffn_w_down.astype(BF16)
    h = _ffn(h, norm_ffn[0][None, :], w_up_all, ffn_conv_w[0], ffn_conv_b[0][None, :], w_down_all,
             0, seq_len, None, "ffn0", add_in=mix0)

    w1, w1_small, tab, esum, eexp = _gdn_params(gdn_w_in[0], gdn_a_log_fwd[0], gdn_dt_bias_fwd[0],
                                                gdn_a_log_bwd[0], gdn_dt_bias_bwd[0])
    n_main1 = GDN_CONV_DIM + GDN_V
    p1, s1 = _norm_proj(h, norm_mix[1][None, :], w1, w1_small, gdn_conv_w[0], seq_len, n_main1,
                        GDN_CONV_DIM, "gdn_in_proj")
    o_g = _gdn(p1, s1, tab, esum, eexp, batch, seq_len)
    h = _ffn(h, norm_ffn[1][None, :], w_up_all, ffn_conv_w[1], ffn_conv_b[1][None, :], w_down_all,
             1, seq_len, norm_final[None, :], "gdn_out_ffn1",
             gdn_out=(o_g, p1, gdn_norm[0][None, :], gdn_w_out[0].astype(BF16)))
    return h.reshape(batch, seq_len, d)
```
